```python
import jax, jax.numpy as jnp
from jax import lax
import numpy as np

D_MODEL = 2048
BATCH = 16
SEQ = 2048
DEPTH = 1

N_HEADS = 16
HEAD_DIM = 64
N_KV = 4
HEADS_PER_KV = N_HEADS // N_KV
ATTN_WIDTH = N_HEADS * HEAD_DIM
KV_WIDTH = N_KV * HEAD_DIM
N_NSA_BRANCHES = 3
CMP_BLOCK = 32
CMP_STRIDE = 16
CMP_HIDDEN = 4 * HEAD_DIM
SEL_BLOCK = 64
N_SEL = 16
N_FORCED_LOCAL = 2
WINDOW = 512
Q_CHUNK = 32
FORCE_BONUS = 1e4
NEG_INF = -1e30
ROPE_THETA = 500000.0
ROPE_DIM = HEAD_DIM // 4
HEAD_OUT_WIDTH = D_MODEL // N_HEADS
MAX_POS_OFFSET = 1024
POOL_WINDOWS = (2, 4, 8, 16)
POOL_GROUPS = 4
POOL_WIDTH = 1024
POOL_GROUP_WIDTH = POOL_WIDTH // POOL_GROUPS
POOL_OUT_WIDTH = D_MODEL // POOL_GROUPS
N_MERGE_BRANCHES = 2
IN_SPLITS = (ATTN_WIDTH, KV_WIDTH, KV_WIDTH, KV_WIDTH, KV_WIDTH, KV_WIDTH, KV_WIDTH,
             N_HEADS * N_NSA_BRANCHES, POOL_WIDTH, N_MERGE_BRANCHES * D_MODEL)
IN_WIDTH = ATTN_WIDTH + 6 * KV_WIDTH + N_HEADS * N_NSA_BRANCHES + POOL_WIDTH + N_MERGE_BRANCHES * D_MODEL
N_EXPERT_GROUPS = 4
EXPERTS_PER_GROUP = 4
N_EXPERTS = N_EXPERT_GROUPS * EXPERTS_PER_GROUP
TOP_K_EXPERTS = 2
D_EXPERT = 256
RMS_EPS = 1e-6

kernel_name = "nsa_pool_hier_moe_hybrid_block"


def _rms(x, g):
    xf = x.astype(jnp.float32)
    inv = lax.rsqrt(jnp.mean(xf * xf, axis=-1, keepdims=True) + RMS_EPS)
    return (xf * inv * g.astype(jnp.float32)).astype(x.dtype)


def _rope_tables(pos):
    inv_freq = ROPE_THETA ** (-jnp.arange(0, ROPE_DIM, 2, dtype=jnp.float32) / ROPE_DIM)
    ang = pos.astype(jnp.float32)[..., None] * inv_freq
    return jnp.cos(ang)[:, :, None, :], jnp.sin(ang)[:, :, None, :]


def _partial_rope(x, cos, sin):
    half = ROPE_DIM // 2
    x1, x2, rest = x[..., :half], x[..., half:ROPE_DIM], x[..., ROPE_DIM:]
    c, s = cos.astype(x.dtype), sin.astype(x.dtype)
    return jnp.concatenate([x1 * c - x2 * s, x2 * c + x1 * s, rest], axis=-1)


def _masked_softmax(s, mask):
    s = jnp.where(mask, s.astype(jnp.float32), NEG_INF)
    return jax.nn.softmax(s, axis=-1) * mask.astype(jnp.float32)


def _compress(kv, pos_emb, w1, w2):
    B, S = kv.shape[:2]
    n_cmp = (S - CMP_BLOCK) // CMP_STRIDE + 1
    idx = np.arange(n_cmp)[:, None] * CMP_STRIDE + np.arange(CMP_BLOCK)[None, :]
    blocks = kv[:, idx] + pos_emb[None, None, :, None, :]
    flat = blocks.transpose(0, 1, 3, 2, 4).reshape(B, n_cmp, N_KV, CMP_BLOCK * HEAD_DIM)
    return jax.nn.silu(flat @ w1) @ w2


def _nsa_attention(q, kc, vc, ks, vs, kw, vw, gates):
    B, S = q.shape[:2]
    n_cmp = kc.shape[1]
    n_blk = S // SEL_BLOCK
    k_top = min(N_SEL, n_blk)
    n_chunks = S // Q_CHUNK
    scale = HEAD_DIM ** -0.5
    cmp_end = np.arange(n_cmp) * CMP_STRIDE + CMP_BLOCK - 1
    a0 = np.arange(n_cmp)[:, None] * CMP_STRIDE
    b0 = np.arange(n_blk)[None, :] * SEL_BLOCK
    ov = np.clip(np.minimum(a0 + CMP_BLOCK, b0 + SEL_BLOCK) - np.maximum(a0, b0), 0, None)
    overlap = jnp.asarray(ov / CMP_BLOCK, dtype=jnp.float32)

    kc_t, vc_t = kc.transpose(0, 2, 1, 3), vc.transpose(0, 2, 1, 3)
    ks_b = ks.reshape(B, n_blk, SEL_BLOCK, N_KV, HEAD_DIM).transpose(0, 3, 1, 2, 4)
    vs_b = vs.reshape(B, n_blk, SEL_BLOCK, N_KV, HEAD_DIM).transpose(0, 3, 1, 2, 4)
    pad = ((0, 0), (0, 0), (WINDOW, 0), (0, 0))
    kw_p = jnp.pad(kw.transpose(0, 2, 1, 3), pad)
    vw_p = jnp.pad(vw.transpose(0, 2, 1, 3), pad)
    q_ch = q.reshape(B, n_chunks, Q_CHUNK, N_KV, HEADS_PER_KV, HEAD_DIM).transpose(1, 0, 3, 4, 2, 5)
    g_ch = gates.reshape(B, n_chunks, Q_CHUNK, N_KV, HEADS_PER_KV, N_NSA_BRANCHES).transpose(1, 0, 3, 4, 2, 5)
    b_ix = jnp.arange(B)[:, None, None, None]
    g_ix = jnp.arange(N_KV)[None, :, None, None]
    blk = jnp.arange(n_blk)

    def chunk(args):
        c, qc, gc = args
        q0 = c * Q_CHUNK
        t = q0 + jnp.arange(Q_CHUNK)
        s = jnp.einsum("bgrqd,bgnd->bgrqn", qc, kc_t) * scale
        p_cmp = _masked_softmax(s, cmp_end[None, :] <= t[:, None])
        o_cmp = jnp.einsum("bgrqn,bgnd->bgrqd", p_cmp.astype(vc_t.dtype), vc_t)
        imp = jnp.einsum("bgrqn,nj->bgqj", p_cmp, overlap)
        dist = (t // SEL_BLOCK)[:, None] - blk[None, :]
        valid_blk = dist >= 0
        forced = (blk[None, :] == 0) | (valid_blk & (dist < N_FORCED_LOCAL))
        score = jnp.where(valid_blk, imp + FORCE_BONUS * forced.astype(jnp.float32), NEG_INF)
        _, sel = lax.top_k(score, k_top)
        ks_g = ks_b[b_ix, g_ix, sel].reshape(B, N_KV, Q_CHUNK, k_top * SEL_BLOCK, HEAD_DIM)
        vs_g = vs_b[b_ix, g_ix, sel].reshape(B, N_KV, Q_CHUNK, k_top * SEL_BLOCK, HEAD_DIM)
        kpos = (sel[..., None] * SEL_BLOCK + jnp.arange(SEL_BLOCK)).reshape(B, N_KV, Q_CHUNK, k_top * SEL_BLOCK)
        mask_s = (kpos <= t[:, None])[:, :, None]
        s = jnp.einsum("bgrqd,bgqkd->bgrqk", qc, ks_g) * scale
        p = _masked_softmax(s, mask_s)
        o_slc = jnp.einsum("bgrqk,bgqkd->bgrqd", p.astype(vs_g.dtype), vs_g)
        kw_c = lax.dynamic_slice_in_dim(kw_p, q0, WINDOW + Q_CHUNK, axis=2)
        vw_c = lax.dynamic_slice_in_dim(vw_p, q0, WINDOW + Q_CHUNK, axis=2)
        wpos = q0 - WINDOW + jnp.arange(WINDOW + Q_CHUNK)
        rel = t[:, None] - wpos[None, :]
        mask_w = (wpos[None, :] >= 0) & (rel >= 0) & (rel < WINDOW)
        s = jnp.einsum("bgrqd,bgkd->bgrqk", qc, kw_c) * scale
        p = _masked_softmax(s, mask_w)
        o_swa = jnp.einsum("bgrqk,bgkd->bgrqd", p.astype(vw_c.dtype), vw_c)
        return gc[..., 0:1] * o_cmp + gc[..., 1:2] * o_slc + gc[..., 2:3] * o_swa

    out = lax.map(chunk, (jnp.arange(n_chunks), q_ch, g_ch))
    return out.transpose(1, 0, 4, 2, 3, 5).reshape(B, S, N_HEADS, HEAD_DIM)


def _multiscale_pool(u):
    B, S, _ = u.shape
    ug = u.reshape(B, S, POOL_GROUPS, POOL_GROUP_WIDTH).astype(jnp.float32)
    csum = jnp.pad(jnp.cumsum(ug, axis=1), ((0, 0), (1, 0), (0, 0), (0, 0)))
    t = np.arange(S)
    outs = []
    for gi, w in enumerate(POOL_WINDOWS):
        lo = np.maximum(t + 1 - w, 0)
        cnt = jnp.asarray((t + 1 - lo).astype(np.float32))
        mean = (csum[:, t + 1, gi] - csum[:, lo, gi]) / cnt[None, :, None]
        outs.append(mean - ug[:, :, gi])
    return jnp.stack(outs, axis=2).astype(u.dtype)


def _mixer_sublayer(x, positions, attn_norm_g, w_in, q_norm_g, k_norm_cmp_g, k_norm_slc_g,
                    k_norm_swa_g, cmp_pos_emb_k, cmp_w1_k, cmp_w2_k, cmp_pos_emb_v, cmp_w1_v,
                    cmp_w2_v, w_head_out, w_pool, pool_scale, w_out):
    B, S, _ = x.shape
    h = _rms(x, attn_norm_g)
    proj = h @ w_in
    splits = np.cumsum(IN_SPLITS)[:-1].tolist()
    (q, k_cmp, v_cmp, k_slc, v_slc, k_swa, v_swa, branch_logits, pool_in,
     merge_logits) = jnp.split(proj, splits, axis=-1)
    q = q.reshape(B, S, N_HEADS, HEAD_DIM)
    kv_shape = (B, S, N_KV, HEAD_DIM)
    k_cmp, v_cmp, k_slc, v_slc, k_swa, v_swa = [a.reshape(kv_shape) for a in (k_cmp, v_cmp, k_slc, v_slc, k_swa, v_swa)]
    cos_t, sin_t = _rope_tables(positions)
    q = _partial_rope(_rms(q, q_norm_g), cos_t, sin_t)
    k_slc = _partial_rope(_rms(k_slc, k_norm_slc_g), cos_t, sin_t)
    k_swa = _partial_rope(_rms(k_swa, k_norm_swa_g), cos_t, sin_t)
    n_cmp = (S - CMP_BLOCK) // CMP_STRIDE + 1
    cmp_end = np.arange(n_cmp) * CMP_STRIDE + CMP_BLOCK - 1
    cos_c, sin_c = _rope_tables(positions[:, cmp_end])
    kc = _partial_rope(_rms(_compress(k_cmp, cmp_pos_emb_k, cmp_w1_k, cmp_w2_k), k_norm_cmp_g), cos_c, sin_c)
    vc = _compress(v_cmp, cmp_pos_emb_v, cmp_w1_v, cmp_w2_v)
    gates = jax.nn.sigmoid(branch_logits.reshape(B, S, N_HEADS, N_NSA_BRANCHES))
    o = _nsa_attention(q, kc, vc, k_slc, v_slc, k_swa, v_swa, gates)
    attn_branch = jnp.einsum("bshd,hde->bshe", o, w_head_out).reshape(B, S, D_MODEL)
    pooled = _multiscale_pool(pool_in)
    pool_branch = jnp.einsum("bsgc,gce->bsge", pooled, w_pool).reshape(B, S, D_MODEL) * pool_scale
    merge = jax.nn.sigmoid(merge_logits.reshape(B, S, N_MERGE_BRANCHES, D_MODEL))
    y = merge[:, :, 0] * attn_branch + merge[:, :, 1] * pool_branch
    return x + y @ w_out


def _moe_sublayer(x, ffn_norm_g, w_router_group, b_router_group, w_router_expert,
                  b_router_expert, w_expert_gate, w_expert_up, w_expert_down):
    B, S, D = x.shape
    ht = _rms(x, ffn_norm_g).reshape(B * S, D)
    T = ht.shape[0]
    g_prob = jax.nn.softmax((ht @ w_router_group).astype(jnp.float32) + b_router_group, axis=-1)
    p_group, g_sel = lax.top_k(g_prob, 1)
    e_logit = ((ht @ w_router_expert).astype(jnp.float32) + b_router_expert).reshape(T, N_EXPERT_GROUPS, EXPERTS_PER_GROUP)
    e_logit = e_logit[jnp.arange(T), g_sel[:, 0]]
    top_val, top_idx = lax.top_k(e_logit, TOP_K_EXPERTS)
    w_top = jax.nn.softmax(top_val, axis=-1) * p_group
    expert_id = g_sel * EXPERTS_PER_GROUP + top_idx
    combine = jnp.sum(jax.nn.one_hot(expert_id, N_EXPERTS, dtype=jnp.float32) * w_top[..., None], axis=1)
    combine = combine.astype(ht.dtype)
    y = jnp.zeros_like(ht)
    for e in range(N_EXPERTS):
        hid = jax.nn.silu(ht @ w_expert_gate[e]) * (ht @ w_expert_up[e])
        y = y + combine[:, e:e + 1] * (hid @ w_expert_down[e])
    return x + y.reshape(B, S, D)


def setup_inputs(seed: int = 0) -> dict:
    key = jax.random.key(seed)
    ks = jax.random.split(key, 26)
    f32 = jnp.float32
    L = DEPTH

    def nrm(k, shape, fan_in):
        return jax.random.normal(k, shape, f32) * (fan_in ** -0.5)

    def gain(k, shape):
        return 1.0 + 0.05 * jax.random.normal(k, shape, f32)

    x = jax.random.normal(ks[0], (BATCH, SEQ, D_MODEL), f32)
    positions = (jax.random.randint(ks[1], (BATCH, 1), 0, MAX_POS_OFFSET, dtype=jnp.int32)
                 + jnp.arange(SEQ, dtype=jnp.int32)[None, :])
    return {
        "x": x,
        "positions": positions,
        "attn_norm_g": gain(ks[2], (L, D_MODEL)),
        "w_in": nrm(ks[3], (L, D_MODEL, IN_WIDTH), D_MODEL),
        "q_norm_g": gain(ks[4], (L, HEAD_DIM)),
        "k_norm_cmp_g": gain(ks[5], (L, HEAD_DIM)),
        "k_norm_slc_g": gain(ks[6], (L, HEAD_DIM)),
        "k_norm_swa_g": gain(ks[7], (L, HEAD_DIM)),
        "cmp_pos_emb_k": 0.02 * jax.random.normal(ks[8], (L, CMP_BLOCK, HEAD_DIM), f32),
        "cmp_w1_k": nrm(ks[9], (L, CMP_BLOCK * HEAD_DIM, CMP_HIDDEN), CMP_BLOCK * HEAD_DIM),
        "cmp_w2_k": nrm(ks[10], (L, CMP_HIDDEN, HEAD_DIM), CMP_HIDDEN),
        "cmp_pos_emb_v": 0.02 * jax.random.normal(ks[11], (L, CMP_BLOCK, HEAD_DIM), f32),
        "cmp_w1_v": nrm(ks[12], (L, CMP_BLOCK * HEAD_DIM, CMP_HIDDEN), CMP_BLOCK * HEAD_DIM),
        "cmp_w2_v": nrm(ks[13], (L, CMP_HIDDEN, HEAD_DIM), CMP_HIDDEN),
        "w_head_out": nrm(ks[14], (L, N_HEADS, HEAD_DIM, HEAD_OUT_WIDTH), HEAD_DIM),
        "w_pool": nrm(ks[15], (L, POOL_GROUPS, POOL_GROUP_WIDTH, POOL_OUT_WIDTH), POOL_GROUP_WIDTH),
        "pool_scale": gain(ks[16], (L, D_MODEL)),
        "w_out": nrm(ks[17], (L, D_MODEL, D_MODEL), D_MODEL),
        "ffn_norm_g": gain(ks[18], (L, D_MODEL)),
        "w_router_group": nrm(ks[19], (L, D_MODEL, N_EXPERT_GROUPS), D_MODEL),
        "b_router_group": 0.01 * jax.random.normal(ks[20], (L, N_EXPERT_GROUPS), f32),
        "w_router_expert": nrm(ks[21], (L, D_MODEL, N_EXPERTS), D_MODEL),
        "b_router_expert": 0.01 * jax.random.normal(ks[22], (L, N_EXPERTS), f32),
        "w_expert_gate": nrm(ks[23], (L, N_EXPERTS, D_MODEL, D_EXPERT), D_MODEL),
        "w_expert_up": nrm(ks[24], (L, N_EXPERTS, D_MODEL, D_EXPERT), D_MODEL),
        "w_expert_down": nrm(ks[25], (L, N_EXPERTS, D_EXPERT, D_MODEL), D_EXPERT),
    }


def reference(x, positions, attn_norm_g, w_in, q_norm_g, k_norm_cmp_g, k_norm_slc_g,
              k_norm_swa_g, cmp_pos_emb_k, cmp_w1_k, cmp_w2_k, cmp_pos_emb_v, cmp_w1_v,
              cmp_w2_v, w_head_out, w_pool, pool_scale, w_out, ffn_norm_g, w_router_group,
              b_router_group, w_router_expert, b_router_expert, w_expert_gate, w_expert_up,
              w_expert_down):
    for layer in range(DEPTH):
        x = _mixer_sublayer(x, positions, attn_norm_g[layer], w_in[layer], q_norm_g[layer],
                            k_norm_cmp_g[layer], k_norm_slc_g[layer], k_norm_swa_g[layer],
                            cmp_pos_emb_k[layer], cmp_w1_k[layer], cmp_w2_k[layer],
                            cmp_pos_emb_v[layer], cmp_w1_v[layer], cmp_w2_v[layer],
                            w_head_out[layer], w_pool[layer], pool_scale[layer], w_out[layer])
        x = _moe_sublayer(x, ffn_norm_g[layer], w_router_group[layer], b_router_group[layer],
                          w_router_expert[layer], b_router_expert[layer], w_expert_gate[layer],
                          w_expert_up[layer], w_expert_down[layer])
    return x
```

```python
import functools

import jax
import jax.numpy as jnp
import numpy as np
from jax import lax
from jax.experimental import pallas as pl
from jax.experimental.pallas import tpu as pltpu

F32 = jnp.float32
BF16 = jnp.bfloat16

D_MODEL = 2048
N_HEADS = 16
HEAD_DIM = 64
N_KV = 4
HEADS_PER_KV = N_HEADS // N_KV
ATTN_WIDTH = N_HEADS * HEAD_DIM
KV_WIDTH = N_KV * HEAD_DIM
N_BRANCH = 3
CMP_BLOCK = 32
CMP_STRIDE = 16
CMP_HIDDEN = 4 * HEAD_DIM
SEL_BLOCK = 64
N_SEL = 16
N_FORCED_LOCAL = 2
WINDOW = 512
FORCE_BONUS = 1e4
NEG_INF = -1e30
ROPE_THETA = 500000.0
ROPE_DIM = HEAD_DIM // 4
HEAD_OUT_WIDTH = D_MODEL // N_HEADS
POOL_WINDOWS = (2, 4, 8, 16)
POOL_GROUPS = 4
POOL_WIDTH = 1024
POOL_GROUP_WIDTH = POOL_WIDTH // POOL_GROUPS
POOL_OUT_WIDTH = D_MODEL // POOL_GROUPS
N_EXPERT_GROUPS = 4
EXPERTS_PER_GROUP = 4
N_EXPERTS = N_EXPERT_GROUPS * EXPERTS_PER_GROUP
D_EXPERT = 256
RMS_EPS = 1e-6

LANES = 128
PROJ_WIDTH = 8192
PROJ_TN = 1024
COL_Q, COL_POOL, COL_M0, COL_M1, COL_KVA, COL_KVB = 0, 1, 2, 4, 6, 7
GATE_OFF = 2 * KV_WIDTH
VMEM_LIMIT = 56 * 1024 * 1024


def _cparams(sem):
    return pltpu.CompilerParams(dimension_semantics=sem, vmem_limit_bytes=VMEM_LIMIT)


def _nt_dot(a, b):
    return lax.dot_general(a, b, (((1,), (1,)), ((), ())), preferred_element_type=F32)


def _tn_dot(a, b):
    return lax.dot_general(a, b, (((0,), (0,)), ((), ())), preferred_element_type=F32)


def _split_dot(x, e):
    hi = x.astype(BF16)
    lo = (x - hi.astype(F32)).astype(BF16)
    return (jnp.dot(hi, e, preferred_element_type=F32)
            + jnp.dot(lo, e, preferred_element_type=F32))


def _proj_kernel(x_ref, g_ref, w_ref, o_ref, h_ref):
    @pl.when(pl.program_id(1) == 0)
    def _():
        x = x_ref[...]
        inv = lax.rsqrt(jnp.mean(x * x, axis=-1, keepdims=True) + RMS_EPS)
        h_ref[...] = (x * inv * g_ref[...]).astype(BF16)

    o_ref[...] = jnp.dot(h_ref[...], w_ref[...], preferred_element_type=F32)


def _proj(x2, g, w_packed, tm=1024):
    t = x2.shape[0]
    return pl.pallas_call(
        _proj_kernel,
        grid=(t // tm, PROJ_WIDTH // PROJ_TN),
        in_specs=[
            pl.BlockSpec((tm, D_MODEL), lambda i, j: (i, 0)),
            pl.BlockSpec((1, D_MODEL), lambda i, j: (0, 0)),
            pl.BlockSpec((D_MODEL, PROJ_TN), lambda i, j: (0, j)),
        ],
        out_specs=pl.BlockSpec((tm, PROJ_TN), lambda i, j: (i, j)),
        out_shape=jax.ShapeDtypeStruct((t, PROJ_WIDTH), F32),
        scratch_shapes=[pltpu.VMEM((tm, D_MODEL), BF16)],
        compiler_params=_cparams(("parallel", "arbitrary")),
        name="proj",
    )(x2, g, w_packed)


def _rope(xn, cos, sin):
    w = xn.shape[-1]
    lane = lax.broadcasted_iota(jnp.int32, xn.shape, 1) & (HEAD_DIM - 1)
    up = pltpu.roll(xn, w - ROPE_DIM // 2, axis=1)
    dn = pltpu.roll(xn, ROPE_DIM // 2, axis=1)
    return jnp.where(lane < ROPE_DIM // 2, xn * cos - up * sin,
                     jnp.where(lane < ROPE_DIM, xn * cos + dn * sin, xn))


def _head_norm_rope(x, gain, e_blk, cos, sin):
    w = x.shape[-1]
    ssq = _split_dot(x * x, e_blk)
    xn = x * lax.rsqrt(ssq * (1.0 / HEAD_DIM) + RMS_EPS) * gain
    reps = w // LANES
    return _rope(xn, jnp.concatenate([cos] * reps, axis=1), jnp.concatenate([sin] * reps, axis=1))


def _prep_kernel(q_ref, kva_ref, kvb_ref, pos_ref, f_ref, gq_ref, gs_ref, gw_ref, eq_ref, ek_ref,
                 qo_ref, kso_ref, vso_ref, kwo_ref, vwo_ref, kco_ref, vco_ref, go_ref):
    ang = pos_ref[...].astype(F32) * f_ref[...]
    cos, sin = jnp.cos(ang), jnp.sin(ang)
    scale = HEAD_DIM ** -0.5
    q = _head_norm_rope(q_ref[...], gq_ref[...], eq_ref[...], cos, sin) * scale
    for h in range(N_HEADS):
        qo_ref[0, h // HEADS_PER_KV, h % HEADS_PER_KV] = q[:, h * HEAD_DIM:(h + 1) * HEAD_DIM].astype(BF16)
    kva = kva_ref[...]
    kvb = kvb_ref[...]
    ks = _head_norm_rope(kva[:, 2 * KV_WIDTH:3 * KV_WIDTH], gs_ref[...], ek_ref[...], cos, sin)
    kw = _head_norm_rope(kvb[:, 0:KV_WIDTH], gw_ref[...], ek_ref[...], cos, sin)
    vs = kva[:, 3 * KV_WIDTH:4 * KV_WIDTH]
    vw = kvb[:, KV_WIDTH:2 * KV_WIDTH]
    kc = kva[:, 0:KV_WIDTH]
    vc = kva[:, KV_WIDTH:2 * KV_WIDTH]
    for g in range(N_KV):
        sl = slice(g * HEAD_DIM, (g + 1) * HEAD_DIM)
        kso_ref[0, g] = ks[:, sl].astype(BF16)
        vso_ref[0, g] = vs[:, sl].astype(BF16)
        kwo_ref[0, g] = kw[:, sl].astype(BF16)
        vwo_ref[0, g] = vw[:, sl].astype(BF16)
        kco_ref[0, g] = kc[:, sl]
        vco_ref[0, g] = vc[:, sl]
    go_ref[...] = jax.nn.sigmoid(kvb[:, GATE_OFF:GATE_OFF + LANES])


def _prep(proj, pos2, f_lane, gq, gs, gw, e_q, e_k, b, s, tm=512):
    nt = s // tm
    row = lambda bb, i: bb * nt + i
    kv_shape = jax.ShapeDtypeStruct((b, N_KV, s, HEAD_DIM), BF16)
    kv_spec = pl.BlockSpec((1, N_KV, tm, HEAD_DIM), lambda bb, i: (bb, 0, i, 0))
    const = lambda shape: pl.BlockSpec(shape, lambda bb, i: (0,) * len(shape))
    return pl.pallas_call(
        _prep_kernel,
        grid=(b, nt),
        in_specs=[
            pl.BlockSpec((tm, PROJ_TN), lambda bb, i: (row(bb, i), COL_Q)),
            pl.BlockSpec((tm, PROJ_TN), lambda bb, i: (row(bb, i), COL_KVA)),
            pl.BlockSpec((tm, PROJ_TN), lambda bb, i: (row(bb, i), COL_KVB)),
            pl.BlockSpec((tm, 1), lambda bb, i: (row(bb, i), 0)),
            const((1, LANES)), const((1, ATTN_WIDTH)), const((1, KV_WIDTH)), const((1, KV_WIDTH)),
            const((ATTN_WIDTH, ATTN_WIDTH)), const((KV_WIDTH, KV_WIDTH)),
        ],
        out_specs=[
            pl.BlockSpec((1, N_KV, HEADS_PER_KV, tm, HEAD_DIM), lambda bb, i: (bb, 0, 0, i, 0)),
            kv_spec, kv_spec, kv_spec, kv_spec, kv_spec, kv_spec,
            pl.BlockSpec((tm, LANES), lambda bb, i: (row(bb, i), 0)),
        ],
        out_shape=[
            jax.ShapeDtypeStruct((b, N_KV, HEADS_PER_KV, s, HEAD_DIM), BF16),
            kv_shape, kv_shape, kv_shape, kv_shape,
            jax.ShapeDtypeStruct((b, N_KV, s, HEAD_DIM), F32),
            jax.ShapeDtypeStruct((b, N_KV, s, HEAD_DIM), F32),
            jax.ShapeDtypeStruct((b * s, LANES), F32),
        ],
        compiler_params=_cparams(("parallel", "parallel")),
        name="prep",
    )(proj, proj, proj, pos2, f_lane, gq, gs, gw, e_q, e_k)


def _compress_one(x, pe_a, pe_b, w1_ref, w2_ref):
    half = CMP_STRIDE * HEAD_DIM
    nxt = pltpu.roll(x, x.shape[0] - 1, axis=0)
    a = (x + pe_a).astype(BF16)
    b = (nxt + pe_b).astype(BF16)
    hid = (jnp.dot(a, w1_ref[0:half, :], preferred_element_type=F32)
           + jnp.dot(b, w1_ref[half:2 * half, :], preferred_element_type=F32))
    hid = hid * jax.nn.sigmoid(hid)
    return jnp.dot(hid.astype(BF16), w2_ref[...], preferred_element_type=F32)


def _compress_kernel(xk_ref, xv_ref, pek_ref, pev_ref, w1k_ref, w2k_ref, w1v_ref, w2v_ref,
                     gk_ref, posc_ref, f_ref, kc_ref, vc_ref):
    ang = posc_ref[0].astype(F32) * f_ref[...]
    cos, sin = jnp.cos(ang), jnp.sin(ang)
    for g in range(N_KV):
        k = _compress_one(xk_ref[0, g], pek_ref[0:1, :], pek_ref[1:2, :], w1k_ref, w2k_ref)
        ssq = jnp.sum(k * k, axis=-1, keepdims=True)
        kn = k * lax.rsqrt(ssq * (1.0 / HEAD_DIM) + RMS_EPS) * gk_ref[...]
        kn = _rope(kn, cos, sin)
        kc_ref[0, g] = kn[:, 0:HEAD_DIM].astype(BF16)
        v = _compress_one(xv_ref[0, g], pev_ref[0:1, :], pev_ref[1:2, :], w1v_ref, w2v_ref)
        vc_ref[0, g] = v[:, 0:HEAD_DIM].astype(BF16)


def _compress(xk, xv, pek, pev, w1k, w2k, w1v, w2v, gk, posc, f_lane):
    b = xk.shape[0]
    nrow = xk.shape[2]
    half = CMP_STRIDE * HEAD_DIM
    const = lambda shape: pl.BlockSpec(shape, lambda bb: (0,) * len(shape))
    x_spec = pl.BlockSpec((1, N_KV, nrow, half), lambda bb: (bb, 0, 0, 0))
    o_spec = pl.BlockSpec((1, N_KV, nrow, HEAD_DIM), lambda bb: (bb, 0, 0, 0))
    o_shape = jax.ShapeDtypeStruct((b, N_KV, nrow, HEAD_DIM), BF16)
    return pl.pallas_call(
        _compress_kernel,
        grid=(b,),
        in_specs=[x_spec, x_spec, const((2, half)), const((2, half)),
                  const((2 * half, CMP_HIDDEN)), const((CMP_HIDDEN, LANES)),
                  const((2 * half, CMP_HIDDEN)), const((CMP_HIDDEN, LANES)),
                  const((1, LANES)), pl.BlockSpec((1, nrow, 1), lambda bb: (bb, 0, 0)), const((1, LANES))],
        out_specs=[o_spec, o_spec],
        out_shape=[o_shape, o_shape],
        compiler_params=_cparams(("parallel",)),
        name="compress",
    )(xk, xv, pek, pev, w1k, w2k, w1v, w2v, gk, posc, f_lane)


def _softmax_rows(s, mask):
    s = jnp.where(mask, s, NEG_INF)
    m = jnp.max(s, axis=-1, keepdims=True)
    p = jnp.where(mask, jnp.exp(s - m), 0.0)
    l = jnp.sum(p, axis=-1, keepdims=True)
    return p, 1.0 / jnp.where(l > 0.0, l, 1.0)


def _attn_kernel(q_ref, kc_ref, vc_ref, ks_ref, vs_ref, kw_ref, vw_ref, gate_ref, ovl_ref, exp_ref,
                 o_ref, *, tq, tk, seq):
    r = HEADS_PER_KV
    rows = r * tq
    n_cmp = kc_ref.shape[2]
    n_blk = seq // SEL_BLOCK
    t0 = pl.program_id(1) * tq
    gates = gate_ref[...]

    def t_of_row(shape):
        return t0 + (lax.broadcasted_iota(jnp.int32, shape, 0) & (tq - 1))

    for g in range(N_KV):
        qg = q_ref[0, g].reshape(rows, HEAD_DIM)

        s = _nt_dot(qg, kc_ref[0, g])
        n_idx = lax.broadcasted_iota(jnp.int32, (rows, n_cmp), 1)
        mask = (n_idx * CMP_STRIDE + (CMP_BLOCK - 1)) <= t_of_row((rows, n_cmp))
        p, inv_l = _softmax_rows(s, mask)
        p = p * inv_l
        o_cmp = jnp.dot(p.astype(BF16), vc_ref[0, g], preferred_element_type=F32)

        p_sum = p[0:tq] + p[tq:2 * tq] + p[2 * tq:3 * tq] + p[3 * tq:4 * tq]
        hi = p_sum.astype(BF16)
        lo = (p_sum - hi.astype(F32)).astype(BF16)
        imp = _nt_dot(ovl_ref[...], hi) + _nt_dot(ovl_ref[...], lo)
        j_idx = lax.broadcasted_iota(jnp.int32, (n_blk, tq), 0)
        t_idx = t0 + lax.broadcasted_iota(jnp.int32, (n_blk, tq), 1)
        dist = jnp.right_shift(t_idx, SEL_BLOCK.bit_length() - 1) - j_idx
        valid = dist >= 0
        forced = (j_idx == 0) | (valid & (dist < N_FORCED_LOCAL))
        score = jnp.where(valid, imp + FORCE_BONUS * forced.astype(F32), NEG_INF)
        cnt = jnp.zeros((n_blk, tq), F32)
        for jp in range(n_blk):
            row = score[jp:jp + 1, :]
            beats = (row > score) | ((row == score) & (j_idx > jp))
            cnt = cnt + beats.astype(F32)
        sel = ((cnt < float(min(N_SEL, n_blk))) & valid).astype(BF16)

        def sel_body(kt, carry):
            m_i, l_i, acc = carry
            k0 = pl.multiple_of(kt * tk, tk)
            s = _nt_dot(qg, ks_ref[0, g, pl.ds(k0, tk), :])
            bm = _tn_dot(sel, exp_ref[:, pl.ds(k0, tk)])
            bm = jnp.concatenate([bm] * r, axis=0)
            k_idx = k0 + lax.broadcasted_iota(jnp.int32, (rows, tk), 1)
            mask = (bm > 0.5) & (k_idx <= t_of_row((rows, tk)))
            s = jnp.where(mask, s, NEG_INF)
            m_new = jnp.maximum(m_i, jnp.max(s, axis=-1, keepdims=True))
            alpha = jnp.exp(m_i - m_new)
            p = jnp.where(mask, jnp.exp(s - m_new), 0.0)
            l_new = alpha * l_i + jnp.sum(p, axis=-1, keepdims=True)
            acc = alpha * acc + jnp.dot(p.astype(BF16), vs_ref[0, g, pl.ds(k0, tk), :],
                                        preferred_element_type=F32)
            return m_new, l_new, acc

        init = (jnp.full((rows, 1), NEG_INF, F32), jnp.zeros((rows, 1), F32),
                jnp.zeros((rows, HEAD_DIM), F32))
        _, l_s, acc_s = lax.fori_loop(0, t0 // tk + 1, sel_body, init)
        o_slc = acc_s / jnp.where(l_s > 0.0, l_s, 1.0)

        span = WINDOW + tq
        w0 = pl.multiple_of(jnp.maximum(t0 - WINDOW, 0), tq)
        s = _nt_dot(qg, kw_ref[0, g, pl.ds(w0, span), :])
        rel = t_of_row((rows, span)) - (w0 + lax.broadcasted_iota(jnp.int32, (rows, span), 1))
        p, inv_l = _softmax_rows(s, (rel >= 0) & (rel < WINDOW))
        o_swa = jnp.dot(p.astype(BF16), vw_ref[0, g, pl.ds(w0, span), :],
                        preferred_element_type=F32) * inv_l

        def gate_col(c):
            cols = [gates[:, (g * r + hh) * N_BRANCH + c:(g * r + hh) * N_BRANCH + c + 1] for hh in range(r)]
            return jnp.concatenate(cols, axis=0)

        o = gate_col(0) * o_cmp + gate_col(1) * o_slc + gate_col(2) * o_swa
        for hh in range(r):
            o_ref[0, g * r + hh] = o[hh * tq:(hh + 1) * tq].astype(BF16)


def _attn(q, kc, vc, ks, vs, kw, vw, gates, ovl_t, expand, tq=128, tk=512):
    b, _, _, s, _ = q.shape
    n_cmp = kc.shape[2]
    nq = s // tq
    full = lambda shape: pl.BlockSpec((1,) + shape, lambda bb, i: (bb,) + (0,) * len(shape))
    const = lambda shape: pl.BlockSpec(shape, lambda bb, i: (0,) * len(shape))
    return pl.pallas_call(
        functools.partial(_attn_kernel, tq=tq, tk=tk, seq=s),
        grid=(b, nq),
        in_specs=[
            pl.BlockSpec((1, N_KV, HEADS_PER_KV, tq, HEAD_DIM), lambda bb, i: (bb, 0, 0, i, 0)),
            full((N_KV, n_cmp, HEAD_DIM)), full((N_KV, n_cmp, HEAD_DIM)),
            full((N_KV, s, HEAD_DIM)), full((N_KV, s, HEAD_DIM)),
            full((N_KV, s, HEAD_DIM)), full((N_KV, s, HEAD_DIM)),
            pl.BlockSpec((tq, LANES), lambda bb, i: (bb * nq + i, 0)),
            const(ovl_t.shape), const(expand.shape),
        ],
        out_specs=pl.BlockSpec((1, N_HEADS, tq, HEAD_DIM), lambda bb, i: (bb, 0, i, 0)),
        out_shape=jax.ShapeDtypeStruct((b, N_HEADS, s, HEAD_DIM), BF16),
        compiler_params=_cparams(("parallel", "arbitrary")),
        name="attn",
    )(q, kc, vc, ks, vs, kw, vw, gates, ovl_t, expand)


POOL_HALO = 16


def _mix_kernel(x_ref, m0_ref, m1_ref, u_ref, halo_ref, o_ref, wh_ref, wp_ref, ps_ref, wo_ref, y_ref,
                *, tm, seq):
    it = pl.program_id(0) % (seq // tm)
    u = u_ref[...]
    halo = jnp.where(it == 0, 0.0, halo_ref[...])
    ext = jnp.concatenate([halo, u], axis=0)
    sums = []
    acc = ext
    for sh in (1, 2, 4, 8):
        acc = acc + pltpu.roll(acc, sh, axis=0)
        sums.append(acc)
    t_in = it * tm + lax.broadcasted_iota(jnp.int32, (tm, 1), 0)
    pool_parts = []
    for gi, win in enumerate(POOL_WINDOWS):
        sl = slice(gi * POOL_GROUP_WIDTH, (gi + 1) * POOL_GROUP_WIDTH)
        cnt = jnp.minimum(t_in + 1, win).astype(F32)
        pooled = sums[gi][POOL_HALO:, sl] / cnt - u[:, sl]
        pool_parts.append(jnp.dot(pooled.astype(BF16), wp_ref[gi], preferred_element_type=F32))
    pool_branch = jnp.concatenate(pool_parts, axis=1) * ps_ref[...]
    attn_branch = jnp.concatenate(
        [jnp.dot(o_ref[0, h], wh_ref[h], preferred_element_type=F32) for h in range(N_HEADS)], axis=1)
    y = jax.nn.sigmoid(m0_ref[...]) * attn_branch + jax.nn.sigmoid(m1_ref[...]) * pool_branch
    y_ref[...] = x_ref[...] + jnp.dot(y.astype(BF16), wo_ref[...], preferred_element_type=F32)


def _mix(x2, proj, o, wh, wp, ps, wo, s, tm=256):
    t = x2.shape[0]
    nt = s // tm
    hb = tm // POOL_HALO
    const = lambda shape: pl.BlockSpec(shape, lambda i: (0,) * len(shape))
    return pl.pallas_call(
        functools.partial(_mix_kernel, tm=tm, seq=s),
        grid=(t // tm,),
        in_specs=[
            pl.BlockSpec((tm, D_MODEL), lambda i: (i, 0)),
            pl.BlockSpec((tm, D_MODEL), lambda i: (i, COL_M0 // 2)),
            pl.BlockSpec((tm, D_MODEL), lambda i: (i, COL_M1 // 2)),
            pl.BlockSpec((tm, POOL_WIDTH), lambda i: (i, COL_POOL)),
            pl.BlockSpec((POOL_HALO, POOL_WIDTH), lambda i: (jnp.maximum(i * hb - 1, 0), COL_POOL)),
            pl.BlockSpec((1, N_HEADS, tm, HEAD_DIM), lambda i: (i // nt, 0, i % nt, 0)),
            const(wh.shape), const(wp.shape), const(ps.shape), const(wo.shape),
        ],
        out_specs=pl.BlockSpec((tm, D_MODEL), lambda i: (i, 0)),
        out_shape=jax.ShapeDtypeStruct((t, D_MODEL), F32),
        compiler_params=_cparams(("parallel",)),
        name="mix",
    )(x2, proj, proj, proj, proj, o, wh, wp, ps, wo)


ROUTE_EXPERT_OFF = N_EXPERT_GROUPS


def _route(logits):
    lane = lax.broadcasted_iota(jnp.int32, logits.shape, 1)
    is_g = lane < N_EXPERT_GROUPS
    gmax = jnp.max(jnp.where(is_g, logits, -jnp.inf), axis=-1, keepdims=True)
    ge = jnp.where(is_g, jnp.exp(logits - gmax), 0.0)
    gprob = ge / jnp.sum(ge, axis=-1, keepdims=True)
    p_group = jnp.max(jnp.where(is_g, gprob, -jnp.inf), axis=-1, keepdims=True)
    g_sel = jnp.min(jnp.where(is_g & (gprob == p_group), lane, LANES), axis=-1, keepdims=True)
    e_lane = lane - ROUTE_EXPERT_OFF
    in_grp = ((e_lane >= 0) & (e_lane < N_EXPERTS)
              & (jnp.right_shift(e_lane, EXPERTS_PER_GROUP.bit_length() - 1) == g_sel))
    el = jnp.where(in_grp, logits, -jnp.inf)
    v1 = jnp.max(el, axis=-1, keepdims=True)
    i1 = jnp.min(jnp.where(in_grp & (el == v1), lane, LANES), axis=-1, keepdims=True)
    el2 = jnp.where(lane == i1, -jnp.inf, el)
    v2 = jnp.max(el2, axis=-1, keepdims=True)
    i2 = jnp.min(jnp.where(in_grp & (lane != i1) & (el2 == v2), lane, LANES), axis=-1, keepdims=True)
    e2 = jnp.exp(v2 - v1)
    den = 1.0 + e2
    return (jnp.where(lane == i1, (1.0 / den) * p_group, 0.0)
            + jnp.where(lane == i2, (e2 / den) * p_group, 0.0))


def _moe_kernel(x_ref, g_ref, wr_ref, br_ref, wg_ref, wu_ref, wd_ref, y_ref, h_ref, c_ref):
    e = pl.program_id(1)

    @pl.when(e == 0)
    def _():
        x = x_ref[...]
        inv = lax.rsqrt(jnp.mean(x * x, axis=-1, keepdims=True) + RMS_EPS)
        ht = x * inv * g_ref[...]
        h_ref[...] = ht.astype(BF16)
        logits = jnp.dot(ht, wr_ref[...], preferred_element_type=F32,
                         precision=lax.Precision.HIGHEST) + br_ref[...]
        c_ref[...] = _route(logits)

    h = h_ref[...]
    a = jnp.dot(h, wg_ref[0], preferred_element_type=F32)
    hid = a * jax.nn.sigmoid(a) * jnp.dot(h, wu_ref[0], preferred_element_type=F32)
    lane = lax.broadcasted_iota(jnp.int32, c_ref.shape, 1)
    c = jnp.sum(jnp.where(lane == e + ROUTE_EXPERT_OFF, c_ref[...], 0.0), axis=-1, keepdims=True)
    contrib = jnp.dot((hid * c).astype(BF16), wd_ref[0], preferred_element_type=F32)

    @pl.when(e == 0)
    def _():
        y_ref[...] = x_ref[...] + contrib

    @pl.when(e != 0)
    def _():
        y_ref[...] += contrib


def _moe(x1, g, wr, br, wg, wu, wd, tm=512):
    t = x1.shape[0]
    const = lambda shape: pl.BlockSpec(shape, lambda i, e: (0,) * len(shape))
    return pl.pallas_call(
        _moe_kernel,
        grid=(t // tm, N_EXPERTS),
        in_specs=[
            pl.BlockSpec((tm, D_MODEL), lambda i, e: (i, 0)),
            const((1, D_MODEL)), const((D_MODEL, LANES)), const((1, LANES)),
            pl.BlockSpec((1, D_MODEL, D_EXPERT), lambda i, e: (e, 0, 0)),
            pl.BlockSpec((1, D_MODEL, D_EXPERT), lambda i, e: (e, 0, 0)),
            pl.BlockSpec((1, D_EXPERT, D_MODEL), lambda i, e: (e, 0, 0)),
        ],
        out_specs=pl.BlockSpec((tm, D_MODEL), lambda i, e: (i, 0)),
        out_shape=jax.ShapeDtypeStruct((t, D_MODEL), F32),
        scratch_shapes=[pltpu.VMEM((tm, D_MODEL), BF16), pltpu.VMEM((tm, LANES), F32)],
        compiler_params=_cparams(("parallel", "arbitrary")),
        name="moe",
    )(x1, g, wr, br, wg, wu, wd)


def _rope_lane_freq():
    inv_freq = ROPE_THETA ** (-jnp.arange(0, ROPE_DIM, 2, dtype=F32) / ROPE_DIM)
    d = np.arange(LANES) % HEAD_DIM
    f = jnp.where(jnp.asarray(d < ROPE_DIM), inv_freq[np.asarray(d % (ROPE_DIM // 2))], 0.0)
    return f.reshape(1, LANES).astype(F32)


def _head_block_ones(width):
    idx = np.arange(width) // HEAD_DIM
    return jnp.asarray(idx[:, None] == idx[None, :], dtype=BF16)


def _overlap_t(n_cmp_pad, n_blk):
    a0 = np.arange(n_cmp_pad)[:, None] * CMP_STRIDE
    b0 = np.arange(n_blk)[None, :] * SEL_BLOCK
    ov = np.clip(np.minimum(a0 + CMP_BLOCK, b0 + SEL_BLOCK) - np.maximum(a0, b0), 0, None) / CMP_BLOCK
    return jnp.asarray(ov.T, dtype=BF16)


def _block_expand(n_blk, s):
    return jnp.asarray(np.arange(n_blk)[:, None] == (np.arange(s)[None, :] // SEL_BLOCK), dtype=BF16)


def _layer(x, positions, attn_norm_g, w_in, q_norm_g, k_norm_cmp_g, k_norm_slc_g, k_norm_swa_g,
           cmp_pos_emb_k, cmp_w1_k, cmp_w2_k, cmp_pos_emb_v, cmp_w1_v, cmp_w2_v, w_head_out, w_pool,
           pool_scale, w_out, ffn_norm_g, w_router_group, b_router_group, w_router_expert,
           b_router_expert, w_expert_gate, w_expert_up, w_expert_down):
    b, s, d = x.shape
    t = b * s
    x2 = x.reshape(t, d)

    o_q, o_kv, o_g = ATTN_WIDTH, ATTN_WIDTH + 6 * KV_WIDTH, ATTN_WIDTH + 6 * KV_WIDTH + N_HEADS * N_BRANCH
    o_p = o_g + POOL_WIDTH
    used = 2 * ATTN_WIDTH + 2 * D_MODEL + 6 * KV_WIDTH + N_HEADS * N_BRANCH
    w_packed = jnp.concatenate(
        [w_in[:, :o_q], w_in[:, o_g:o_p], w_in[:, o_p:], w_in[:, o_q:o_kv], w_in[:, o_kv:o_g],
         jnp.zeros((d, PROJ_WIDTH - used), w_in.dtype)], axis=1).astype(BF16)
    proj = _proj(x2, attn_norm_g.reshape(1, d), w_packed)

    f_lane = _rope_lane_freq()
    tile_h = lambda g_, n: jnp.tile(g_, n).reshape(1, n * HEAD_DIM)
    q, ks, vs, kw, vw, kc_raw, vc_raw, gates = _prep(
        proj, positions.reshape(t, 1), f_lane, tile_h(q_norm_g, N_HEADS), tile_h(k_norm_slc_g, N_KV),
        tile_h(k_norm_swa_g, N_KV), _head_block_ones(ATTN_WIDTH), _head_block_ones(KV_WIDTH), b, s)

    n_row = s // CMP_STRIDE
    half = CMP_STRIDE * HEAD_DIM
    xk = kc_raw.reshape(b, N_KV, n_row, half)
    xv = vc_raw.reshape(b, N_KV, n_row, half)
    pad2 = lambda w2: jnp.pad(w2, ((0, 0), (0, LANES - HEAD_DIM))).astype(BF16)
    cmp_end = np.minimum(np.arange(n_row) * CMP_STRIDE + CMP_BLOCK - 1, s - 1)
    posc = positions[:, cmp_end].reshape(b, n_row, 1)
    gk = jnp.pad(k_norm_cmp_g, (0, LANES - HEAD_DIM)).reshape(1, LANES)
    kc, vc = _compress(xk, xv, cmp_pos_emb_k.reshape(2, half), cmp_pos_emb_v.reshape(2, half),
                       cmp_w1_k.astype(BF16), pad2(cmp_w2_k), cmp_w1_v.astype(BF16), pad2(cmp_w2_v),
                       gk, posc, f_lane)

    n_blk = s // SEL_BLOCK
    o = _attn(q, kc, vc, ks, vs, kw, vw, gates, _overlap_t(n_row, n_blk), _block_expand(n_blk, s))

    x1 = _mix(x2, proj, o, w_head_out.astype(BF16), w_pool.astype(BF16), pool_scale.reshape(1, d),
              w_out.astype(BF16), s)

    wr = jnp.concatenate([w_router_group, w_router_expert,
                          jnp.zeros((d, LANES - N_EXPERT_GROUPS - N_EXPERTS), F32)], axis=1)
    br = jnp.concatenate([b_router_group, b_router_expert,
                          jnp.zeros((LANES - N_EXPERT_GROUPS - N_EXPERTS,), F32)]).reshape(1, LANES)
    x_out = _moe(x1, ffn_norm_g.reshape(1, d), wr, br, w_expert_gate.astype(BF16),
                 w_expert_up.astype(BF16), w_expert_down.astype(BF16))
    return x_out.reshape(b, s, d)


def kernel(x, positions, attn_norm_g, w_in, q_norm_g, k_norm_cmp_g, k_norm_slc_g, k_norm_swa_g,
           cmp_pos_emb_k, cmp_w1_k, cmp_w2_k, cmp_pos_emb_v, cmp_w1_v, cmp_w2_v, w_head_out, w_pool,
           pool_scale, w_out, ffn_norm_g, w_router_group, b_router_group, w_router_expert,
           b_router_expert, w_expert_gate, w_expert_up, w_expert_down):
    params = (attn_norm_g, w_in, q_norm_g, k_norm_cmp_g, k_norm_slc_g, k_norm_swa_g, cmp_pos_emb_k,
              cmp_w1_k, cmp_w2_k, cmp_pos_emb_v, cmp_w1_v, cmp_w2_v, w_head_out, w_pool, pool_scale,
              w_out, ffn_norm_g, w_router_group, b_router_group, w_router_expert, b_router_expert,
              w_expert_gate, w_expert_up, w_expert_down)
    for layer in range(attn_norm_g.shape[0]):
        x = _layer(x, positions, *[p[layer] for p in params])
    return x
```

```python
import functools

import jax
import jax.numpy as jnp
import numpy as np
from jax import lax
from jax.experimental import pallas as pl
from jax.experimental.pallas import tpu as pltpu

F32 = jnp.float32
BF16 = jnp.bfloat16

D_MODEL = 2048
N_HEADS = 16
HEAD_DIM = 64
N_KV = 4
HEADS_PER_KV = N_HEADS // N_KV
ATTN_WIDTH = N_HEADS * HEAD_DIM
KV_WIDTH = N_KV * HEAD_DIM
N_BRANCH = 3
CMP_BLOCK = 32
CMP_STRIDE = 16
CMP_HIDDEN = 4 * HEAD_DIM
SEL_BLOCK = 64
N_SEL = 16
N_FORCED_LOCAL = 2
WINDOW = 512
FORCE_BONUS = 1e4
NEG_INF = -1e30
ROPE_THETA = 500000.0
ROPE_DIM = HEAD_DIM // 4
HEAD_OUT_WIDTH = D_MODEL // N_HEADS
POOL_WINDOWS = (2, 4, 8, 16)
POOL_GROUPS = 4
POOL_WIDTH = 1024
POOL_GROUP_WIDTH = POOL_WIDTH // POOL_GROUPS
POOL_OUT_WIDTH = D_MODEL // POOL_GROUPS
N_EXPERT_GROUPS = 4
EXPERTS_PER_GROUP = 4
N_EXPERTS = N_EXPERT_GROUPS * EXPERTS_PER_GROUP
D_EXPERT = 256
RMS_EPS = 1e-6

LANES = 128
Q_SCALE = HEAD_DIM ** -0.5 * float(np.log2(np.e))
BLK_LANE0 = HEAD_DIM
PAD_LANE = HEAD_DIM + 32
PROJ_WIDTH = 8192
PROJ_TN = 1024
COL_Q, COL_POOL, COL_M0, COL_M1, COL_KVA, COL_KVB = 0, 1, 2, 4, 6, 7
GATE_OFF = 2 * KV_WIDTH
VMEM_LIMIT = 56 * 1024 * 1024


def _cparams(sem):
    return pltpu.CompilerParams(dimension_semantics=sem, vmem_limit_bytes=VMEM_LIMIT)


def _nt_dot(a, b):
    return lax.dot_general(a, b, (((1,), (1,)), ((), ())), preferred_element_type=F32)


def _tn_dot(a, b):
    return lax.dot_general(a, b, (((0,), (0,)), ((), ())), preferred_element_type=F32)


def _split_dot(x, e):
    hi = x.astype(BF16)
    lo = (x - hi.astype(F32)).astype(BF16)
    return (jnp.dot(hi, e, preferred_element_type=F32)
            + jnp.dot(lo, e, preferred_element_type=F32))


def _proj_kernel(x_ref, g_ref, w_ref, o_ref, h_ref):
    @pl.when(pl.program_id(1) == 0)
    def _():
        x = x_ref[...]
        inv = lax.rsqrt(jnp.mean(x * x, axis=-1, keepdims=True) + RMS_EPS)
        h_ref[...] = (x * inv * g_ref[...]).astype(BF16)

    o_ref[...] = jnp.dot(h_ref[...], w_ref[...], preferred_element_type=F32)


def _proj(x2, g, w_packed, tm=1024):
    t = x2.shape[0]
    return pl.pallas_call(
        _proj_kernel,
        grid=(t // tm, PROJ_WIDTH // PROJ_TN),
        in_specs=[
            pl.BlockSpec((tm, D_MODEL), lambda i, j: (i, 0)),
            pl.BlockSpec((1, D_MODEL), lambda i, j: (0, 0)),
            pl.BlockSpec((D_MODEL, PROJ_TN), lambda i, j: (0, j)),
        ],
        out_specs=pl.BlockSpec((tm, PROJ_TN), lambda i, j: (i, j)),
        out_shape=jax.ShapeDtypeStruct((t, PROJ_WIDTH), F32),
        scratch_shapes=[pltpu.VMEM((tm, D_MODEL), BF16)],
        compiler_params=_cparams(("parallel", "arbitrary")),
        name="proj",
    )(x2, g, w_packed)


def _rope(xn, cos, sin):
    w = xn.shape[-1]
    lane = lax.broadcasted_iota(jnp.int32, xn.shape, 1) & (HEAD_DIM - 1)
    up = pltpu.roll(xn, w - ROPE_DIM // 2, axis=1)
    dn = pltpu.roll(xn, ROPE_DIM // 2, axis=1)
    return jnp.where(lane < ROPE_DIM // 2, xn * cos - up * sin,
                     jnp.where(lane < ROPE_DIM, xn * cos + dn * sin, xn))


def _head_norm_rope(x, gain, e_blk, cos, sin):
    w = x.shape[-1]
    ssq = _split_dot(x * x, e_blk)
    xn = x * lax.rsqrt(ssq * (1.0 / HEAD_DIM) + RMS_EPS) * gain
    reps = w // LANES
    return _rope(xn, jnp.concatenate([cos] * reps, axis=1), jnp.concatenate([sin] * reps, axis=1))


def _widen_head(x, h, upper):
    col = x[:, (h // 2) * LANES:(h // 2 + 1) * LANES]
    if h % 2:
        col = pltpu.roll(col, HEAD_DIM, axis=1)
    lane = lax.broadcasted_iota(jnp.int32, col.shape, 1)
    return jnp.where(lane < HEAD_DIM, col, upper)


def _prep_kernel(q_ref, kva_ref, kvb_ref, pos_ref, f_ref, gq_ref, gs_ref, gw_ref, eq_ref, ek_ref,
                 qo_ref, kso_ref, vso_ref, kwo_ref, vwo_ref, kco_ref, vco_ref, go_ref):
    tm = q_ref.shape[0]
    ang = pos_ref[...].astype(F32) * f_ref[...]
    cos, sin = jnp.cos(ang), jnp.sin(ang)
    q = _head_norm_rope(q_ref[...], gq_ref[...], eq_ref[...], cos, sin) * Q_SCALE
    lane = lax.broadcasted_iota(jnp.int32, (tm, LANES), 1)
    t_in = pl.program_id(1) * tm + lax.broadcasted_iota(jnp.int32, (tm, LANES), 0)
    q_upper = jnp.where(lane == PAD_LANE, NEG_INF, 0.0)
    blk_onehot = (jnp.right_shift(t_in, SEL_BLOCK.bit_length() - 1) == lane - BLK_LANE0).astype(F32)
    for h in range(N_HEADS):
        qo_ref[0, h // HEADS_PER_KV, h % HEADS_PER_KV] = _widen_head(q, h, q_upper).astype(BF16)
    kva = kva_ref[...]
    kvb = kvb_ref[...]
    ks = _head_norm_rope(kva[:, 2 * KV_WIDTH:3 * KV_WIDTH], gs_ref[...], ek_ref[...], cos, sin)
    kw = _head_norm_rope(kvb[:, 0:KV_WIDTH], gw_ref[...], ek_ref[...], cos, sin)
    vs = kva[:, 3 * KV_WIDTH:4 * KV_WIDTH]
    vw = kvb[:, KV_WIDTH:2 * KV_WIDTH]
    kc = kva[:, 0:KV_WIDTH]
    vc = kva[:, KV_WIDTH:2 * KV_WIDTH]
    for g in range(N_KV):
        sl = slice(g * HEAD_DIM, (g + 1) * HEAD_DIM)
        kso_ref[0, g] = _widen_head(ks, g, blk_onehot).astype(BF16)
        kwo_ref[0, g] = _widen_head(kw, g, 0.0).astype(BF16)
        kco_ref[0, g] = kc[:, sl]
        vco_ref[0, g] = vc[:, sl]
    vso_ref[0] = vs.T.astype(BF16)
    vwo_ref[0] = vw.T.astype(BF16)
    go_ref[0] = jax.nn.sigmoid(kvb[:, GATE_OFF:GATE_OFF + LANES]).T


def _prep(proj, pos2, f_lane, gq, gs, gw, e_q, e_k, b, s, tm=512):
    nt = s // tm
    row = lambda bb, i: bb * nt + i
    kv_shape = jax.ShapeDtypeStruct((b, N_KV, s, HEAD_DIM), BF16)
    kv_spec = pl.BlockSpec((1, N_KV, tm, HEAD_DIM), lambda bb, i: (bb, 0, i, 0))
    kx_shape = jax.ShapeDtypeStruct((b, N_KV, s, LANES), BF16)
    kx_spec = pl.BlockSpec((1, N_KV, tm, LANES), lambda bb, i: (bb, 0, i, 0))
    vt_shape = jax.ShapeDtypeStruct((b, KV_WIDTH, s), BF16)
    vt_spec = pl.BlockSpec((1, KV_WIDTH, tm), lambda bb, i: (bb, 0, i))
    const = lambda shape: pl.BlockSpec(shape, lambda bb, i: (0,) * len(shape))
    return pl.pallas_call(
        _prep_kernel,
        grid=(b, nt),
        in_specs=[
            pl.BlockSpec((tm, PROJ_TN), lambda bb, i: (row(bb, i), COL_Q)),
            pl.BlockSpec((tm, PROJ_TN), lambda bb, i: (row(bb, i), COL_KVA)),
            pl.BlockSpec((tm, PROJ_TN), lambda bb, i: (row(bb, i), COL_KVB)),
            pl.BlockSpec((tm, 1), lambda bb, i: (row(bb, i), 0)),
            const((1, LANES)), const((1, ATTN_WIDTH)), const((1, KV_WIDTH)), const((1, KV_WIDTH)),
            const((ATTN_WIDTH, ATTN_WIDTH)), const((KV_WIDTH, KV_WIDTH)),
        ],
        out_specs=[
            pl.BlockSpec((1, N_KV, HEADS_PER_KV, tm, LANES), lambda bb, i: (bb, 0, 0, i, 0)),
            kx_spec, vt_spec, kx_spec, vt_spec, kv_spec, kv_spec,
            pl.BlockSpec((1, LANES, tm), lambda bb, i: (bb, 0, i)),
        ],
        out_shape=[
            jax.ShapeDtypeStruct((b, N_KV, HEADS_PER_KV, s, LANES), BF16),
            kx_shape, vt_shape, kx_shape, vt_shape,
            jax.ShapeDtypeStruct((b, N_KV, s, HEAD_DIM), F32),
            jax.ShapeDtypeStruct((b, N_KV, s, HEAD_DIM), F32),
            jax.ShapeDtypeStruct((b, LANES, s), F32),
        ],
        compiler_params=_cparams(("parallel", "parallel")),
        name="prep",
    )(proj, proj, proj, pos2, f_lane, gq, gs, gw, e_q, e_k)


def _compress_one(x, pe_a, pe_b, w1_ref, w2_ref):
    half = CMP_STRIDE * HEAD_DIM
    nxt = pltpu.roll(x, x.shape[0] - 1, axis=0)
    a = (x + pe_a).astype(BF16)
    b = (nxt + pe_b).astype(BF16)
    hid = (jnp.dot(a, w1_ref[0:half, :], preferred_element_type=F32)
           + jnp.dot(b, w1_ref[half:2 * half, :], preferred_element_type=F32))
    hid = hid * jax.nn.sigmoid(hid)
    return jnp.dot(hid.astype(BF16), w2_ref[...], preferred_element_type=F32)


def _compress_kernel(xk_ref, xv_ref, pek_ref, pev_ref, w1k_ref, w2k_ref, w1v_ref, w2v_ref,
                     gk_ref, posc_ref, f_ref, kc_ref, vc_ref):
    ang = posc_ref[0].astype(F32) * f_ref[...]
    cos, sin = jnp.cos(ang), jnp.sin(ang)
    for g in range(N_KV):
        k = _compress_one(xk_ref[0, g], pek_ref[0:1, :], pek_ref[1:2, :], w1k_ref, w2k_ref)
        ssq = jnp.sum(k * k, axis=-1, keepdims=True)
        kn = k * lax.rsqrt(ssq * (1.0 / HEAD_DIM) + RMS_EPS) * gk_ref[...]
        kn = _rope(kn, cos, sin)
        kc_ref[0, g] = kn.astype(BF16)
        v = _compress_one(xv_ref[0, g], pev_ref[0:1, :], pev_ref[1:2, :], w1v_ref, w2v_ref)
        vc_ref[0, g] = v[:, 0:HEAD_DIM].astype(BF16)


def _compress(xk, xv, pek, pev, w1k, w2k, w1v, w2v, gk, posc, f_lane):
    b = xk.shape[0]
    nrow = xk.shape[2]
    half = CMP_STRIDE * HEAD_DIM
    const = lambda shape: pl.BlockSpec(shape, lambda bb: (0,) * len(shape))
    x_spec = pl.BlockSpec((1, N_KV, nrow, half), lambda bb: (bb, 0, 0, 0))
    o_spec = pl.BlockSpec((1, N_KV, nrow, HEAD_DIM), lambda bb: (bb, 0, 0, 0))
    o_shape = jax.ShapeDtypeStruct((b, N_KV, nrow, HEAD_DIM), BF16)
    k_spec = pl.BlockSpec((1, N_KV, nrow, LANES), lambda bb: (bb, 0, 0, 0))
    k_shape = jax.ShapeDtypeStruct((b, N_KV, nrow, LANES), BF16)
    return pl.pallas_call(
        _compress_kernel,
        grid=(b,),
        in_specs=[x_spec, x_spec, const((2, half)), const((2, half)),
                  const((2 * half, CMP_HIDDEN)), const((CMP_HIDDEN, LANES)),
                  const((2 * half, CMP_HIDDEN)), const((CMP_HIDDEN, LANES)),
                  const((1, LANES)), pl.BlockSpec((1, nrow, 1), lambda bb: (bb, 0, 0)), const((1, LANES))],
        out_specs=[k_spec, o_spec],
        out_shape=[k_shape, o_shape],
        compiler_params=_cparams(("parallel",)),
        name="compress",
    )(xk, xv, pek, pev, w1k, w2k, w1v, w2v, gk, posc, f_lane)


def _attn_kernel(q_ref, kc_ref, vc_ref, ks_ref, vs_ref, kw_ref, vw_ref, gate_ref, ovl_ref, place_ref,
                 o_ref, *, tq, seq):
    r = HEADS_PER_KV
    cols = r * tq
    n_cmp = kc_ref.shape[2]
    n_blk = seq // SEL_BLOCK
    i = pl.program_id(1)
    t0 = i * tq
    gates = gate_ref[0]

    kk = lax.broadcasted_iota(jnp.int32, (tq, cols), 0)
    tt = lax.broadcasted_iota(jnp.int32, (tq, cols), 1) & (tq - 1)
    causal_bias = jnp.where(kk <= tt, 0.0, NEG_INF)
    band_bias = jnp.where(kk > tt, 0.0, NEG_INF)

    for g in range(N_KV):
        q3 = q_ref[0, g]
        qg = q3.reshape(cols, LANES)
        vrow = slice(g * HEAD_DIM, (g + 1) * HEAD_DIM)

        s = _nt_dot(kc_ref[0, g], qg)
        n_idx = lax.broadcasted_iota(jnp.int32, (n_cmp, cols), 0)
        t_col = t0 + (lax.broadcasted_iota(jnp.int32, (n_cmp, cols), 1) & (tq - 1))
        mask = (n_idx * CMP_STRIDE + (CMP_BLOCK - 1)) <= t_col
        s = jnp.where(mask, s, NEG_INF)
        p = jnp.where(mask, jnp.exp2(s - jnp.max(s, axis=0, keepdims=True)), 0.0)
        l = jnp.sum(p, axis=0, keepdims=True)
        p = p * (1.0 / jnp.where(l > 0.0, l, 1.0))
        o_cmp = _tn_dot(vc_ref[0, g], p.astype(BF16))

        p_sum = p[:, 0:tq] + p[:, tq:2 * tq] + p[:, 2 * tq:3 * tq] + p[:, 3 * tq:4 * tq]
        hi = p_sum.astype(BF16)
        lo = (p_sum - hi.astype(F32)).astype(BF16)
        imp = (jnp.dot(ovl_ref[...], hi, preferred_element_type=F32)
               + jnp.dot(ovl_ref[...], lo, preferred_element_type=F32))
        j_idx = lax.broadcasted_iota(jnp.int32, (n_blk, tq), 0)
        t_idx = t0 + lax.broadcasted_iota(jnp.int32, (n_blk, tq), 1)
        dist = jnp.right_shift(t_idx, SEL_BLOCK.bit_length() - 1) - j_idx
        valid = dist >= 0
        forced = (j_idx == 0) | (valid & (dist < N_FORCED_LOCAL))
        score = jnp.where(valid, imp + FORCE_BONUS * forced.astype(F32), NEG_INF)
        cnt = jnp.zeros((n_blk, tq), F32)
        for jp in range(n_blk):
            row = score[jp:jp + 1, :]
            beats = (row > score) | ((row == score) & (j_idx > jp))
            cnt = cnt + beats.astype(F32)
        selected = (cnt < float(min(N_SEL, n_blk))) & valid
        blk_bias = jnp.where(selected, 0.0, NEG_INF).astype(BF16)
        q_bias = _tn_dot(blk_bias, place_ref[...]).astype(BF16)
        qs = (q3 + q_bias[None]).reshape(cols, LANES)

        def sel_tile(k0, carry, bias):
            m_i, l_i, acc = carry
            s = _nt_dot(ks_ref[0, g, pl.ds(k0, tq), :], qs)
            if bias is not None:
                s = s + bias
            m_new = jnp.maximum(m_i, jnp.max(s, axis=0, keepdims=True))
            alpha = jnp.exp2(m_i - m_new)
            p = jnp.exp2(s - m_new)
            l_new = alpha * l_i + jnp.sum(p, axis=0, keepdims=True)
            acc = alpha * acc + jnp.dot(vs_ref[0, vrow, pl.ds(k0, tq)], p.astype(BF16),
                                        preferred_element_type=F32)
            return m_new, l_new, acc

        init = (jnp.full((1, cols), NEG_INF, F32), jnp.zeros((1, cols), F32),
                jnp.zeros((HEAD_DIM, cols), F32))
        carry = lax.fori_loop(
            0, i, lambda kt, c: sel_tile(pl.multiple_of(kt * tq, tq), c, None), init)
        _, l_s, acc_s = sel_tile(pl.multiple_of(t0, tq), carry, causal_bias)
        o_slc = acc_s * (1.0 / l_s)

        span = WINDOW + tq
        w0 = pl.multiple_of(t0, tq)
        s = _nt_dot(kw_ref[0, g, pl.ds(w0, span), :], qg)
        s = jnp.concatenate([s[0:tq] + band_bias, s[tq:WINDOW], s[WINDOW:span] + causal_bias], axis=0)
        p = jnp.exp2(s - jnp.max(s, axis=0, keepdims=True))
        inv_l = 1.0 / jnp.sum(p, axis=0, keepdims=True)
        o_swa = jnp.dot(vw_ref[0, vrow, pl.ds(w0, span)], p.astype(BF16),
                        preferred_element_type=F32) * inv_l

        def gate_row(c):
            parts = [gates[(g * r + hh) * N_BRANCH + c:(g * r + hh) * N_BRANCH + c + 1, :] for hh in range(r)]
            return jnp.concatenate(parts, axis=1)

        o = gate_row(0) * o_cmp + gate_row(1) * o_slc + gate_row(2) * o_swa
        for hh in range(r):
            o_ref[0, g * r + hh] = o[:, hh * tq:(hh + 1) * tq].astype(BF16)


def _attn(q, kc, vc, ks, vs, kw, vw, gates, ovl_t, place, tq=256):
    b, _, _, s, _ = q.shape
    n_cmp = kc.shape[2]
    nq = s // tq
    assert WINDOW % tq == 0 and tq % SEL_BLOCK == 0
    full = lambda shape: pl.BlockSpec((1,) + shape, lambda bb, i: (bb,) + (0,) * len(shape))
    const = lambda shape: pl.BlockSpec(shape, lambda bb, i: (0,) * len(shape))
    return pl.pallas_call(
        functools.partial(_attn_kernel, tq=tq, seq=s),
        grid=(b, nq),
        in_specs=[
            pl.BlockSpec((1, N_KV, HEADS_PER_KV, tq, LANES), lambda bb, i: (bb, 0, 0, i, 0)),
            full((N_KV, n_cmp, LANES)), full((N_KV, n_cmp, HEAD_DIM)),
            full((N_KV, s, LANES)), full((KV_WIDTH, s)),
            full((N_KV, s + WINDOW, LANES)), full((KV_WIDTH, s + WINDOW)),
            pl.BlockSpec((1, LANES, tq), lambda bb, i: (bb, 0, i)),
            const(ovl_t.shape), const(place.shape),
        ],
        out_specs=pl.BlockSpec((1, N_HEADS, HEAD_DIM, tq), lambda bb, i: (bb, 0, 0, i)),
        out_shape=jax.ShapeDtypeStruct((b, N_HEADS, HEAD_DIM, s), BF16),
        compiler_params=_cparams(("parallel", "arbitrary")),
        name="attn",
    )(q, kc, vc, ks, vs, kw, vw, gates, ovl_t, place)


POOL_HALO = 16


def _mix_kernel(x_ref, m0_ref, m1_ref, u_ref, halo_ref, o_ref, wh_ref, wp_ref, ps_ref, wo_ref, y_ref,
                *, tm, seq):
    it = pl.program_id(0) % (seq // tm)
    u = u_ref[...]
    halo = jnp.where(it == 0, 0.0, halo_ref[...])
    ext = jnp.concatenate([halo, u], axis=0)
    sums = []
    acc = ext
    for sh in (1, 2, 4, 8):
        acc = acc + pltpu.roll(acc, sh, axis=0)
        sums.append(acc)
    t_in = it * tm + lax.broadcasted_iota(jnp.int32, (tm, 1), 0)
    pool_parts = []
    for gi, win in enumerate(POOL_WINDOWS):
        sl = slice(gi * POOL_GROUP_WIDTH, (gi + 1) * POOL_GROUP_WIDTH)
        cnt = jnp.minimum(t_in + 1, win).astype(F32)
        pooled = sums[gi][POOL_HALO:, sl] / cnt - u[:, sl]
        pool_parts.append(jnp.dot(pooled.astype(BF16), wp_ref[gi], preferred_element_type=F32))
    pool_branch = jnp.concatenate(pool_parts, axis=1) * ps_ref[...]
    attn_branch = jnp.concatenate(
        [_tn_dot(o_ref[0, h], wh_ref[h]) for h in range(N_HEADS)], axis=1)
    y = jax.nn.sigmoid(m0_ref[...]) * attn_branch + jax.nn.sigmoid(m1_ref[...]) * pool_branch
    y_ref[...] = x_ref[...] + jnp.dot(y.astype(BF16), wo_ref[...], preferred_element_type=F32)


def _mix(x2, proj, o, wh, wp, ps, wo, s, tm=256):
    t = x2.shape[0]
    nt = s // tm
    hb = tm // POOL_HALO
    const = lambda shape: pl.BlockSpec(shape, lambda i: (0,) * len(shape))
    return pl.pallas_call(
        functools.partial(_mix_kernel, tm=tm, seq=s),
        grid=(t // tm,),
        in_specs=[
            pl.BlockSpec((tm, D_MODEL), lambda i: (i, 0)),
            pl.BlockSpec((tm, D_MODEL), lambda i: (i, COL_M0 // 2)),
            pl.BlockSpec((tm, D_MODEL), lambda i: (i, COL_M1 // 2)),
            pl.BlockSpec((tm, POOL_WIDTH), lambda i: (i, COL_POOL)),
            pl.BlockSpec((POOL_HALO, POOL_WIDTH), lambda i: (jnp.maximum(i * hb - 1, 0), COL_POOL)),
            pl.BlockSpec((1, N_HEADS, HEAD_DIM, tm), lambda i: (i // nt, 0, 0, i % nt)),
            const(wh.shape), const(wp.shape), const(ps.shape), const(wo.shape),
        ],
        out_specs=pl.BlockSpec((tm, D_MODEL), lambda i: (i, 0)),
        out_shape=jax.ShapeDtypeStruct((t, D_MODEL), F32),
        compiler_params=_cparams(("parallel",)),
        name="mix",
    )(x2, proj, proj, proj, proj, o, wh, wp, ps, wo)


ROUTE_EXPERT_OFF = N_EXPERT_GROUPS


def _route(logits):
    lane = lax.broadcasted_iota(jnp.int32, logits.shape, 1)
    is_g = lane < N_EXPERT_GROUPS
    gmax = jnp.max(jnp.where(is_g, logits, -jnp.inf), axis=-1, keepdims=True)
    ge = jnp.where(is_g, jnp.exp(logits - gmax), 0.0)
    gprob = ge / jnp.sum(ge, axis=-1, keepdims=True)
    p_group = jnp.max(jnp.where(is_g, gprob, -jnp.inf), axis=-1, keepdims=True)
    g_sel = jnp.min(jnp.where(is_g & (gprob == p_group), lane, LANES), axis=-1, keepdims=True)
    e_lane = lane - ROUTE_EXPERT_OFF
    in_grp = ((e_lane >= 0) & (e_lane < N_EXPERTS)
              & (jnp.right_shift(e_lane, EXPERTS_PER_GROUP.bit_length() - 1) == g_sel))
    el = jnp.where(in_grp, logits, -jnp.inf)
    v1 = jnp.max(el, axis=-1, keepdims=True)
    i1 = jnp.min(jnp.where(in_grp & (el == v1), lane, LANES), axis=-1, keepdims=True)
    el2 = jnp.where(lane == i1, -jnp.inf, el)
    v2 = jnp.max(el2, axis=-1, keepdims=True)
    i2 = jnp.min(jnp.where(in_grp & (lane != i1) & (el2 == v2), lane, LANES), axis=-1, keepdims=True)
    e2 = jnp.exp(v2 - v1)
    den = 1.0 + e2
    return (jnp.where(lane == i1, (1.0 / den) * p_group, 0.0)
            + jnp.where(lane == i2, (e2 / den) * p_group, 0.0))


def _moe_kernel(x_ref, g_ref, wr_ref, br_ref, wg_ref, wu_ref, wd_ref, y_ref, h_ref, c_ref):
    e = pl.program_id(1)

    @pl.when(e == 0)
    def _():
        x = x_ref[...]
        inv = lax.rsqrt(jnp.mean(x * x, axis=-1, keepdims=True) + RMS_EPS)
        ht = x * inv * g_ref[...]
        h_ref[...] = ht.astype(BF16)
        logits = jnp.dot(ht, wr_ref[...], preferred_element_type=F32,
                         precision=lax.Precision.HIGHEST) + br_ref[...]
        c_ref[...] = _route(logits)

    h = h_ref[...]
    a = jnp.dot(h, wg_ref[0], preferred_element_type=F32)
    hid = a * jax.nn.sigmoid(a) * jnp.dot(h, wu_ref[0], preferred_element_type=F32)
    lane = lax.broadcasted_iota(jnp.int32, c_ref.shape, 1)
    c = jnp.sum(jnp.where(lane == e + ROUTE_EXPERT_OFF, c_ref[...], 0.0), axis=-1, keepdims=True)
    contrib = jnp.dot((hid * c).astype(BF16), wd_ref[0], preferred_element_type=F32)

    @pl.when(e == 0)
    def _():
        y_ref[...] = x_ref[...] + contrib

    @pl.when(e != 0)
    def _():
        y_ref[...] += contrib


def _moe(x1, g, wr, br, wg, wu, wd, tm=512):
    t = x1.shape[0]
    const = lambda shape: pl.BlockSpec(shape, lambda i, e: (0,) * len(shape))
    return pl.pallas_call(
        _moe_kernel,
        grid=(t // tm, N_EXPERTS),
        in_specs=[
            pl.BlockSpec((tm, D_MODEL), lambda i, e: (i, 0)),
            const((1, D_MODEL)), const((D_MODEL, LANES)), const((1, LANES)),
            pl.BlockSpec((1, D_MODEL, D_EXPERT), lambda i, e: (e, 0, 0)),
            pl.BlockSpec((1, D_MODEL, D_EXPERT), lambda i, e: (e, 0, 0)),
            pl.BlockSpec((1, D_EXPERT, D_MODEL), lambda i, e: (e, 0, 0)),
        ],
        out_specs=pl.BlockSpec((tm, D_MODEL), lambda i, e: (i, 0)),
        out_shape=jax.ShapeDtypeStruct((t, D_MODEL), F32),
        scratch_shapes=[pltpu.VMEM((tm, D_MODEL), BF16), pltpu.VMEM((tm, LANES), F32)],
        compiler_params=_cparams(("parallel", "arbitrary")),
        name="moe",
    )(x1, g, wr, br, wg, wu, wd)


def _rope_lane_freq():
    inv_freq = ROPE_THETA ** (-jnp.arange(0, ROPE_DIM, 2, dtype=F32) / ROPE_DIM)
    d = np.arange(LANES) % HEAD_DIM
    f = jnp.where(jnp.asarray(d < ROPE_DIM), inv_freq[np.asarray(d % (ROPE_DIM // 2))], 0.0)
    return f.reshape(1, LANES).astype(F32)


def _head_block_ones(width):
    idx = np.arange(width) // HEAD_DIM
    return jnp.asarray(idx[:, None] == idx[None, :], dtype=BF16)


def _overlap_t(n_cmp_pad, n_blk):
    a0 = np.arange(n_cmp_pad)[:, None] * CMP_STRIDE
    b0 = np.arange(n_blk)[None, :] * SEL_BLOCK
    ov = np.clip(np.minimum(a0 + CMP_BLOCK, b0 + SEL_BLOCK) - np.maximum(a0, b0), 0, None) / CMP_BLOCK
    return jnp.asarray(ov.T, dtype=BF16)


def _bias_place(n_blk):
    return jnp.asarray(np.arange(n_blk)[:, None] + BLK_LANE0 == np.arange(LANES)[None, :], dtype=BF16)


def _layer(x, positions, attn_norm_g, w_in, q_norm_g, k_norm_cmp_g, k_norm_slc_g, k_norm_swa_g,
           cmp_pos_emb_k, cmp_w1_k, cmp_w2_k, cmp_pos_emb_v, cmp_w1_v, cmp_w2_v, w_head_out, w_pool,
           pool_scale, w_out, ffn_norm_g, w_router_group, b_router_group, w_router_expert,
           b_router_expert, w_expert_gate, w_expert_up, w_expert_down):
    b, s, d = x.shape
    t = b * s
    x2 = x.reshape(t, d)

    o_q, o_kv, o_g = ATTN_WIDTH, ATTN_WIDTH + 6 * KV_WIDTH, ATTN_WIDTH + 6 * KV_WIDTH + N_HEADS * N_BRANCH
    o_p = o_g + POOL_WIDTH
    used = 2 * ATTN_WIDTH + 2 * D_MODEL + 6 * KV_WIDTH + N_HEADS * N_BRANCH
    w_packed = jnp.concatenate(
        [w_in[:, :o_q], w_in[:, o_g:o_p], w_in[:, o_p:], w_in[:, o_q:o_kv], w_in[:, o_kv:o_g],
         jnp.zeros((d, PROJ_WIDTH - used), w_in.dtype)], axis=1).astype(BF16)
    proj = _proj(x2, attn_norm_g.reshape(1, d), w_packed)

    f_lane = _rope_lane_freq()
    tile_h = lambda g_, n: jnp.tile(g_, n).reshape(1, n * HEAD_DIM)
    q, ks, vs, kw, vw, kc_raw, vc_raw, gates = _prep(
        proj, positions.reshape(t, 1), f_lane, tile_h(q_norm_g, N_HEADS), tile_h(k_norm_slc_g, N_KV),
        tile_h(k_norm_swa_g, N_KV), _head_block_ones(ATTN_WIDTH), _head_block_ones(KV_WIDTH), b, s)

    n_row = s // CMP_STRIDE
    half = CMP_STRIDE * HEAD_DIM
    xk = kc_raw.reshape(b, N_KV, n_row, half)
    xv = vc_raw.reshape(b, N_KV, n_row, half)
    pad2 = lambda w2: jnp.pad(w2, ((0, 0), (0, LANES - HEAD_DIM))).astype(BF16)
    cmp_end = np.minimum(np.arange(n_row) * CMP_STRIDE + CMP_BLOCK - 1, s - 1)
    posc = positions[:, cmp_end].reshape(b, n_row, 1)
    gk = jnp.pad(k_norm_cmp_g, (0, LANES - HEAD_DIM)).reshape(1, LANES)
    kc, vc = _compress(xk, xv, cmp_pos_emb_k.reshape(2, half), cmp_pos_emb_v.reshape(2, half),
                       cmp_w1_k.astype(BF16), pad2(cmp_w2_k), cmp_w1_v.astype(BF16), pad2(cmp_w2_v),
                       gk, posc, f_lane)

    n_blk = s // SEL_BLOCK
    assert n_blk <= PAD_LANE - BLK_LANE0
    front = jnp.zeros((b, N_KV, WINDOW, LANES), BF16).at[..., PAD_LANE].set(1.0)
    kw = jnp.concatenate([front, kw], axis=2)
    vw = jnp.concatenate([jnp.zeros((b, KV_WIDTH, WINDOW), BF16), vw], axis=2)
    o = _attn(q, kc, vc, ks, vs, kw, vw, gates, _overlap_t(n_row, n_blk), _bias_place(n_blk))

    x1 = _mix(x2, proj, o, w_head_out.astype(BF16), w_pool.astype(BF16), pool_scale.reshape(1, d),
              w_out.astype(BF16), s)

    wr = jnp.concatenate([w_router_group, w_router_expert,
                          jnp.zeros((d, LANES - N_EXPERT_GROUPS - N_EXPERTS), F32)], axis=1)
    br = jnp.concatenate([b_router_group, b_router_expert,
                          jnp.zeros((LANES - N_EXPERT_GROUPS - N_EXPERTS,), F32)]).reshape(1, LANES)
    x_out = _moe(x1, ffn_norm_g.reshape(1, d), wr, br, w_expert_gate.astype(BF16),
                 w_expert_up.astype(BF16), w_expert_down.astype(BF16))
    return x_out.reshape(b, s, d)


def kernel(x, positions, attn_norm_g, w_in, q_norm_g, k_norm_cmp_g, k_norm_slc_g, k_norm_swa_g,
           cmp_pos_emb_k, cmp_w1_k, cmp_w2_k, cmp_pos_emb_v, cmp_w1_v, cmp_w2_v, w_head_out, w_pool,
           pool_scale, w_out, ffn_norm_g, w_router_group, b_router_group, w_router_expert,
           b_router_expert, w_expert_gate, w_expert_up, w_expert_down):
    params = (attn_norm_g, w_in, q_norm_g, k_norm_cmp_g, k_norm_slc_g, k_norm_swa_g, cmp_pos_emb_k,
              cmp_w1_k, cmp_w2_k, cmp_pos_emb_v, cmp_w1_v, cmp_w2_v, w_head_out, w_pool, pool_scale,
              w_out, ffn_norm_g, w_router_group, b_router_group, w_router_expert, b_router_expert,
              w_expert_gate, w_expert_up, w_expert_down)
    for layer in range(attn_norm_g.shape[0]):
        x = _layer(x, positions, *[p[layer] for p in params])
    return x
```

```python
import functools

import jax
import jax.numpy as jnp
import numpy as np
from jax import lax
from jax.experimental import pallas as pl
from jax.experimental.pallas import tpu as pltpu

F32 = jnp.float32
BF16 = jnp.bfloat16

D_MODEL = 2048
N_HEADS = 16
HEAD_DIM = 64
N_KV = 4
HEADS_PER_KV = N_HEADS // N_KV
ATTN_WIDTH = N_HEADS * HEAD_DIM
KV_WIDTH = N_KV * HEAD_DIM
N_BRANCH = 3
CMP_BLOCK = 32
CMP_STRIDE = 16
CMP_HIDDEN = 4 * HEAD_DIM
SEL_BLOCK = 64
N_SEL = 16
N_FORCED_LOCAL = 2
WINDOW = 512
FORCE_BONUS = 1e4
NEG_INF = -1e30
ROPE_THETA = 500000.0
ROPE_DIM = HEAD_DIM // 4
HEAD_OUT_WIDTH = D_MODEL // N_HEADS
POOL_WINDOWS = (2, 4, 8, 16)
POOL_GROUPS = 4
POOL_WIDTH = 1024
POOL_GROUP_WIDTH = POOL_WIDTH // POOL_GROUPS
POOL_OUT_WIDTH = D_MODEL // POOL_GROUPS
N_EXPERT_GROUPS = 4
EXPERTS_PER_GROUP = 4
N_EXPERTS = N_EXPERT_GROUPS * EXPERTS_PER_GROUP
D_EXPERT = 256
RMS_EPS = 1e-6

LANES = 128
Q_SCALE = HEAD_DIM ** -0.5 * float(np.log2(np.e))
BLK_LANE0 = HEAD_DIM
PAD_LANE = HEAD_DIM + 32
PROJ_WIDTH = 8192
PROJ_TN = 1024
COL_Q, COL_POOL, COL_M0, COL_M1, COL_KVA, COL_KVB = 0, 1, 2, 4, 6, 7
GATE_OFF = 2 * KV_WIDTH
VMEM_LIMIT = 56 * 1024 * 1024


def _cparams(sem):
    return pltpu.CompilerParams(dimension_semantics=sem, vmem_limit_bytes=VMEM_LIMIT)


def _nt_dot(a, b):
    return lax.dot_general(a, b, (((1,), (1,)), ((), ())), preferred_element_type=F32)


def _tn_dot(a, b):
    return lax.dot_general(a, b, (((0,), (0,)), ((), ())), preferred_element_type=F32)


def _split_dot(x, e):
    hi = x.astype(BF16)
    lo = (x - hi.astype(F32)).astype(BF16)
    return (jnp.dot(hi, e, preferred_element_type=F32)
            + jnp.dot(lo, e, preferred_element_type=F32))


def _proj_kernel(x_ref, g_ref, w_ref, o_ref, h_ref):
    @pl.when(pl.program_id(1) == 0)
    def _():
        x = x_ref[...]
        inv = lax.rsqrt(jnp.mean(x * x, axis=-1, keepdims=True) + RMS_EPS)
        h_ref[...] = (x * inv * g_ref[...]).astype(BF16)

    o_ref[...] = jnp.dot(h_ref[...], w_ref[...], preferred_element_type=F32)


def _proj(x2, g, w_packed, tm=1024):
    t = x2.shape[0]
    return pl.pallas_call(
        _proj_kernel,
        grid=(t // tm, PROJ_WIDTH // PROJ_TN),
        in_specs=[
            pl.BlockSpec((tm, D_MODEL), lambda i, j: (i, 0)),
            pl.BlockSpec((1, D_MODEL), lambda i, j: (0, 0)),
            pl.BlockSpec((D_MODEL, PROJ_TN), lambda i, j: (0, j)),
        ],
        out_specs=pl.BlockSpec((tm, PROJ_TN), lambda i, j: (i, j)),
        out_shape=jax.ShapeDtypeStruct((t, PROJ_WIDTH), F32),
        scratch_shapes=[pltpu.VMEM((tm, D_MODEL), BF16)],
        compiler_params=_cparams(("parallel", "arbitrary")),
        name="proj",
    )(x2, g, w_packed)


def _rope(xn, cos, sin):
    w = xn.shape[-1]
    lane = lax.broadcasted_iota(jnp.int32, xn.shape, 1) & (HEAD_DIM - 1)
    up = pltpu.roll(xn, w - ROPE_DIM // 2, axis=1)
    dn = pltpu.roll(xn, ROPE_DIM // 2, axis=1)
    return jnp.where(lane < ROPE_DIM // 2, xn * cos - up * sin,
                     jnp.where(lane < ROPE_DIM, xn * cos + dn * sin, xn))


def _head_norm_rope(x, gain, e_blk, cos, sin):
    w = x.shape[-1]
    ssq = _split_dot(x * x, e_blk)
    xn = x * lax.rsqrt(ssq * (1.0 / HEAD_DIM) + RMS_EPS) * gain
    reps = w // LANES
    return _rope(xn, jnp.concatenate([cos] * reps, axis=1), jnp.concatenate([sin] * reps, axis=1))


def _widen_head(x, h, upper):
    col = x[:, (h // 2) * LANES:(h // 2 + 1) * LANES]
    if h % 2:
        col = pltpu.roll(col, HEAD_DIM, axis=1)
    lane = lax.broadcasted_iota(jnp.int32, col.shape, 1)
    return jnp.where(lane < HEAD_DIM, col, upper)


def _prep_kernel(q_ref, kva_ref, kvb_ref, pos_ref, f_ref, gq_ref, gs_ref, gw_ref, eq_ref, ek_ref,
                 qo_ref, kso_ref, vso_ref, kwo_ref, vwo_ref, kco_ref, vco_ref, go_ref):
    tm = q_ref.shape[0]
    ang = pos_ref[...].astype(F32) * f_ref[...]
    cos, sin = jnp.cos(ang), jnp.sin(ang)
    q = _head_norm_rope(q_ref[...], gq_ref[...], eq_ref[...], cos, sin) * Q_SCALE
    lane = lax.broadcasted_iota(jnp.int32, (tm, LANES), 1)
    t_in = pl.program_id(1) * tm + lax.broadcasted_iota(jnp.int32, (tm, LANES), 0)
    q_upper = jnp.where(lane == PAD_LANE, NEG_INF, 0.0)
    blk_onehot = (jnp.right_shift(t_in, SEL_BLOCK.bit_length() - 1) == lane - BLK_LANE0).astype(F32)
    for h in range(N_HEADS):
        qo_ref[0, h // HEADS_PER_KV, h % HEADS_PER_KV] = _widen_head(q, h, q_upper).astype(BF16)
    kva = kva_ref[...]
    kvb = kvb_ref[...]
    ks = _head_norm_rope(kva[:, 2 * KV_WIDTH:3 * KV_WIDTH], gs_ref[...], ek_ref[...], cos, sin)
    kw = _head_norm_rope(kvb[:, 0:KV_WIDTH], gw_ref[...], ek_ref[...], cos, sin)
    vs = kva[:, 3 * KV_WIDTH:4 * KV_WIDTH]
    vw = kvb[:, KV_WIDTH:2 * KV_WIDTH]
    kc = kva[:, 0:KV_WIDTH]
    vc = kva[:, KV_WIDTH:2 * KV_WIDTH]
    for g in range(N_KV):
        sl = slice(g * HEAD_DIM, (g + 1) * HEAD_DIM)
        kso_ref[0, g] = _widen_head(ks, g, blk_onehot).astype(BF16)
        kwo_ref[0, g] = _widen_head(kw, g, 0.0).astype(BF16)
        kco_ref[0, g] = kc[:, sl]
        vco_ref[0, g] = vc[:, sl]
    vso_ref[0] = vs.T.astype(BF16)
    vwo_ref[0] = vw.T.astype(BF16)
    go_ref[0] = jax.nn.sigmoid(kvb[:, GATE_OFF:GATE_OFF + LANES]).T


def _prep(proj, pos2, f_lane, gq, gs, gw, e_q, e_k, b, s, tm=512):
    nt = s // tm
    row = lambda bb, i: bb * nt + i
    kv_shape = jax.ShapeDtypeStruct((b, N_KV, s, HEAD_DIM), BF16)
    kv_spec = pl.BlockSpec((1, N_KV, tm, HEAD_DIM), lambda bb, i: (bb, 0, i, 0))
    kx_shape = jax.ShapeDtypeStruct((b, N_KV, s, LANES), BF16)
    kx_spec = pl.BlockSpec((1, N_KV, tm, LANES), lambda bb, i: (bb, 0, i, 0))
    vt_shape = jax.ShapeDtypeStruct((b, KV_WIDTH, s), BF16)
    vt_spec = pl.BlockSpec((1, KV_WIDTH, tm), lambda bb, i: (bb, 0, i))
    const = lambda shape: pl.BlockSpec(shape, lambda bb, i: (0,) * len(shape))
    return pl.pallas_call(
        _prep_kernel,
        grid=(b, nt),
        in_specs=[
            pl.BlockSpec((tm, PROJ_TN), lambda bb, i: (row(bb, i), COL_Q)),
            pl.BlockSpec((tm, PROJ_TN), lambda bb, i: (row(bb, i), COL_KVA)),
            pl.BlockSpec((tm, PROJ_TN), lambda bb, i: (row(bb, i), COL_KVB)),
            pl.BlockSpec((tm, 1), lambda bb, i: (row(bb, i), 0)),
            const((1, LANES)), const((1, ATTN_WIDTH)), const((1, KV_WIDTH)), const((1, KV_WIDTH)),
            const((ATTN_WIDTH, ATTN_WIDTH)), const((KV_WIDTH, KV_WIDTH)),
        ],
        out_specs=[
            pl.BlockSpec((1, N_KV, HEADS_PER_KV, tm, LANES), lambda bb, i: (bb, 0, 0, i, 0)),
            kx_spec, vt_spec, kx_spec, vt_spec, kv_spec, kv_spec,
            pl.BlockSpec((1, LANES, tm), lambda bb, i: (bb, 0, i)),
        ],
        out_shape=[
            jax.ShapeDtypeStruct((b, N_KV, HEADS_PER_KV, s, LANES), BF16),
            kx_shape, vt_shape, kx_shape, vt_shape,
            jax.ShapeDtypeStruct((b, N_KV, s, HEAD_DIM), F32),
            jax.ShapeDtypeStruct((b, N_KV, s, HEAD_DIM), F32),
            jax.ShapeDtypeStruct((b, LANES, s), F32),
        ],
        compiler_params=_cparams(("parallel", "parallel")),
        name="prep",
    )(proj, proj, proj, pos2, f_lane, gq, gs, gw, e_q, e_k)


def _compress_one(x, pe_a, pe_b, w1_ref, w2_ref):
    half = CMP_STRIDE * HEAD_DIM
    nxt = pltpu.roll(x, x.shape[0] - 1, axis=0)
    a = (x + pe_a).astype(BF16)
    b = (nxt + pe_b).astype(BF16)
    hid = (jnp.dot(a, w1_ref[0:half, :], preferred_element_type=F32)
           + jnp.dot(b, w1_ref[half:2 * half, :], preferred_element_type=F32))
    hid = hid * jax.nn.sigmoid(hid)
    return jnp.dot(hid.astype(BF16), w2_ref[...], preferred_element_type=F32)


def _compress_kernel(xk_ref, xv_ref, pek_ref, pev_ref, w1k_ref, w2k_ref, w1v_ref, w2v_ref,
                     gk_ref, posc_ref, f_ref, kc_ref, vc_ref):
    ang = posc_ref[0].astype(F32) * f_ref[...]
    cos, sin = jnp.cos(ang), jnp.sin(ang)
    for g in range(N_KV):
        k = _compress_one(xk_ref[0, g], pek_ref[0:1, :], pek_ref[1:2, :], w1k_ref, w2k_ref)
        ssq = jnp.sum(k * k, axis=-1, keepdims=True)
        kn = k * lax.rsqrt(ssq * (1.0 / HEAD_DIM) + RMS_EPS) * gk_ref[...]
        kn = _rope(kn, cos, sin)
        kc_ref[0, g] = kn.astype(BF16)
        v = _compress_one(xv_ref[0, g], pev_ref[0:1, :], pev_ref[1:2, :], w1v_ref, w2v_ref)
        vc_ref[0, g] = v[:, 0:HEAD_DIM].astype(BF16)


def _compress(xk, xv, pek, pev, w1k, w2k, w1v, w2v, gk, posc, f_lane):
    b = xk.shape[0]
    nrow = xk.shape[2]
    half = CMP_STRIDE * HEAD_DIM
    const = lambda shape: pl.BlockSpec(shape, lambda bb: (0,) * len(shape))
    x_spec = pl.BlockSpec((1, N_KV, nrow, half), lambda bb: (bb, 0, 0, 0))
    o_spec = pl.BlockSpec((1, N_KV, nrow, HEAD_DIM), lambda bb: (bb, 0, 0, 0))
    o_shape = jax.ShapeDtypeStruct((b, N_KV, nrow, HEAD_DIM), BF16)
    k_spec = pl.BlockSpec((1, N_KV, nrow, LANES), lambda bb: (bb, 0, 0, 0))
    k_shape = jax.ShapeDtypeStruct((b, N_KV, nrow, LANES), BF16)
    return pl.pallas_call(
        _compress_kernel,
        grid=(b,),
        in_specs=[x_spec, x_spec, const((2, half)), const((2, half)),
                  const((2 * half, CMP_HIDDEN)), const((CMP_HIDDEN, LANES)),
                  const((2 * half, CMP_HIDDEN)), const((CMP_HIDDEN, LANES)),
                  const((1, LANES)), pl.BlockSpec((1, nrow, 1), lambda bb: (bb, 0, 0)), const((1, LANES))],
        out_specs=[k_spec, o_spec],
        out_shape=[k_shape, o_shape],
        compiler_params=_cparams(("parallel",)),
        name="compress",
    )(xk, xv, pek, pev, w1k, w2k, w1v, w2v, gk, posc, f_lane)


def _attn_kernel(q_ref, kc_ref, vc_ref, ks_ref, vs_ref, kw_ref, vw_ref, gate_ref, ovl_ref, place_ref,
                 o_ref, *, tq, seq):
    r = HEADS_PER_KV
    cols = r * tq
    n_cmp = kc_ref.shape[2]
    n_blk = seq // SEL_BLOCK
    i = pl.program_id(1)
    t0 = i * tq
    gates = gate_ref[0]

    kk = lax.broadcasted_iota(jnp.int32, (tq, cols), 0)
    tt = lax.broadcasted_iota(jnp.int32, (tq, cols), 1) & (tq - 1)
    causal_bias = jnp.where(kk <= tt, 0.0, NEG_INF)
    band_bias = jnp.where(kk > tt, 0.0, NEG_INF)

    for g in range(N_KV):
        q3 = q_ref[0, g]
        qg = q3.reshape(cols, LANES)
        vrow = slice(g * HEAD_DIM, (g + 1) * HEAD_DIM)

        s = _nt_dot(kc_ref[0, g], qg)
        n_idx = lax.broadcasted_iota(jnp.int32, (n_cmp, cols), 0)
        t_col = t0 + (lax.broadcasted_iota(jnp.int32, (n_cmp, cols), 1) & (tq - 1))
        mask = (n_idx * CMP_STRIDE + (CMP_BLOCK - 1)) <= t_col
        s = jnp.where(mask, s, NEG_INF)
        p = jnp.where(mask, jnp.exp2(s - jnp.max(s, axis=0, keepdims=True)), 0.0)
        l = jnp.sum(p, axis=0, keepdims=True)
        p = p * (1.0 / jnp.where(l > 0.0, l, 1.0))
        o_cmp = _tn_dot(vc_ref[0, g], p.astype(BF16))

        p_sum = p[:, 0:tq] + p[:, tq:2 * tq] + p[:, 2 * tq:3 * tq] + p[:, 3 * tq:4 * tq]
        hi = p_sum.astype(BF16)
        lo = (p_sum - hi.astype(F32)).astype(BF16)
        imp = (jnp.dot(ovl_ref[...], hi, preferred_element_type=F32)
               + jnp.dot(ovl_ref[...], lo, preferred_element_type=F32))
        j_idx = lax.broadcasted_iota(jnp.int32, (n_blk, tq), 0)
        t_idx = t0 + lax.broadcasted_iota(jnp.int32, (n_blk, tq), 1)
        dist = jnp.right_shift(t_idx, SEL_BLOCK.bit_length() - 1) - j_idx
        valid = dist >= 0
        forced = (j_idx == 0) | (valid & (dist < N_FORCED_LOCAL))
        score = jnp.where(valid, imp + FORCE_BONUS * forced.astype(F32), NEG_INF)
        cnt = jnp.zeros((n_blk, tq), F32)
        for jp in range(n_blk):
            row = score[jp:jp + 1, :]
            beats = (row > score) | ((row == score) & (j_idx > jp))
            cnt = cnt + beats.astype(F32)
        selected = (cnt < float(min(N_SEL, n_blk))) & valid
        blk_bias = jnp.where(selected, 0.0, NEG_INF).astype(BF16)
        q_bias = _tn_dot(blk_bias, place_ref[...]).astype(BF16)
        qs = (q3 + q_bias[None]).reshape(cols, LANES)

        def sel_tile(k0, carry, bias):
            m_i, l_i, acc = carry
            s = _nt_dot(ks_ref[0, g, pl.ds(k0, tq), :], qs)
            if bias is not None:
                s = s + bias
            m_new = jnp.maximum(m_i, jnp.max(s, axis=0, keepdims=True))
            alpha = jnp.exp2(m_i - m_new)
            p = jnp.exp2(s - m_new)
            l_new = alpha * l_i + jnp.sum(p, axis=0, keepdims=True)
            acc = alpha * acc + jnp.dot(vs_ref[0, vrow, pl.ds(k0, tq)], p.astype(BF16),
                                        preferred_element_type=F32)
            return m_new, l_new, acc

        init = (jnp.full((1, cols), NEG_INF, F32), jnp.zeros((1, cols), F32),
                jnp.zeros((HEAD_DIM, cols), F32))
        carry = lax.fori_loop(
            0, i, lambda kt, c: sel_tile(pl.multiple_of(kt * tq, tq), c, None), init)
        _, l_s, acc_s = sel_tile(pl.multiple_of(t0, tq), carry, causal_bias)
        o_slc = acc_s * (1.0 / l_s)

        span = WINDOW + tq
        w0 = pl.multiple_of(t0, tq)
        s = _nt_dot(kw_ref[0, g, pl.ds(w0, span), :], qg)
        s = jnp.concatenate([s[0:tq] + band_bias, s[tq:WINDOW], s[WINDOW:span] + causal_bias], axis=0)
        p = jnp.exp2(s - jnp.max(s, axis=0, keepdims=True))
        inv_l = 1.0 / jnp.sum(p, axis=0, keepdims=True)
        o_swa = jnp.dot(vw_ref[0, vrow, pl.ds(w0, span)], p.astype(BF16),
                        preferred_element_type=F32) * inv_l

        def gate_row(c):
            parts = [gates[(g * r + hh) * N_BRANCH + c:(g * r + hh) * N_BRANCH + c + 1, :] for hh in range(r)]
            return jnp.concatenate(parts, axis=1)

        o = gate_row(0) * o_cmp + gate_row(1) * o_slc + gate_row(2) * o_swa
        for hh in range(r):
            o_ref[0, g * r + hh] = o[:, hh * tq:(hh + 1) * tq].astype(BF16)


def _attn(q, kc, vc, ks, vs, kw, vw, gates, ovl_t, place, tq=256):
    b, _, _, s, _ = q.shape
    n_cmp = kc.shape[2]
    nq = s // tq
    assert WINDOW % tq == 0 and tq % SEL_BLOCK == 0
    full = lambda shape: pl.BlockSpec((1,) + shape, lambda bb, i: (bb,) + (0,) * len(shape))
    const = lambda shape: pl.BlockSpec(shape, lambda bb, i: (0,) * len(shape))
    return pl.pallas_call(
        functools.partial(_attn_kernel, tq=tq, seq=s),
        grid=(b, nq),
        in_specs=[
            pl.BlockSpec((1, N_KV, HEADS_PER_KV, tq, LANES), lambda bb, i: (bb, 0, 0, i, 0)),
            full((N_KV, n_cmp, LANES)), full((N_KV, n_cmp, HEAD_DIM)),
            full((N_KV, s, LANES)), full((KV_WIDTH, s)),
            full((N_KV, s + WINDOW, LANES)), full((KV_WIDTH, s + WINDOW)),
            pl.BlockSpec((1, LANES, tq), lambda bb, i: (bb, 0, i)),
            const(ovl_t.shape), const(place.shape),
        ],
        out_specs=pl.BlockSpec((1, N_HEADS, HEAD_DIM, tq), lambda bb, i: (bb, 0, 0, i)),
        out_shape=jax.ShapeDtypeStruct((b, N_HEADS, HEAD_DIM, s), BF16),
        compiler_params=_cparams(("parallel", "arbitrary")),
        name="attn",
    )(q, kc, vc, ks, vs, kw, vw, gates, ovl_t, place)


POOL_HALO = 16


def _mix_kernel(x_ref, m0_ref, m1_ref, u_ref, halo_ref, o_ref, wh_ref, wp_ref, ps_ref, wo_ref,
                gf_ref, wr_ref, br_ref, y_ref, *, tm, seq):
    it = pl.program_id(0) % (seq // tm)
    u = u_ref[...]
    halo = jnp.where(it == 0, 0.0, halo_ref[...])
    ext = jnp.concatenate([halo, u], axis=0)
    sums = []
    acc = ext
    for sh in (1, 2, 4, 8):
        acc = acc + pltpu.roll(acc, sh, axis=0)
        sums.append(acc)
    t_in = it * tm + lax.broadcasted_iota(jnp.int32, (tm, 1), 0)
    pool_parts = []
    for gi, win in enumerate(POOL_WINDOWS):
        sl = slice(gi * POOL_GROUP_WIDTH, (gi + 1) * POOL_GROUP_WIDTH)
        cnt = jnp.minimum(t_in + 1, win).astype(F32)
        pooled = sums[gi][POOL_HALO:, sl] / cnt - u[:, sl]
        pool_parts.append(jnp.dot(pooled.astype(BF16), wp_ref[gi], preferred_element_type=F32))
    pool_branch = jnp.concatenate(pool_parts, axis=1) * ps_ref[...]
    attn_branch = jnp.concatenate(
        [_tn_dot(o_ref[0, h], wh_ref[h]) for h in range(N_HEADS)], axis=1)
    y = jax.nn.sigmoid(m0_ref[...]) * attn_branch + jax.nn.sigmoid(m1_ref[...]) * pool_branch
    x1 = x_ref[...] + jnp.dot(y.astype(BF16), wo_ref[...], preferred_element_type=F32)
    y_ref[:, 0:D_MODEL] = x1
    inv = lax.rsqrt(jnp.mean(x1 * x1, axis=-1, keepdims=True) + RMS_EPS)
    logits = jnp.dot(x1 * inv * gf_ref[...], wr_ref[...], preferred_element_type=F32,
                     precision=lax.Precision.HIGHEST) + br_ref[...]
    y_ref[:, D_MODEL:D_MODEL + LANES] = _route(logits)


def _mix(x2, proj, o, wh, wp, ps, wo, gf, wr, br, s, tm=256):
    t = x2.shape[0]
    nt = s // tm
    hb = tm // POOL_HALO
    const = lambda shape: pl.BlockSpec(shape, lambda i: (0,) * len(shape))
    return pl.pallas_call(
        functools.partial(_mix_kernel, tm=tm, seq=s),
        grid=(t // tm,),
        in_specs=[
            pl.BlockSpec((tm, D_MODEL), lambda i: (i, 0)),
            pl.BlockSpec((tm, D_MODEL), lambda i: (i, COL_M0 // 2)),
            pl.BlockSpec((tm, D_MODEL), lambda i: (i, COL_M1 // 2)),
            pl.BlockSpec((tm, POOL_WIDTH), lambda i: (i, COL_POOL)),
            pl.BlockSpec((POOL_HALO, POOL_WIDTH), lambda i: (jnp.maximum(i * hb - 1, 0), COL_POOL)),
            pl.BlockSpec((1, N_HEADS, HEAD_DIM, tm), lambda i: (i // nt, 0, 0, i % nt)),
            const(wh.shape), const(wp.shape), const(ps.shape), const(wo.shape),
            const(gf.shape), const(wr.shape), const(br.shape),
        ],
        out_specs=pl.BlockSpec((tm, ROW_WIDTH), lambda i: (i, 0)),
        out_shape=jax.ShapeDtypeStruct((t, ROW_WIDTH), F32),
        compiler_params=_cparams(("parallel",)),
        name="mix",
    )(x2, proj, proj, proj, proj, o, wh, wp, ps, wo, gf, wr, br)


ROUTE_EXPERT_OFF = N_EXPERT_GROUPS
ROW_WIDTH = D_MODEL + LANES
META_GROUP_LANE = EXPERTS_PER_GROUP


def _route(logits):
    lane = lax.broadcasted_iota(jnp.int32, logits.shape, 1)
    is_g = lane < N_EXPERT_GROUPS
    gmax = jnp.max(jnp.where(is_g, logits, -jnp.inf), axis=-1, keepdims=True)
    ge = jnp.where(is_g, jnp.exp(logits - gmax), 0.0)
    gprob = ge / jnp.sum(ge, axis=-1, keepdims=True)
    p_group = jnp.max(jnp.where(is_g, gprob, -jnp.inf), axis=-1, keepdims=True)
    g_sel = jnp.min(jnp.where(is_g & (gprob == p_group), lane, LANES), axis=-1, keepdims=True)
    e_lane = lane - ROUTE_EXPERT_OFF
    in_grp = ((e_lane >= 0) & (e_lane < N_EXPERTS)
              & (jnp.right_shift(e_lane, EXPERTS_PER_GROUP.bit_length() - 1) == g_sel))
    el = jnp.where(in_grp, logits, -jnp.inf)
    v1 = jnp.max(el, axis=-1, keepdims=True)
    i1 = jnp.min(jnp.where(in_grp & (el == v1), lane, LANES), axis=-1, keepdims=True)
    el2 = jnp.where(lane == i1, -jnp.inf, el)
    v2 = jnp.max(el2, axis=-1, keepdims=True)
    i2 = jnp.min(jnp.where(in_grp & (lane != i1) & (el2 == v2), lane, LANES), axis=-1, keepdims=True)
    e2 = jnp.exp(v2 - v1)
    den = 1.0 + e2
    first = ROUTE_EXPERT_OFF + EXPERTS_PER_GROUP * g_sel
    return (jnp.where(lane == i1 - first, (1.0 / den) * p_group, 0.0)
            + jnp.where(lane == i2 - first, (e2 / den) * p_group, 0.0)
            + jnp.where(lane == META_GROUP_LANE, g_sel.astype(F32), 0.0))


def _moe_kernel(tgrp_ref, nval_ref, tok_ref, tok_next_ref, xa_ref, g_ref, wg_ref, wu_ref, wd_ref, out_ref,
                buf_ref, obuf_ref, sem_in, sem_out, *, tm, n_tiles):
    i = pl.program_id(0)
    slot = i % 2

    def row_in(tok_smem, r, s):
        return pltpu.make_async_copy(xa_ref.at[pl.ds(tok_smem[0, 0, r], 1)], buf_ref.at[s, pl.ds(r, 1)],
                                     sem_in.at[s])

    def row_out(r, s):
        return pltpu.make_async_copy(obuf_ref.at[s, pl.ds(r, 1)], out_ref.at[pl.ds(tok_ref[0, 0, r], 1)],
                                     sem_out.at[s])

    def start_gather(tok_smem, s):
        def body(r, carry):
            row_in(tok_smem, r, s).start()
            return carry
        lax.fori_loop(0, tm, body, 0, unroll=8)

    def wait_scatter(tile, s):
        n = nval_ref[tile]

        def chunk(c, carry):
            pltpu.make_async_copy(obuf_ref.at[s, pl.ds(0, WAIT_CHUNK)], out_ref.at[pl.ds(0, WAIT_CHUNK)],
                                  sem_out.at[s]).wait()
            return carry

        def row(r, carry):
            row_out(0, s).wait()
            return carry
        lax.fori_loop(0, jnp.right_shift(n, WAIT_CHUNK.bit_length() - 1), chunk, 0)
        lax.fori_loop(0, n & (WAIT_CHUNK - 1), row, 0)

    @pl.when(i == 0)
    def _():
        start_gather(tok_ref, 0)

    @pl.when((i + 1 < n_tiles) & (nval_ref[jnp.minimum(i + 1, n_tiles - 1)] > 0))
    def _():
        start_gather(tok_next_ref, 1 - slot)

    @pl.when(i >= 2)
    def _():
        wait_scatter(i - 2, slot)

    @pl.when(nval_ref[i] > 0)
    def _():
        pltpu.make_async_copy(xa_ref.at[pl.ds(0, tm)], buf_ref.at[slot], sem_in.at[slot]).wait()
        x = buf_ref[slot, :, 0:D_MODEL]
        meta = buf_ref[slot, :, D_MODEL:ROW_WIDTH]
        inv = lax.rsqrt(jnp.mean(x * x, axis=-1, keepdims=True) + RMS_EPS)
        h = (x * inv * g_ref[...]).astype(BF16)
        a = jnp.dot(h, wg_ref[0], preferred_element_type=F32)
        hid = a * jax.nn.sigmoid(a) * jnp.dot(h, wu_ref[0], preferred_element_type=F32)
        hid = jnp.concatenate(
            [hid[:, e * D_EXPERT:(e + 1) * D_EXPERT] * meta[:, e:e + 1] for e in range(EXPERTS_PER_GROUP)],
            axis=1)
        obuf_ref[slot] = x + jnp.dot(hid.astype(BF16), wd_ref[0], preferred_element_type=F32)

        n = nval_ref[i]
        n_main = n & ~(SCATTER_UNROLL - 1)

        def chunk(c, carry):
            for u in range(SCATTER_UNROLL):
                row_out(c * SCATTER_UNROLL + u, slot).start()
            return carry

        def row(r, carry):
            row_out(n_main + r, slot).start()
            return carry
        lax.fori_loop(0, jnp.right_shift(n, SCATTER_UNROLL.bit_length() - 1), chunk, 0)
        lax.fori_loop(0, n - n_main, row, 0)

    @pl.when(i == n_tiles - 1)
    def _():
        @pl.when(i >= 1)
        def _():
            wait_scatter(i - 1, 1 - slot)
        wait_scatter(i, slot)


def _moe(xa, g, tgrp, nval, tok, wg, wu, wd, tm):
    t = xa.shape[0]
    n_tiles = tok.shape[0]
    hid_w = EXPERTS_PER_GROUP * D_EXPERT
    smem_tile = lambda off: pl.BlockSpec(
        (1, 1, tm), lambda i, tg, nv: (jnp.minimum(i + off, n_tiles - 1), 0, 0), memory_space=pltpu.SMEM)
    grid_spec = pltpu.PrefetchScalarGridSpec(
        num_scalar_prefetch=2,
        grid=(n_tiles,),
        in_specs=[
            smem_tile(0), smem_tile(1),
            pl.BlockSpec(memory_space=pl.ANY),
            pl.BlockSpec((1, D_MODEL), lambda i, tg, nv: (0, 0)),
            pl.BlockSpec((1, D_MODEL, hid_w), lambda i, tg, nv: (tg[i], 0, 0)),
            pl.BlockSpec((1, D_MODEL, hid_w), lambda i, tg, nv: (tg[i], 0, 0)),
            pl.BlockSpec((1, hid_w, D_MODEL), lambda i, tg, nv: (tg[i], 0, 0)),
        ],
        out_specs=pl.BlockSpec(memory_space=pl.ANY),
        scratch_shapes=[pltpu.VMEM((2, tm, ROW_WIDTH), F32), pltpu.VMEM((2, tm, D_MODEL), F32),
                        pltpu.SemaphoreType.DMA((2,)), pltpu.SemaphoreType.DMA((2,))],
    )
    return pl.pallas_call(
        functools.partial(_moe_kernel, tm=tm, n_tiles=n_tiles),
        grid_spec=grid_spec,
        out_shape=jax.ShapeDtypeStruct((t, D_MODEL), F32),
        compiler_params=_cparams(("arbitrary",)),
        name="moe",
    )(tgrp, nval, tok, tok, xa, g, wg, wu, wd)


MOE_TM = 512
WAIT_CHUNK = 64
SCATTER_UNROLL = 8


def _moe_plan(gid, tm):
    t = gid.shape[0]
    n_tiles = t // tm + N_EXPERT_GROUPS
    order = jnp.argsort(gid, stable=True).astype(jnp.int32)
    counts = jnp.sum((gid[:, None] == jnp.arange(N_EXPERT_GROUPS)[None, :]).astype(jnp.int32), axis=0)
    tiles_per = (counts + tm - 1) // tm
    tile_end = jnp.cumsum(tiles_per)
    row_start = jnp.cumsum(counts) - counts
    tile = jnp.arange(n_tiles, dtype=jnp.int32)
    tgrp = jnp.minimum(jnp.sum((tile[:, None] >= tile_end[None, :]).astype(jnp.int32), axis=1),
                       N_EXPERT_GROUPS - 1)
    local = tile - (tile_end - tiles_per)[tgrp]
    nval = jnp.clip(counts[tgrp] - local * tm, 0, tm)
    slot = jnp.arange(tm, dtype=jnp.int32)[None, :]
    src = row_start[tgrp][:, None] + local[:, None] * tm + jnp.where(slot < nval[:, None], slot, 0)
    tok = order[jnp.clip(src, 0, t - 1)]
    return tgrp.astype(jnp.int32), nval.astype(jnp.int32), tok.reshape(n_tiles, 1, tm)


def _rope_lane_freq():
    inv_freq = ROPE_THETA ** (-jnp.arange(0, ROPE_DIM, 2, dtype=F32) / ROPE_DIM)
    d = np.arange(LANES) % HEAD_DIM
    f = jnp.where(jnp.asarray(d < ROPE_DIM), inv_freq[np.asarray(d % (ROPE_DIM // 2))], 0.0)
    return f.reshape(1, LANES).astype(F32)


def _head_block_ones(width):
    idx = np.arange(width) // HEAD_DIM
    return jnp.asarray(idx[:, None] == idx[None, :], dtype=BF16)


def _overlap_t(n_cmp_pad, n_blk):
    a0 = np.arange(n_cmp_pad)[:, None] * CMP_STRIDE
    b0 = np.arange(n_blk)[None, :] * SEL_BLOCK
    ov = np.clip(np.minimum(a0 + CMP_BLOCK, b0 + SEL_BLOCK) - np.maximum(a0, b0), 0, None) / CMP_BLOCK
    return jnp.asarray(ov.T, dtype=BF16)


def _bias_place(n_blk):
    return jnp.asarray(np.arange(n_blk)[:, None] + BLK_LANE0 == np.arange(LANES)[None, :], dtype=BF16)


def _layer(x, positions, attn_norm_g, w_in, q_norm_g, k_norm_cmp_g, k_norm_slc_g, k_norm_swa_g,
           cmp_pos_emb_k, cmp_w1_k, cmp_w2_k, cmp_pos_emb_v, cmp_w1_v, cmp_w2_v, w_head_out, w_pool,
           pool_scale, w_out, ffn_norm_g, w_router_group, b_router_group, w_router_expert,
           b_router_expert, w_expert_gate, w_expert_up, w_expert_down):
    b, s, d = x.shape
    t = b * s
    x2 = x.reshape(t, d)

    o_q, o_kv, o_g = ATTN_WIDTH, ATTN_WIDTH + 6 * KV_WIDTH, ATTN_WIDTH + 6 * KV_WIDTH + N_HEADS * N_BRANCH
    o_p = o_g + POOL_WIDTH
    used = 2 * ATTN_WIDTH + 2 * D_MODEL + 6 * KV_WIDTH + N_HEADS * N_BRANCH
    w_packed = jnp.concatenate(
        [w_in[:, :o_q], w_in[:, o_g:o_p], w_in[:, o_p:], w_in[:, o_q:o_kv], w_in[:, o_kv:o_g],
         jnp.zeros((d, PROJ_WIDTH - used), w_in.dtype)], axis=1).astype(BF16)
    proj = _proj(x2, attn_norm_g.reshape(1, d), w_packed)

    f_lane = _rope_lane_freq()
    tile_h = lambda g_, n: jnp.tile(g_, n).reshape(1, n * HEAD_DIM)
    q, ks, vs, kw, vw, kc_raw, vc_raw, gates = _prep(
        proj, positions.reshape(t, 1), f_lane, tile_h(q_norm_g, N_HEADS), tile_h(k_norm_slc_g, N_KV),
        tile_h(k_norm_swa_g, N_KV), _head_block_ones(ATTN_WIDTH), _head_block_ones(KV_WIDTH), b, s)

    n_row = s // CMP_STRIDE
    half = CMP_STRIDE * HEAD_DIM
    xk = kc_raw.reshape(b, N_KV, n_row, half)
    xv = vc_raw.reshape(b, N_KV, n_row, half)
    pad2 = lambda w2: jnp.pad(w2, ((0, 0), (0, LANES - HEAD_DIM))).astype(BF16)
    cmp_end = np.minimum(np.arange(n_row) * CMP_STRIDE + CMP_BLOCK - 1, s - 1)
    posc = positions[:, cmp_end].reshape(b, n_row, 1)
    gk = jnp.pad(k_norm_cmp_g, (0, LANES - HEAD_DIM)).reshape(1, LANES)
    kc, vc = _compress(xk, xv, cmp_pos_emb_k.reshape(2, half), cmp_pos_emb_v.reshape(2, half),
                       cmp_w1_k.astype(BF16), pad2(cmp_w2_k), cmp_w1_v.astype(BF16), pad2(cmp_w2_v),
                       gk, posc, f_lane)

    n_blk = s // SEL_BLOCK
    assert n_blk <= PAD_LANE - BLK_LANE0
    front = jnp.zeros((b, N_KV, WINDOW, LANES), BF16).at[..., PAD_LANE].set(1.0)
    kw = jnp.concatenate([front, kw], axis=2)
    vw = jnp.concatenate([jnp.zeros((b, KV_WIDTH, WINDOW), BF16), vw], axis=2)
    o = _attn(q, kc, vc, ks, vs, kw, vw, gates, _overlap_t(n_row, n_blk), _bias_place(n_blk))

    wr = jnp.concatenate([w_router_group, w_router_expert,
                          jnp.zeros((d, LANES - N_EXPERT_GROUPS - N_EXPERTS), F32)], axis=1)
    br = jnp.concatenate([b_router_group, b_router_expert,
                          jnp.zeros((LANES - N_EXPERT_GROUPS - N_EXPERTS,), F32)]).reshape(1, LANES)
    gf = ffn_norm_g.reshape(1, d)
    xa = _mix(x2, proj, o, w_head_out.astype(BF16), w_pool.astype(BF16), pool_scale.reshape(1, d),
              w_out.astype(BF16), gf, wr, br, s)

    tgrp, nval, tok = _moe_plan(xa[:, D_MODEL + META_GROUP_LANE].astype(jnp.int32), MOE_TM)
    grp_cols = lambda w: (w.reshape(N_EXPERT_GROUPS, EXPERTS_PER_GROUP, d, D_EXPERT)
                          .transpose(0, 2, 1, 3).reshape(N_EXPERT_GROUPS, d, EXPERTS_PER_GROUP * D_EXPERT))
    wd = w_expert_down.reshape(N_EXPERT_GROUPS, EXPERTS_PER_GROUP * D_EXPERT, d)
    x_out = _moe(xa, gf, tgrp, nval, tok, grp_cols(w_expert_gate.astype(BF16)),
                 grp_cols(w_expert_up.astype(BF16)), wd.astype(BF16), MOE_TM)
    return x_out.reshape(b, s, d)


def kernel(x, positions, attn_norm_g, w_in, q_norm_g, k_norm_cmp_g, k_norm_slc_g, k_norm_swa_g,
           cmp_pos_emb_k, cmp_w1_k, cmp_w2_k, cmp_pos_emb_v, cmp_w1_v, cmp_w2_v, w_head_out, w_pool,
           pool_scale, w_out, ffn_norm_g, w_router_group, b_router_group, w_router_expert,
           b_router_expert, w_expert_gate, w_expert_up, w_expert_down):
    params = (attn_norm_g, w_in, q_norm_g, k_norm_cmp_g, k_norm_slc_g, k_norm_swa_g, cmp_pos_emb_k,
              cmp_w1_k, cmp_w2_k, cmp_pos_emb_v, cmp_w1_v, cmp_w2_v, w_head_out, w_pool, pool_scale,
              w_out, ffn_norm_g, w_router_group, b_router_group, w_router_expert, b_router_expert,
              w_expert_gate, w_expert_up, w_expert_down)
    for layer in range(attn_norm_g.shape[0]):
        x = _layer(x, positions, *[p[layer] for p in params])
    return x
```

```python
import functools

import jax
import jax.numpy as jnp
import numpy as np
from jax import lax
from jax.experimental import pallas as pl
from jax.experimental.pallas import tpu as pltpu

F32 = jnp.float32
BF16 = jnp.bfloat16

D_MODEL = 2048
N_HEADS = 16
HEAD_DIM = 64
N_KV = 4
HEADS_PER_KV = N_HEADS // N_KV
ATTN_WIDTH = N_HEADS * HEAD_DIM
KV_WIDTH = N_KV * HEAD_DIM
N_BRANCH = 3
CMP_BLOCK = 32
CMP_STRIDE = 16
CMP_HIDDEN = 4 * HEAD_DIM
SEL_BLOCK = 64
N_SEL = 16
N_FORCED_LOCAL = 2
WINDOW = 512
FORCE_BONUS = 1e4
NEG_INF = -1e30
ROPE_THETA = 500000.0
ROPE_DIM = HEAD_DIM // 4
HEAD_OUT_WIDTH = D_MODEL // N_HEADS
POOL_WINDOWS = (2, 4, 8, 16)
POOL_GROUPS = 4
POOL_WIDTH = 1024
POOL_GROUP_WIDTH = POOL_WIDTH // POOL_GROUPS
POOL_OUT_WIDTH = D_MODEL // POOL_GROUPS
N_EXPERT_GROUPS = 4
EXPERTS_PER_GROUP = 4
N_EXPERTS = N_EXPERT_GROUPS * EXPERTS_PER_GROUP
D_EXPERT = 256
RMS_EPS = 1e-6

LANES = 128
Q_SCALE = HEAD_DIM ** -0.5 * float(np.log2(np.e))
BLK_LANE0 = HEAD_DIM
PAD_LANE = HEAD_DIM + 32
V_ONES = 16
PROJ_WIDTH = 8192
PROJ_TN = 1024
COL_Q, COL_POOL, COL_M0, COL_M1, COL_KVA, COL_KVB = 0, 1, 2, 4, 6, 7
GATE_OFF = 2 * KV_WIDTH
VMEM_LIMIT = 56 * 1024 * 1024


def _cparams(sem):
    return pltpu.CompilerParams(dimension_semantics=sem, vmem_limit_bytes=VMEM_LIMIT)


def _nt_dot(a, b):
    return lax.dot_general(a, b, (((1,), (1,)), ((), ())), preferred_element_type=F32)


def _tn_dot(a, b):
    return lax.dot_general(a, b, (((0,), (0,)), ((), ())), preferred_element_type=F32)


def _split_dot(x, e):
    hi = x.astype(BF16)
    lo = (x - hi.astype(F32)).astype(BF16)
    return (jnp.dot(hi, e, preferred_element_type=F32)
            + jnp.dot(lo, e, preferred_element_type=F32))


def _proj_kernel(x_ref, g_ref, w_ref, o_ref, h_ref):
    @pl.when(pl.program_id(1) == 0)
    def _():
        x = x_ref[...]
        inv = lax.rsqrt(jnp.mean(x * x, axis=-1, keepdims=True) + RMS_EPS)
        h_ref[...] = (x * inv * g_ref[...]).astype(BF16)

    o_ref[...] = jnp.dot(h_ref[...], w_ref[...], preferred_element_type=F32)


def _proj(x2, g, w_packed, tm=1024):
    t = x2.shape[0]
    return pl.pallas_call(
        _proj_kernel,
        grid=(t // tm, PROJ_WIDTH // PROJ_TN),
        in_specs=[
            pl.BlockSpec((tm, D_MODEL), lambda i, j: (i, 0)),
            pl.BlockSpec((1, D_MODEL), lambda i, j: (0, 0)),
            pl.BlockSpec((D_MODEL, PROJ_TN), lambda i, j: (0, j)),
        ],
        out_specs=pl.BlockSpec((tm, PROJ_TN), lambda i, j: (i, j)),
        out_shape=jax.ShapeDtypeStruct((t, PROJ_WIDTH), F32),
        scratch_shapes=[pltpu.VMEM((tm, D_MODEL), BF16)],
        compiler_params=_cparams(("parallel", "arbitrary")),
        name="proj",
    )(x2, g, w_packed)


def _rope(xn, cos, sin):
    w = xn.shape[-1]
    lane = lax.broadcasted_iota(jnp.int32, xn.shape, 1) & (HEAD_DIM - 1)
    up = pltpu.roll(xn, w - ROPE_DIM // 2, axis=1)
    dn = pltpu.roll(xn, ROPE_DIM // 2, axis=1)
    return jnp.where(lane < ROPE_DIM // 2, xn * cos - up * sin,
                     jnp.where(lane < ROPE_DIM, xn * cos + dn * sin, xn))


def _head_norm_rope(x, gain, e_blk, cos, sin):
    w = x.shape[-1]
    ssq = _split_dot(x * x, e_blk)
    xn = x * lax.rsqrt(ssq * (1.0 / HEAD_DIM) + RMS_EPS) * gain
    reps = w // LANES
    return _rope(xn, jnp.concatenate([cos] * reps, axis=1), jnp.concatenate([sin] * reps, axis=1))


def _widen_head(x, h, upper):
    col = x[:, (h // 2) * LANES:(h // 2 + 1) * LANES]
    if h % 2:
        col = pltpu.roll(col, HEAD_DIM, axis=1)
    lane = lax.broadcasted_iota(jnp.int32, col.shape, 1)
    return jnp.where(lane < HEAD_DIM, col, upper)


def _prep_kernel(q_ref, kva_ref, kvb_ref, pos_ref, f_ref, gq_ref, gs_ref, gw_ref, eq_ref, ek_ref,
                 qo_ref, kso_ref, vso_ref, kwo_ref, vwo_ref, kco_ref, vco_ref, go_ref):
    tm = q_ref.shape[0]
    ang = pos_ref[...].astype(F32) * f_ref[...]
    cos, sin = jnp.cos(ang), jnp.sin(ang)
    q = _head_norm_rope(q_ref[...], gq_ref[...], eq_ref[...], cos, sin) * Q_SCALE
    lane = lax.broadcasted_iota(jnp.int32, (tm, LANES), 1)
    t_in = pl.program_id(1) * tm + lax.broadcasted_iota(jnp.int32, (tm, LANES), 0)
    q_upper = jnp.where(lane == PAD_LANE, NEG_INF, 0.0)
    blk_onehot = (jnp.right_shift(t_in, SEL_BLOCK.bit_length() - 1) == lane - BLK_LANE0).astype(F32)
    for h in range(N_HEADS):
        qo_ref[0, h // HEADS_PER_KV, h % HEADS_PER_KV] = _widen_head(q, h, q_upper).astype(BF16)
    kva = kva_ref[...]
    kvb = kvb_ref[...]
    ks = _head_norm_rope(kva[:, 2 * KV_WIDTH:3 * KV_WIDTH], gs_ref[...], ek_ref[...], cos, sin)
    kw = _head_norm_rope(kvb[:, 0:KV_WIDTH], gw_ref[...], ek_ref[...], cos, sin)
    vs = kva[:, 3 * KV_WIDTH:4 * KV_WIDTH]
    vw = kvb[:, KV_WIDTH:2 * KV_WIDTH]
    kc = kva[:, 0:KV_WIDTH]
    vc = kva[:, KV_WIDTH:2 * KV_WIDTH]
    for g in range(N_KV):
        sl = slice(g * HEAD_DIM, (g + 1) * HEAD_DIM)
        kso_ref[0, g] = _widen_head(ks, g, blk_onehot).astype(BF16)
        kwo_ref[0, g] = _widen_head(kw, g, 0.0).astype(BF16)
        kco_ref[0, g] = kc[:, sl]
        vco_ref[0, g] = vc[:, sl]
    ones = jnp.ones((V_ONES, tm), BF16)
    for v, vo_ref in ((vs, vso_ref), (vw, vwo_ref)):
        vt = v.T.astype(BF16)
        for g in range(N_KV):
            vo_ref[0, g, 0:HEAD_DIM, :] = vt[g * HEAD_DIM:(g + 1) * HEAD_DIM]
            vo_ref[0, g, HEAD_DIM:HEAD_DIM + V_ONES, :] = ones
    go_ref[0] = jax.nn.sigmoid(kvb[:, GATE_OFF:GATE_OFF + LANES]).T


def _prep(proj, pos2, f_lane, gq, gs, gw, e_q, e_k, b, s, tm=512):
    nt = s // tm
    row = lambda bb, i: bb * nt + i
    kv_shape = jax.ShapeDtypeStruct((b, N_KV, s, HEAD_DIM), F32)
    kv_spec = pl.BlockSpec((1, N_KV, tm, HEAD_DIM), lambda bb, i: (bb, 0, i, 0))
    kx_shape = jax.ShapeDtypeStruct((b, N_KV, s, LANES), BF16)
    kx_spec = pl.BlockSpec((1, N_KV, tm, LANES), lambda bb, i: (bb, 0, i, 0))
    vt_shape = jax.ShapeDtypeStruct((b, N_KV, HEAD_DIM + V_ONES, s), BF16)
    vt_spec = pl.BlockSpec((1, N_KV, HEAD_DIM + V_ONES, tm), lambda bb, i: (bb, 0, 0, i))
    const = lambda shape: pl.BlockSpec(shape, lambda bb, i: (0,) * len(shape))
    return pl.pallas_call(
        _prep_kernel,
        grid=(b, nt),
        in_specs=[
            pl.BlockSpec((tm, PROJ_TN), lambda bb, i: (row(bb, i), COL_Q)),
            pl.BlockSpec((tm, PROJ_TN), lambda bb, i: (row(bb, i), COL_KVA)),
            pl.BlockSpec((tm, PROJ_TN), lambda bb, i: (row(bb, i), COL_KVB)),
            pl.BlockSpec((tm, 1), lambda bb, i: (row(bb, i), 0)),
            const((1, LANES)), const((1, ATTN_WIDTH)), const((1, KV_WIDTH)), const((1, KV_WIDTH)),
            const((ATTN_WIDTH, ATTN_WIDTH)), const((KV_WIDTH, KV_WIDTH)),
        ],
        out_specs=[
            pl.BlockSpec((1, N_KV, HEADS_PER_KV, tm, LANES), lambda bb, i: (bb, 0, 0, i, 0)),
            kx_spec, vt_spec, kx_spec, vt_spec, kv_spec, kv_spec,
            pl.BlockSpec((1, LANES, tm), lambda bb, i: (bb, 0, i)),
        ],
        out_shape=[
            jax.ShapeDtypeStruct((b, N_KV, HEADS_PER_KV, s, LANES), BF16),
            kx_shape, vt_shape, kx_shape, vt_shape, kv_shape, kv_shape,
            jax.ShapeDtypeStruct((b, LANES, s), F32),
        ],
        compiler_params=_cparams(("parallel", "parallel")),
        name="prep",
    )(proj, proj, proj, pos2, f_lane, gq, gs, gw, e_q, e_k)


def _compress_one(x, pe_a, pe_b, w1_ref, w2_ref):
    half = CMP_STRIDE * HEAD_DIM
    nxt = pltpu.roll(x, x.shape[0] - 1, axis=0)
    a = (x + pe_a).astype(BF16)
    b = (nxt + pe_b).astype(BF16)
    hid = (jnp.dot(a, w1_ref[0:half, :], preferred_element_type=F32)
           + jnp.dot(b, w1_ref[half:2 * half, :], preferred_element_type=F32))
    hid = hid * jax.nn.sigmoid(hid)
    return jnp.dot(hid.astype(BF16), w2_ref[...], preferred_element_type=F32)


def _compress_kernel(xk_ref, xv_ref, pek_ref, pev_ref, w1k_ref, w2k_ref, w1v_ref, w2v_ref,
                     gk_ref, posc_ref, f_ref, kc_ref, vc_ref):
    ang = posc_ref[0].astype(F32) * f_ref[...]
    cos, sin = jnp.cos(ang), jnp.sin(ang)
    for g in range(N_KV):
        k = _compress_one(xk_ref[0, g], pek_ref[0:1, :], pek_ref[1:2, :], w1k_ref, w2k_ref)
        ssq = jnp.sum(k * k, axis=-1, keepdims=True)
        kn = k * lax.rsqrt(ssq * (1.0 / HEAD_DIM) + RMS_EPS) * gk_ref[...]
        kn = _rope(kn, cos, sin)
        kc_ref[0, g] = kn.astype(BF16)
        v = _compress_one(xv_ref[0, g], pev_ref[0:1, :], pev_ref[1:2, :], w1v_ref, w2v_ref)
        vc_ref[0, g] = v[:, 0:HEAD_DIM].astype(BF16)


def _compress(xk, xv, pek, pev, w1k, w2k, w1v, w2v, gk, posc, f_lane):
    b = xk.shape[0]
    nrow = xk.shape[2]
    half = CMP_STRIDE * HEAD_DIM
    const = lambda shape: pl.BlockSpec(shape, lambda bb: (0,) * len(shape))
    x_spec = pl.BlockSpec((1, N_KV, nrow, half), lambda bb: (bb, 0, 0, 0))
    o_spec = pl.BlockSpec((1, N_KV, nrow, HEAD_DIM), lambda bb: (bb, 0, 0, 0))
    o_shape = jax.ShapeDtypeStruct((b, N_KV, nrow, HEAD_DIM), BF16)
    k_spec = pl.BlockSpec((1, N_KV, nrow, LANES), lambda bb: (bb, 0, 0, 0))
    k_shape = jax.ShapeDtypeStruct((b, N_KV, nrow, LANES), BF16)
    return pl.pallas_call(
        _compress_kernel,
        grid=(b,),
        in_specs=[x_spec, x_spec, const((2, half)), const((2, half)),
                  const((2 * half, CMP_HIDDEN)), const((CMP_HIDDEN, LANES)),
                  const((2 * half, CMP_HIDDEN)), const((CMP_HIDDEN, LANES)),
                  const((1, LANES)), pl.BlockSpec((1, nrow, 1), lambda bb: (bb, 0, 0)), const((1, LANES))],
        out_specs=[k_spec, o_spec],
        out_shape=[k_shape, o_shape],
        compiler_params=_cparams(("parallel",)),
        name="compress",
    )(xk, xv, pek, pev, w1k, w2k, w1v, w2v, gk, posc, f_lane)


def _attn_kernel(q_ref, kc_ref, vc_ref, ks_ref, vs_ref, kw_ref, vw_ref, gate_ref, ovl_ref, place_ref,
                 o_ref, *, tq, seq):
    r = HEADS_PER_KV
    cols = r * tq
    n_cmp = kc_ref.shape[2]
    n_blk = seq // SEL_BLOCK
    i = pl.program_id(1)
    t0 = i * tq
    gates = gate_ref[0]

    kk = lax.broadcasted_iota(jnp.int32, (tq, cols), 0)
    tt = lax.broadcasted_iota(jnp.int32, (tq, cols), 1) & (tq - 1)
    causal_bias = jnp.where(kk <= tt, 0.0, NEG_INF)
    band_bias = jnp.where(kk > tt, 0.0, NEG_INF)

    q_plain, q_sel, o_cmps = [], [], []

    for g in range(N_KV):
        q3 = q_ref[0, g]
        qg = q3.reshape(cols, LANES)

        s = _nt_dot(kc_ref[0, g], qg)
        n_idx = lax.broadcasted_iota(jnp.int32, (n_cmp, cols), 0)
        t_col = t0 + (lax.broadcasted_iota(jnp.int32, (n_cmp, cols), 1) & (tq - 1))
        mask = (n_idx * CMP_STRIDE + (CMP_BLOCK - 1)) <= t_col
        s = jnp.where(mask, s, NEG_INF)
        p = jnp.where(mask, jnp.exp2(s - jnp.max(s, axis=0, keepdims=True)), 0.0)
        l = jnp.sum(p, axis=0, keepdims=True)
        p = p * (1.0 / jnp.where(l > 0.0, l, 1.0))
        o_cmp = _tn_dot(vc_ref[0, g], p.astype(BF16))

        p_sum = p[:, 0:tq] + p[:, tq:2 * tq] + p[:, 2 * tq:3 * tq] + p[:, 3 * tq:4 * tq]
        hi = p_sum.astype(BF16)
        lo = (p_sum - hi.astype(F32)).astype(BF16)
        imp = (jnp.dot(ovl_ref[...], hi, preferred_element_type=F32)
               + jnp.dot(ovl_ref[...], lo, preferred_element_type=F32))
        j_idx = lax.broadcasted_iota(jnp.int32, (n_blk, tq), 0)
        t_idx = t0 + lax.broadcasted_iota(jnp.int32, (n_blk, tq), 1)
        dist = jnp.right_shift(t_idx, SEL_BLOCK.bit_length() - 1) - j_idx
        valid = dist >= 0
        forced = (j_idx == 0) | (valid & (dist < N_FORCED_LOCAL))
        score = jnp.where(valid, imp + FORCE_BONUS * forced.astype(F32), NEG_INF)
        cnt = jnp.zeros((n_blk, tq), F32)
        for jp in range(n_blk):
            row = score[jp:jp + 1, :]
            beats = (row > score) | ((row == score) & (j_idx > jp))
            cnt = cnt + beats.astype(F32)
        selected = (cnt < float(min(N_SEL, n_blk))) & valid
        blk_bias = jnp.where(selected, 0.0, NEG_INF).astype(BF16)
        q_bias = _tn_dot(blk_bias, place_ref[...]).astype(BF16)
        q_plain.append(qg)
        q_sel.append((q3 + q_bias[None]).reshape(cols, LANES))
        o_cmps.append(o_cmp)

    def sel_tile(g, k0, carry, bias):
        m_i, acc = carry
        s = _nt_dot(ks_ref[0, g, pl.ds(k0, tq), :], q_sel[g])
        if bias is not None:
            s = s + bias
        m_new = jnp.maximum(m_i, jnp.max(s, axis=0, keepdims=True))
        p = jnp.exp2((s - m_new).astype(BF16))
        acc = jnp.exp2(m_i - m_new) * acc + jnp.dot(vs_ref[0, g, :, pl.ds(k0, tq)], p,
                                                    preferred_element_type=F32)
        return m_new, acc

    def sel_body(kt, carries):
        k0 = pl.multiple_of(kt * tq, tq)
        return tuple(sel_tile(g, k0, carries[g], None) for g in range(N_KV))

    init = (jnp.full((1, cols), NEG_INF, F32), jnp.zeros((HEAD_DIM + V_ONES, cols), F32))
    carries = lax.fori_loop(0, i, sel_body, (init,) * N_KV)

    for g in range(N_KV):
        _, acc_s = sel_tile(g, pl.multiple_of(t0, tq), carries[g], causal_bias)
        o_slc = acc_s[0:HEAD_DIM] * (1.0 / acc_s[HEAD_DIM:HEAD_DIM + 1])

        span = WINDOW + tq
        w0 = pl.multiple_of(t0, tq)
        s = _nt_dot(kw_ref[0, g, pl.ds(w0, span), :], q_plain[g])
        s = jnp.concatenate([s[0:tq] + band_bias, s[tq:WINDOW], s[WINDOW:span] + causal_bias], axis=0)
        p = jnp.exp2((s - jnp.max(s, axis=0, keepdims=True)).astype(BF16))
        acc_w = jnp.dot(vw_ref[0, g, :, pl.ds(w0, span)], p, preferred_element_type=F32)
        o_swa = acc_w[0:HEAD_DIM] * (1.0 / acc_w[HEAD_DIM:HEAD_DIM + 1])

        def gate_row(c):
            parts = [gates[(g * r + hh) * N_BRANCH + c:(g * r + hh) * N_BRANCH + c + 1, :] for hh in range(r)]
            return jnp.concatenate(parts, axis=1)

        o = gate_row(0) * o_cmps[g] + gate_row(1) * o_slc + gate_row(2) * o_swa
        for hh in range(r):
            o_ref[0, g * r + hh] = o[:, hh * tq:(hh + 1) * tq].astype(BF16)


def _attn(q, kc, vc, ks, vs, kw, vw, gates, ovl_t, place, tq=256):
    b, _, _, s, _ = q.shape
    n_cmp = kc.shape[2]
    nq = s // tq
    assert WINDOW % tq == 0 and tq % SEL_BLOCK == 0
    full = lambda shape: pl.BlockSpec((1,) + shape, lambda bb, i: (bb,) + (0,) * len(shape))
    const = lambda shape: pl.BlockSpec(shape, lambda bb, i: (0,) * len(shape))
    return pl.pallas_call(
        functools.partial(_attn_kernel, tq=tq, seq=s),
        grid=(b, nq),
        in_specs=[
            pl.BlockSpec((1, N_KV, HEADS_PER_KV, tq, LANES), lambda bb, i: (bb, 0, 0, i, 0)),
            full((N_KV, n_cmp, LANES)), full((N_KV, n_cmp, HEAD_DIM)),
            full((N_KV, s, LANES)), full((N_KV, HEAD_DIM + V_ONES, s)),
            full((N_KV, s + WINDOW, LANES)), full((N_KV, HEAD_DIM + V_ONES, s + WINDOW)),
            pl.BlockSpec((1, LANES, tq), lambda bb, i: (bb, 0, i)),
            const(ovl_t.shape), const(place.shape),
        ],
        out_specs=pl.BlockSpec((1, N_HEADS, HEAD_DIM, tq), lambda bb, i: (bb, 0, 0, i)),
        out_shape=jax.ShapeDtypeStruct((b, N_HEADS, HEAD_DIM, s), BF16),
        compiler_params=_cparams(("parallel", "arbitrary")),
        name="attn",
    )(q, kc, vc, ks, vs, kw, vw, gates, ovl_t, place)


POOL_HALO = 16


def _sigmoid(x):
    return 0.5 * jnp.tanh(0.5 * x) + 0.5


def _mix_kernel(x_ref, m0_ref, m1_ref, u_ref, halo_ref, o_ref, wh_ref, wp_ref, ps_ref, wo_ref,
                gf_ref, wr_ref, br_ref, y_ref, *, tm, seq):
    it = pl.program_id(0) % (seq // tm)
    u = u_ref[...]
    halo = jnp.where(it == 0, 0.0, halo_ref[...])
    ext = jnp.concatenate([halo, u], axis=0)
    sums = []
    acc = ext
    for sh in (1, 2, 4, 8):
        acc = acc + pltpu.roll(acc, sh, axis=0)
        sums.append(acc)
    t_in = it * tm + lax.broadcasted_iota(jnp.int32, (tm, 1), 0)
    pool_parts = []
    for gi, win in enumerate(POOL_WINDOWS):
        sl = slice(gi * POOL_GROUP_WIDTH, (gi + 1) * POOL_GROUP_WIDTH)
        cnt = jnp.minimum(t_in + 1, win).astype(F32)
        pooled = sums[gi][POOL_HALO:, sl] / cnt - u[:, sl]
        pool_parts.append(jnp.dot(pooled.astype(BF16), wp_ref[gi], preferred_element_type=F32))
    pool_branch = jnp.concatenate(pool_parts, axis=1) * ps_ref[...]
    attn_branch = jnp.concatenate(
        [_tn_dot(o_ref[0, h], wh_ref[h]) for h in range(N_HEADS)], axis=1)
    y = _sigmoid(m0_ref[...]) * attn_branch + _sigmoid(m1_ref[...]) * pool_branch
    x1 = x_ref[...] + jnp.dot(y.astype(BF16), wo_ref[...], preferred_element_type=F32)
    y_ref[:, 0:D_MODEL] = x1
    inv = lax.rsqrt(jnp.mean(x1 * x1, axis=-1, keepdims=True) + RMS_EPS)
    ht = x1 * inv * gf_ref[...]
    h_hi = ht.astype(BF16)
    h_lo = (ht - h_hi.astype(F32)).astype(BF16)
    logits = (jnp.dot(h_hi, wr_ref[0], preferred_element_type=F32)
              + jnp.dot(h_lo, wr_ref[0], preferred_element_type=F32)
              + jnp.dot(h_hi, wr_ref[1], preferred_element_type=F32)) + br_ref[...]
    y_ref[:, D_MODEL:D_MODEL + LANES] = _route(logits)


def _mix(x2, proj, o, wh, wp, ps, wo, gf, wr, br, s, tm=256):
    t = x2.shape[0]
    nt = s // tm
    hb = tm // POOL_HALO
    const = lambda shape: pl.BlockSpec(shape, lambda i: (0,) * len(shape))
    return pl.pallas_call(
        functools.partial(_mix_kernel, tm=tm, seq=s),
        grid=(t // tm,),
        in_specs=[
            pl.BlockSpec((tm, D_MODEL), lambda i: (i, 0)),
            pl.BlockSpec((tm, D_MODEL), lambda i: (i, COL_M0 // 2)),
            pl.BlockSpec((tm, D_MODEL), lambda i: (i, COL_M1 // 2)),
            pl.BlockSpec((tm, POOL_WIDTH), lambda i: (i, COL_POOL)),
            pl.BlockSpec((POOL_HALO, POOL_WIDTH), lambda i: (jnp.maximum(i * hb - 1, 0), COL_POOL)),
            pl.BlockSpec((1, N_HEADS, HEAD_DIM, tm), lambda i: (i // nt, 0, 0, i % nt)),
            const(wh.shape), const(wp.shape), const(ps.shape), const(wo.shape),
            const(gf.shape), const(wr.shape), const(br.shape),
        ],
        out_specs=pl.BlockSpec((tm, ROW_WIDTH), lambda i: (i, 0)),
        out_shape=jax.ShapeDtypeStruct((t, ROW_WIDTH), F32),
        compiler_params=_cparams(("parallel",)),
        name="mix",
    )(x2, proj, proj, proj, proj, o, wh, wp, ps, wo, gf, wr, br)


ROUTE_EXPERT_OFF = N_EXPERT_GROUPS
ROW_WIDTH = D_MODEL + LANES
META_GROUP_LANE = EXPERTS_PER_GROUP


def _route(logits):
    lane = lax.broadcasted_iota(jnp.int32, logits.shape, 1)
    is_g = lane < N_EXPERT_GROUPS
    gmax = jnp.max(jnp.where(is_g, logits, -jnp.inf), axis=-1, keepdims=True)
    ge = jnp.where(is_g, jnp.exp(logits - gmax), 0.0)
    gprob = ge / jnp.sum(ge, axis=-1, keepdims=True)
    p_group = jnp.max(jnp.where(is_g, gprob, -jnp.inf), axis=-1, keepdims=True)
    g_sel = jnp.min(jnp.where(is_g & (gprob == p_group), lane, LANES), axis=-1, keepdims=True)
    e_lane = lane - ROUTE_EXPERT_OFF
    in_grp = ((e_lane >= 0) & (e_lane < N_EXPERTS)
              & (jnp.right_shift(e_lane, EXPERTS_PER_GROUP.bit_length() - 1) == g_sel))
    el = jnp.where(in_grp, logits, -jnp.inf)
    v1 = jnp.max(el, axis=-1, keepdims=True)
    i1 = jnp.min(jnp.where(in_grp & (el == v1), lane, LANES), axis=-1, keepdims=True)
    el2 = jnp.where(lane == i1, -jnp.inf, el)
    v2 = jnp.max(el2, axis=-1, keepdims=True)
    i2 = jnp.min(jnp.where(in_grp & (lane != i1) & (el2 == v2), lane, LANES), axis=-1, keepdims=True)
    e2 = jnp.exp(v2 - v1)
    den = 1.0 + e2
    first = ROUTE_EXPERT_OFF + EXPERTS_PER_GROUP * g_sel
    return (jnp.where(lane == i1 - first, (1.0 / den) * p_group, 0.0)
            + jnp.where(lane == i2 - first, (e2 / den) * p_group, 0.0)
            + jnp.where(lane == META_GROUP_LANE, g_sel.astype(F32), 0.0))


def _moe_kernel(tgrp_ref, nval_ref, tok_ref, tok_next_ref, xa_ref, g_ref, wg_ref, wu_ref, wd_ref, out_ref,
                buf_ref, obuf_ref, sem_in, sem_out, *, tm, n_tiles):
    i = pl.program_id(0)
    slot = i % 2

    def row_in(tok_smem, r, s):
        return pltpu.make_async_copy(xa_ref.at[pl.ds(tok_smem[0, 0, r], 1)], buf_ref.at[s, pl.ds(r, 1)],
                                     sem_in.at[s])

    def row_out(r, s):
        return pltpu.make_async_copy(obuf_ref.at[s, pl.ds(r, 1)], out_ref.at[pl.ds(tok_ref[0, 0, r], 1)],
                                     sem_out.at[s])

    def start_gather(tok_smem, s):
        def body(r, carry):
            row_in(tok_smem, r, s).start()
            return carry
        lax.fori_loop(0, tm, body, 0, unroll=8)

    def wait_scatter(tile, s):
        n = nval_ref[tile]

        def chunk(c, carry):
            pltpu.make_async_copy(obuf_ref.at[s, pl.ds(0, WAIT_CHUNK)], out_ref.at[pl.ds(0, WAIT_CHUNK)],
                                  sem_out.at[s]).wait()
            return carry

        def row(r, carry):
            row_out(0, s).wait()
            return carry
        lax.fori_loop(0, jnp.right_shift(n, WAIT_CHUNK.bit_length() - 1), chunk, 0)
        lax.fori_loop(0, n & (WAIT_CHUNK - 1), row, 0)

    @pl.when(i == 0)
    def _():
        start_gather(tok_ref, 0)

    @pl.when((i + 1 < n_tiles) & (nval_ref[jnp.minimum(i + 1, n_tiles - 1)] > 0))
    def _():
        start_gather(tok_next_ref, 1 - slot)

    @pl.when(i >= 2)
    def _():
        wait_scatter(i - 2, slot)

    @pl.when(nval_ref[i] > 0)
    def _():
        pltpu.make_async_copy(xa_ref.at[pl.ds(0, tm)], buf_ref.at[slot], sem_in.at[slot]).wait()
        x = buf_ref[slot, :, 0:D_MODEL]
        meta = buf_ref[slot, :, D_MODEL:ROW_WIDTH]
        inv = lax.rsqrt(jnp.mean(x * x, axis=-1, keepdims=True) + RMS_EPS)
        h = (x * inv * g_ref[...]).astype(BF16)
        hids = []
        for e in range(EXPERTS_PER_GROUP):
            a = jnp.dot(h, wg_ref[e], preferred_element_type=F32)
            up = jnp.dot(h, wu_ref[e], preferred_element_type=F32)
            hids.append(a * _sigmoid(a) * up * meta[:, e:e + 1])
        hid = jnp.concatenate(hids, axis=1).astype(BF16)
        obuf_ref[slot] = x + jnp.dot(hid, wd_ref[0], preferred_element_type=F32)

        n = nval_ref[i]
        n_main = n & ~(SCATTER_UNROLL - 1)

        def chunk(c, carry):
            for u in range(SCATTER_UNROLL):
                row_out(c * SCATTER_UNROLL + u, slot).start()
            return carry

        def row(r, carry):
            row_out(n_main + r, slot).start()
            return carry
        lax.fori_loop(0, jnp.right_shift(n, SCATTER_UNROLL.bit_length() - 1), chunk, 0)
        lax.fori_loop(0, n - n_main, row, 0)

    @pl.when(i == n_tiles - 1)
    def _():
        @pl.when(i >= 1)
        def _():
            wait_scatter(i - 1, 1 - slot)
        wait_scatter(i, slot)


def _moe(xa, g, tgrp, nval, tok, wg, wu, wd, tm):
    t = xa.shape[0]
    n_tiles = tok.shape[0]
    hid_w = EXPERTS_PER_GROUP * D_EXPERT
    smem_tile = lambda off: pl.BlockSpec(
        (1, 1, tm), lambda i, tg, nv: (jnp.minimum(i + off, n_tiles - 1), 0, 0), memory_space=pltpu.SMEM)
    grid_spec = pltpu.PrefetchScalarGridSpec(
        num_scalar_prefetch=2,
        grid=(n_tiles,),
        in_specs=[
            smem_tile(0), smem_tile(1),
            pl.BlockSpec(memory_space=pl.ANY),
            pl.BlockSpec((1, D_MODEL), lambda i, tg, nv: (0, 0)),
            pl.BlockSpec((EXPERTS_PER_GROUP, D_MODEL, D_EXPERT), lambda i, tg, nv: (tg[i], 0, 0)),
            pl.BlockSpec((EXPERTS_PER_GROUP, D_MODEL, D_EXPERT), lambda i, tg, nv: (tg[i], 0, 0)),
            pl.BlockSpec((1, hid_w, D_MODEL), lambda i, tg, nv: (tg[i], 0, 0)),
        ],
        out_specs=pl.BlockSpec(memory_space=pl.ANY),
        scratch_shapes=[pltpu.VMEM((2, tm, ROW_WIDTH), F32), pltpu.VMEM((2, tm, D_MODEL), F32),
                        pltpu.SemaphoreType.DMA((2,)), pltpu.SemaphoreType.DMA((2,))],
    )
    return pl.pallas_call(
        functools.partial(_moe_kernel, tm=tm, n_tiles=n_tiles),
        grid_spec=grid_spec,
        out_shape=jax.ShapeDtypeStruct((t, D_MODEL), F32),
        compiler_params=_cparams(("arbitrary",)),
        name="moe",
    )(tgrp, nval, tok, tok, xa, g, wg, wu, wd)


MOE_TM = 512
WAIT_CHUNK = 64
SCATTER_UNROLL = 8


def _moe_plan(gid, tm):
    t = gid.shape[0]
    n_tiles = t // tm + N_EXPERT_GROUPS
    order = jnp.argsort(gid, stable=True).astype(jnp.int32)
    counts = jnp.sum((gid[:, None] == jnp.arange(N_EXPERT_GROUPS)[None, :]).astype(jnp.int32), axis=0)
    tiles_per = (counts + tm - 1) // tm
    tile_end = jnp.cumsum(tiles_per)
    row_start = jnp.cumsum(counts) - counts
    tile = jnp.arange(n_tiles, dtype=jnp.int32)
    tgrp = jnp.minimum(jnp.sum((tile[:, None] >= tile_end[None, :]).astype(jnp.int32), axis=1),
                       N_EXPERT_GROUPS - 1)
    local = tile - (tile_end - tiles_per)[tgrp]
    nval = jnp.clip(counts[tgrp] - local * tm, 0, tm)
    slot = jnp.arange(tm, dtype=jnp.int32)[None, :]
    src = row_start[tgrp][:, None] + local[:, None] * tm + jnp.where(slot < nval[:, None], slot, 0)
    tok = order[jnp.clip(src, 0, t - 1)]
    return tgrp.astype(jnp.int32), nval.astype(jnp.int32), tok.reshape(n_tiles, 1, tm)


def _rope_lane_freq():
    inv_freq = ROPE_THETA ** (-jnp.arange(0, ROPE_DIM, 2, dtype=F32) / ROPE_DIM)
    d = np.arange(LANES) % HEAD_DIM
    f = jnp.where(jnp.asarray(d < ROPE_DIM), inv_freq[np.asarray(d % (ROPE_DIM // 2))], 0.0)
    return f.reshape(1, LANES).astype(F32)


def _head_block_ones(width):
    idx = np.arange(width) // HEAD_DIM
    return jnp.asarray(idx[:, None] == idx[None, :], dtype=BF16)


def _overlap_t(n_cmp_pad, n_blk):
    a0 = np.arange(n_cmp_pad)[:, None] * CMP_STRIDE
    b0 = np.arange(n_blk)[None, :] * SEL_BLOCK
    ov = np.clip(np.minimum(a0 + CMP_BLOCK, b0 + SEL_BLOCK) - np.maximum(a0, b0), 0, None) / CMP_BLOCK
    return jnp.asarray(ov.T, dtype=BF16)


def _bias_place(n_blk):
    return jnp.asarray(np.arange(n_blk)[:, None] + BLK_LANE0 == np.arange(LANES)[None, :], dtype=BF16)


def _layer(x, positions, attn_norm_g, w_in, q_norm_g, k_norm_cmp_g, k_norm_slc_g, k_norm_swa_g,
           cmp_pos_emb_k, cmp_w1_k, cmp_w2_k, cmp_pos_emb_v, cmp_w1_v, cmp_w2_v, w_head_out, w_pool,
           pool_scale, w_out, ffn_norm_g, w_router_group, b_router_group, w_router_expert,
           b_router_expert, w_expert_gate, w_expert_up, w_expert_down):
    b, s, d = x.shape
    t = b * s
    x2 = x.reshape(t, d)

    o_q, o_kv, o_g = ATTN_WIDTH, ATTN_WIDTH + 6 * KV_WIDTH, ATTN_WIDTH + 6 * KV_WIDTH + N_HEADS * N_BRANCH
    o_p = o_g + POOL_WIDTH
    used = 2 * ATTN_WIDTH + 2 * D_MODEL + 6 * KV_WIDTH + N_HEADS * N_BRANCH
    w_packed = jnp.concatenate(
        [w_in[:, :o_q], w_in[:, o_g:o_p], w_in[:, o_p:], w_in[:, o_q:o_kv], w_in[:, o_kv:o_g],
         jnp.zeros((d, PROJ_WIDTH - used), w_in.dtype)], axis=1).astype(BF16)
    proj = _proj(x2, attn_norm_g.reshape(1, d), w_packed)

    f_lane = _rope_lane_freq()
    tile_h = lambda g_, n: jnp.tile(g_, n).reshape(1, n * HEAD_DIM)
    q, ks, vs, kw, vw, kc_raw, vc_raw, gates = _prep(
        proj, positions.reshape(t, 1), f_lane, tile_h(q_norm_g, N_HEADS), tile_h(k_norm_slc_g, N_KV),
        tile_h(k_norm_swa_g, N_KV), _head_block_ones(ATTN_WIDTH), _head_block_ones(KV_WIDTH), b, s)

    n_row = s // CMP_STRIDE
    half = CMP_STRIDE * HEAD_DIM
    xk = kc_raw.reshape(b, N_KV, n_row, half)
    xv = vc_raw.reshape(b, N_KV, n_row, half)
    pad2 = lambda w2: jnp.pad(w2, ((0, 0), (0, LANES - HEAD_DIM))).astype(BF16)
    cmp_end = np.minimum(np.arange(n_row) * CMP_STRIDE + CMP_BLOCK - 1, s - 1)
    posc = positions[:, cmp_end].reshape(b, n_row, 1)
    gk = jnp.pad(k_norm_cmp_g, (0, LANES - HEAD_DIM)).reshape(1, LANES)
    kc, vc = _compress(xk, xv, cmp_pos_emb_k.reshape(2, half), cmp_pos_emb_v.reshape(2, half),
                       cmp_w1_k.astype(BF16), pad2(cmp_w2_k), cmp_w1_v.astype(BF16), pad2(cmp_w2_v),
                       gk, posc, f_lane)

    n_blk = s // SEL_BLOCK
    assert n_blk <= PAD_LANE - BLK_LANE0
    front = jnp.zeros((b, N_KV, WINDOW, LANES), BF16).at[..., PAD_LANE].set(1.0)
    kw = jnp.concatenate([front, kw], axis=2)
    vw = jnp.concatenate([jnp.zeros((b, N_KV, HEAD_DIM + V_ONES, WINDOW), BF16), vw], axis=3)
    o = _attn(q, kc, vc, ks, vs, kw, vw, gates, _overlap_t(n_row, n_blk), _bias_place(n_blk))

    wr = jnp.concatenate([w_router_group, w_router_expert,
                          jnp.zeros((d, LANES - N_EXPERT_GROUPS - N_EXPERTS), F32)], axis=1)
    br = jnp.concatenate([b_router_group, b_router_expert,
                          jnp.zeros((LANES - N_EXPERT_GROUPS - N_EXPERTS,), F32)]).reshape(1, LANES)
    gf = ffn_norm_g.reshape(1, d)
    wr_hi = wr.astype(BF16)
    wr2 = jnp.stack([wr_hi, (wr - wr_hi.astype(F32)).astype(BF16)])
    xa = _mix(x2, proj, o, w_head_out.astype(BF16), w_pool.astype(BF16), pool_scale.reshape(1, d),
              w_out.astype(BF16), gf, wr2, br, s)

    tgrp, nval, tok = _moe_plan(xa[:, D_MODEL + META_GROUP_LANE].astype(jnp.int32), MOE_TM)
    wd = w_expert_down.reshape(N_EXPERT_GROUPS, EXPERTS_PER_GROUP * D_EXPERT, d)
    x_out = _moe(xa, gf, tgrp, nval, tok, w_expert_gate.astype(BF16), w_expert_up.astype(BF16),
                 wd.astype(BF16), MOE_TM)
    return x_out.reshape(b, s, d)


def kernel(x, positions, attn_norm_g, w_in, q_norm_g, k_norm_cmp_g, k_norm_slc_g, k_norm_swa_g,
           cmp_pos_emb_k, cmp_w1_k, cmp_w2_k, cmp_pos_emb_v, cmp_w1_v, cmp_w2_v, w_head_out, w_pool,
           pool_scale, w_out, ffn_norm_g, w_router_group, b_router_group, w_router_expert,
           b_router_expert, w_expert_gate, w_expert_up, w_expert_down):
    params = (attn_norm_g, w_in, q_norm_g, k_norm_cmp_g, k_norm_slc_g, k_norm_swa_g, cmp_pos_emb_k,
              cmp_w1_k, cmp_w2_k, cmp_pos_emb_v, cmp_w1_v, cmp_w2_v, w_head_out, w_pool, pool_scale,
              w_out, ffn_norm_g, w_router_group, b_router_group, w_router_expert, b_router_expert,
              w_expert_gate, w_expert_up, w_expert_down)
    for layer in range(attn_norm_g.shape[0]):
        x = _layer(x, positions, *[p[layer] for p in params])
    return x
```

```python
import functools

import jax
import jax.numpy as jnp
import numpy as np
from jax import lax
from jax.experimental import pallas as pl
from jax.experimental.pallas import tpu as pltpu

F32 = jnp.float32
BF16 = jnp.bfloat16

D_MODEL = 2048
N_HEADS = 16
HEAD_DIM = 64
N_KV = 4
HEADS_PER_KV = N_HEADS // N_KV
ATTN_WIDTH = N_HEADS * HEAD_DIM
KV_WIDTH = N_KV * HEAD_DIM
N_BRANCH = 3
CMP_BLOCK = 32
CMP_STRIDE = 16
CMP_HIDDEN = 4 * HEAD_DIM
SEL_BLOCK = 64
N_SEL = 16
N_FORCED_LOCAL = 2
WINDOW = 512
FORCE_BONUS = 1e4
NEG_INF = -1e30
ROPE_THETA = 500000.0
ROPE_DIM = HEAD_DIM // 4
HEAD_OUT_WIDTH = D_MODEL // N_HEADS
POOL_WINDOWS = (2, 4, 8, 16)
POOL_GROUPS = 4
POOL_WIDTH = 1024
POOL_GROUP_WIDTH = POOL_WIDTH // POOL_GROUPS
POOL_OUT_WIDTH = D_MODEL // POOL_GROUPS
N_EXPERT_GROUPS = 4
EXPERTS_PER_GROUP = 4
N_EXPERTS = N_EXPERT_GROUPS * EXPERTS_PER_GROUP
D_EXPERT = 256
RMS_EPS = 1e-6

LANES = 128
Q_SCALE = HEAD_DIM ** -0.5 * float(np.log2(np.e))
BLK_LANE0 = HEAD_DIM
PAD_LANE = HEAD_DIM + 32
V_ONES = 16
PROJ_WIDTH = 8192
PROJ_TN = 1024
COL_Q, COL_POOL, COL_M0, COL_M1, COL_KVA, COL_KVB = 0, 1, 2, 4, 6, 7
GATE_OFF = 2 * KV_WIDTH
VMEM_LIMIT = 56 * 1024 * 1024


def _cparams(sem):
    return pltpu.CompilerParams(dimension_semantics=sem, vmem_limit_bytes=VMEM_LIMIT)


def _nt_dot(a, b):
    return lax.dot_general(a, b, (((1,), (1,)), ((), ())), preferred_element_type=F32)


def _tn_dot(a, b):
    return lax.dot_general(a, b, (((0,), (0,)), ((), ())), preferred_element_type=F32)


def _split_dot(x, e):
    hi = x.astype(BF16)
    lo = (x - hi.astype(F32)).astype(BF16)
    return (jnp.dot(hi, e, preferred_element_type=F32)
            + jnp.dot(lo, e, preferred_element_type=F32))


def _proj_kernel(x_ref, g_ref, w_ref, o_ref, h_ref):
    @pl.when(pl.program_id(1) == 0)
    def _():
        x = x_ref[...]
        inv = lax.rsqrt(jnp.mean(x * x, axis=-1, keepdims=True) + RMS_EPS)
        h_ref[...] = (x * inv * g_ref[...]).astype(BF16)

    o_ref[...] = jnp.dot(h_ref[...], w_ref[...], preferred_element_type=F32)


def _proj(x2, g, w_packed, tm=1024):
    t = x2.shape[0]
    return pl.pallas_call(
        _proj_kernel,
        grid=(t // tm, PROJ_WIDTH // PROJ_TN),
        in_specs=[
            pl.BlockSpec((tm, D_MODEL), lambda i, j: (i, 0)),
            pl.BlockSpec((1, D_MODEL), lambda i, j: (0, 0)),
            pl.BlockSpec((D_MODEL, PROJ_TN), lambda i, j: (0, j)),
        ],
        out_specs=pl.BlockSpec((tm, PROJ_TN), lambda i, j: (i, j)),
        out_shape=jax.ShapeDtypeStruct((t, PROJ_WIDTH), F32),
        scratch_shapes=[pltpu.VMEM((tm, D_MODEL), BF16)],
        compiler_params=_cparams(("parallel", "arbitrary")),
        name="proj",
    )(x2, g, w_packed)


def _rope(xn, cos, sin):
    w = xn.shape[-1]
    lane = lax.broadcasted_iota(jnp.int32, xn.shape, 1) & (HEAD_DIM - 1)
    up = pltpu.roll(xn, w - ROPE_DIM // 2, axis=1)
    dn = pltpu.roll(xn, ROPE_DIM // 2, axis=1)
    return jnp.where(lane < ROPE_DIM // 2, xn * cos - up * sin,
                     jnp.where(lane < ROPE_DIM, xn * cos + dn * sin, xn))


def _head_norm_rope(x, gain, head_of_lane, cos, sin):
    w = x.shape[-1]
    per_head = _split_dot(x * x, head_of_lane)
    hi = per_head.astype(BF16)
    lo = (per_head - hi.astype(F32)).astype(BF16)
    ssq = _nt_dot(hi, head_of_lane) + _nt_dot(lo, head_of_lane)
    xn = x * lax.rsqrt(ssq * (1.0 / HEAD_DIM) + RMS_EPS) * gain
    reps = w // LANES
    return _rope(xn, jnp.concatenate([cos] * reps, axis=1), jnp.concatenate([sin] * reps, axis=1))


def _widen_head(x, h, upper):
    col = x[:, (h // 2) * LANES:(h // 2 + 1) * LANES]
    if h % 2:
        col = pltpu.roll(col, HEAD_DIM, axis=1)
    lane = lax.broadcasted_iota(jnp.int32, col.shape, 1)
    return jnp.where(lane < HEAD_DIM, col, upper)


def _prep_kernel(q_ref, kva_ref, kvb_ref, pos_ref, f_ref, gq_ref, gs_ref, gw_ref, eq_ref, ek_ref,
                 qo_ref, kso_ref, vso_ref, kwo_ref, vwo_ref, kco_ref, vco_ref, go_ref):
    tm = q_ref.shape[0]
    ang = pos_ref[...].astype(F32) * f_ref[...]
    cos, sin = jnp.cos(ang), jnp.sin(ang)
    q = _head_norm_rope(q_ref[...], gq_ref[...], eq_ref[...], cos, sin) * Q_SCALE
    lane = lax.broadcasted_iota(jnp.int32, (tm, LANES), 1)
    t_in = pl.program_id(1) * tm + lax.broadcasted_iota(jnp.int32, (tm, LANES), 0)
    q_upper = jnp.where(lane == PAD_LANE, NEG_INF, 0.0)
    blk_onehot = (jnp.right_shift(t_in, SEL_BLOCK.bit_length() - 1) == lane - BLK_LANE0).astype(F32)
    for h in range(N_HEADS):
        qo_ref[0, h // HEADS_PER_KV, h % HEADS_PER_KV] = _widen_head(q, h, q_upper).astype(BF16)
    kva = kva_ref[...]
    kvb = kvb_ref[...]
    ks = _head_norm_rope(kva[:, 2 * KV_WIDTH:3 * KV_WIDTH], gs_ref[...], ek_ref[...], cos, sin)
    kw = _head_norm_rope(kvb[:, 0:KV_WIDTH], gw_ref[...], ek_ref[...], cos, sin)
    vs = kva[:, 3 * KV_WIDTH:4 * KV_WIDTH]
    vw = kvb[:, KV_WIDTH:2 * KV_WIDTH]
    kc = kva[:, 0:KV_WIDTH]
    vc = kva[:, KV_WIDTH:2 * KV_WIDTH]
    for g in range(N_KV):
        sl = slice(g * HEAD_DIM, (g + 1) * HEAD_DIM)
        kso_ref[0, g] = _widen_head(ks, g, blk_onehot).astype(BF16)
        kwo_ref[0, g] = _widen_head(kw, g, 0.0).astype(BF16)
        kco_ref[0, g] = kc[:, sl]
        vco_ref[0, g] = vc[:, sl]
    ones = jnp.ones((V_ONES, tm), BF16)
    for v, vo_ref in ((vs, vso_ref), (vw, vwo_ref)):
        vt = v.T.astype(BF16)
        for g in range(N_KV):
            vo_ref[0, g, 0:HEAD_DIM, :] = vt[g * HEAD_DIM:(g + 1) * HEAD_DIM]
            vo_ref[0, g, HEAD_DIM:HEAD_DIM + V_ONES, :] = ones
    go_ref[0] = jax.nn.sigmoid(kvb[:, GATE_OFF:GATE_OFF + LANES]).T


def _prep(proj, pos2, f_lane, gq, gs, gw, e_q, e_k, b, s, tm=512):
    nt = s // tm
    row = lambda bb, i: bb * nt + i
    kv_shape = jax.ShapeDtypeStruct((b, N_KV, s, HEAD_DIM), F32)
    kv_spec = pl.BlockSpec((1, N_KV, tm, HEAD_DIM), lambda bb, i: (bb, 0, i, 0))
    kx_shape = jax.ShapeDtypeStruct((b, N_KV, s, LANES), BF16)
    kx_spec = pl.BlockSpec((1, N_KV, tm, LANES), lambda bb, i: (bb, 0, i, 0))
    vt_shape = jax.ShapeDtypeStruct((b, N_KV, HEAD_DIM + V_ONES, s), BF16)
    vt_spec = pl.BlockSpec((1, N_KV, HEAD_DIM + V_ONES, tm), lambda bb, i: (bb, 0, 0, i))
    const = lambda shape: pl.BlockSpec(shape, lambda bb, i: (0,) * len(shape))
    return pl.pallas_call(
        _prep_kernel,
        grid=(b, nt),
        in_specs=[
            pl.BlockSpec((tm, PROJ_TN), lambda bb, i: (row(bb, i), COL_Q)),
            pl.BlockSpec((tm, PROJ_TN), lambda bb, i: (row(bb, i), COL_KVA)),
            pl.BlockSpec((tm, PROJ_TN), lambda bb, i: (row(bb, i), COL_KVB)),
            pl.BlockSpec((tm, 1), lambda bb, i: (row(bb, i), 0)),
            const((1, LANES)), const((1, ATTN_WIDTH)), const((1, KV_WIDTH)), const((1, KV_WIDTH)),
            const((ATTN_WIDTH, LANES)), const((KV_WIDTH, LANES)),
        ],
        out_specs=[
            pl.BlockSpec((1, N_KV, HEADS_PER_KV, tm, LANES), lambda bb, i: (bb, 0, 0, i, 0)),
            kx_spec, vt_spec, kx_spec, vt_spec, kv_spec, kv_spec,
            pl.BlockSpec((1, LANES, tm), lambda bb, i: (bb, 0, i)),
        ],
        out_shape=[
            jax.ShapeDtypeStruct((b, N_KV, HEADS_PER_KV, s, LANES), BF16),
            kx_shape, vt_shape, kx_shape, vt_shape, kv_shape, kv_shape,
            jax.ShapeDtypeStruct((b, LANES, s), F32),
        ],
        compiler_params=_cparams(("parallel", "parallel")),
        name="prep",
    )(proj, proj, proj, pos2, f_lane, gq, gs, gw, e_q, e_k)


def _compress_one(x, pe_a, pe_b, w1_ref, w2_ref):
    half = CMP_STRIDE * HEAD_DIM
    nxt = pltpu.roll(x, x.shape[0] - 1, axis=0)
    a = (x + pe_a).astype(BF16)
    b = (nxt + pe_b).astype(BF16)
    hid = (jnp.dot(a, w1_ref[0:half, :], preferred_element_type=F32)
           + jnp.dot(b, w1_ref[half:2 * half, :], preferred_element_type=F32))
    hid = hid * jax.nn.sigmoid(hid)
    return jnp.dot(hid.astype(BF16), w2_ref[...], preferred_element_type=F32)


def _compress_kernel(xk_ref, xv_ref, pek_ref, pev_ref, w1k_ref, w2k_ref, w1v_ref, w2v_ref,
                     gk_ref, posc_ref, f_ref, kc_ref, vc_ref):
    ang = posc_ref[0].astype(F32) * f_ref[...]
    cos, sin = jnp.cos(ang), jnp.sin(ang)
    for g in range(N_KV):
        k = _compress_one(xk_ref[0, g], pek_ref[0:1, :], pek_ref[1:2, :], w1k_ref, w2k_ref)
        ssq = jnp.sum(k * k, axis=-1, keepdims=True)
        kn = k * lax.rsqrt(ssq * (1.0 / HEAD_DIM) + RMS_EPS) * gk_ref[...]
        kn = _rope(kn, cos, sin)
        kc_ref[0, g] = kn.astype(BF16)
        v = _compress_one(xv_ref[0, g], pev_ref[0:1, :], pev_ref[1:2, :], w1v_ref, w2v_ref)
        vc_ref[0, g] = v[:, 0:HEAD_DIM].astype(BF16)


def _compress(xk, xv, pek, pev, w1k, w2k, w1v, w2v, gk, posc, f_lane):
    b = xk.shape[0]
    nrow = xk.shape[2]
    half = CMP_STRIDE * HEAD_DIM
    const = lambda shape: pl.BlockSpec(shape, lambda bb: (0,) * len(shape))
    x_spec = pl.BlockSpec((1, N_KV, nrow, half), lambda bb: (bb, 0, 0, 0))
    o_spec = pl.BlockSpec((1, N_KV, nrow, HEAD_DIM), lambda bb: (bb, 0, 0, 0))
    o_shape = jax.ShapeDtypeStruct((b, N_KV, nrow, HEAD_DIM), BF16)
    k_spec = pl.BlockSpec((1, N_KV, nrow, LANES), lambda bb: (bb, 0, 0, 0))
    k_shape = jax.ShapeDtypeStruct((b, N_KV, nrow, LANES), BF16)
    return pl.pallas_call(
        _compress_kernel,
        grid=(b,),
        in_specs=[x_spec, x_spec, const((2, half)), const((2, half)),
                  const((2 * half, CMP_HIDDEN)), const((CMP_HIDDEN, LANES)),
                  const((2 * half, CMP_HIDDEN)), const((CMP_HIDDEN, LANES)),
                  const((1, LANES)), pl.BlockSpec((1, nrow, 1), lambda bb: (bb, 0, 0)), const((1, LANES))],
        out_specs=[k_spec, o_spec],
        out_shape=[k_shape, o_shape],
        compiler_params=_cparams(("parallel",)),
        name="compress",
    )(xk, xv, pek, pev, w1k, w2k, w1v, w2v, gk, posc, f_lane)


def _attn_kernel(q_ref, kc_ref, vc_ref, ks_ref, vs_ref, kw_ref, vw_ref, gate_ref, ovl_ref, place_ref,
                 o_ref, *, tq, seq):
    r = HEADS_PER_KV
    cols = r * tq
    n_cmp = kc_ref.shape[2]
    n_blk = seq // SEL_BLOCK
    i = pl.program_id(1)
    t0 = i * tq
    gates = gate_ref[0]

    kk = lax.broadcasted_iota(jnp.int32, (tq, cols), 0)
    tt = lax.broadcasted_iota(jnp.int32, (tq, cols), 1) & (tq - 1)
    causal_bias = jnp.where(kk <= tt, 0.0, NEG_INF)
    band_bias = jnp.where(kk > tt, 0.0, NEG_INF)

    q_plain, q_sel, o_cmps = [], [], []

    for g in range(N_KV):
        q3 = q_ref[0, g]
        qg = q3.reshape(cols, LANES)

        s = _nt_dot(kc_ref[0, g], qg)
        n_idx = lax.broadcasted_iota(jnp.int32, (n_cmp, cols), 0)
        t_col = t0 + (lax.broadcasted_iota(jnp.int32, (n_cmp, cols), 1) & (tq - 1))
        mask = (n_idx * CMP_STRIDE + (CMP_BLOCK - 1)) <= t_col
        s = jnp.where(mask, s, NEG_INF)
        p = jnp.where(mask, jnp.exp2(s - jnp.max(s, axis=0, keepdims=True)), 0.0)
        l = jnp.sum(p, axis=0, keepdims=True)
        p = p * (1.0 / jnp.where(l > 0.0, l, 1.0))
        o_cmp = _tn_dot(vc_ref[0, g], p.astype(BF16))

        p_sum = p[:, 0:tq] + p[:, tq:2 * tq] + p[:, 2 * tq:3 * tq] + p[:, 3 * tq:4 * tq]
        hi = p_sum.astype(BF16)
        lo = (p_sum - hi.astype(F32)).astype(BF16)
        imp = (jnp.dot(ovl_ref[...], hi, preferred_element_type=F32)
               + jnp.dot(ovl_ref[...], lo, preferred_element_type=F32))
        j_idx = lax.broadcasted_iota(jnp.int32, (n_blk, tq), 0)
        t_idx = t0 + lax.broadcasted_iota(jnp.int32, (n_blk, tq), 1)
        dist = jnp.right_shift(t_idx, SEL_BLOCK.bit_length() - 1) - j_idx
        valid = dist >= 0
        forced = (j_idx == 0) | (valid & (dist < N_FORCED_LOCAL))
        score = jnp.where(valid, imp + FORCE_BONUS * forced.astype(F32), NEG_INF)
        cnt = jnp.zeros((n_blk, tq), F32)
        for jp in range(n_blk):
            row = score[jp:jp + 1, :]
            beats = (row > score) | ((row == score) & (j_idx > jp))
            cnt = cnt + beats.astype(F32)
        selected = (cnt < float(min(N_SEL, n_blk))) & valid
        blk_bias = jnp.where(selected, 0.0, NEG_INF).astype(BF16)
        q_bias = _tn_dot(blk_bias, place_ref[...]).astype(BF16)
        q_plain.append(qg)
        q_sel.append((q3 + q_bias[None]).reshape(cols, LANES))
        o_cmps.append(o_cmp)

    def sel_tile(g, k0, carry, bias):
        m_i, acc = carry
        s = _nt_dot(ks_ref[0, g, pl.ds(k0, tq), :], q_sel[g])
        if bias is not None:
            s = s + bias
        m_new = jnp.maximum(m_i, jnp.max(s, axis=0, keepdims=True))
        p = jnp.exp2((s - m_new).astype(BF16))
        acc = jnp.exp2(m_i - m_new) * acc + jnp.dot(vs_ref[0, g, :, pl.ds(k0, tq)], p,
                                                    preferred_element_type=F32)
        return m_new, acc

    def sel_body(kt, carries):
        k0 = pl.multiple_of(kt * tq, tq)
        return tuple(sel_tile(g, k0, carries[g], None) for g in range(N_KV))

    init = (jnp.full((1, cols), NEG_INF, F32), jnp.zeros((HEAD_DIM + V_ONES, cols), F32))
    carries = lax.fori_loop(0, i, sel_body, (init,) * N_KV)

    for g in range(N_KV):
        _, acc_s = sel_tile(g, pl.multiple_of(t0, tq), carries[g], causal_bias)
        o_slc = acc_s[0:HEAD_DIM] * (1.0 / acc_s[HEAD_DIM:HEAD_DIM + 1])

        span = WINDOW + tq
        w0 = pl.multiple_of(t0, tq)
        s = _nt_dot(kw_ref[0, g, pl.ds(w0, span), :], q_plain[g])
        parts = [s[0:tq] + band_bias] + ([s[tq:WINDOW]] if WINDOW > tq else []) + [s[WINDOW:span] + causal_bias]
        s = jnp.concatenate(parts, axis=0)
        p = jnp.exp2((s - jnp.max(s, axis=0, keepdims=True)).astype(BF16))
        acc_w = jnp.dot(vw_ref[0, g, :, pl.ds(w0, span)], p, preferred_element_type=F32)
        o_swa = acc_w[0:HEAD_DIM] * (1.0 / acc_w[HEAD_DIM:HEAD_DIM + 1])

        def gate_row(c):
            parts = [gates[(g * r + hh) * N_BRANCH + c:(g * r + hh) * N_BRANCH + c + 1, :] for hh in range(r)]
            return jnp.concatenate(parts, axis=1)

        o = gate_row(0) * o_cmps[g] + gate_row(1) * o_slc + gate_row(2) * o_swa
        for hh in range(r):
            o_ref[0, g * r + hh] = o[:, hh * tq:(hh + 1) * tq].astype(BF16)


def _attn(q, kc, vc, ks, vs, kw, vw, gates, ovl_t, place, tq=512):
    b, _, _, s, _ = q.shape
    n_cmp = kc.shape[2]
    nq = s // tq
    assert WINDOW % tq == 0 and tq % SEL_BLOCK == 0
    full = lambda shape: pl.BlockSpec((1,) + shape, lambda bb, i: (bb,) + (0,) * len(shape))
    const = lambda shape: pl.BlockSpec(shape, lambda bb, i: (0,) * len(shape))
    return pl.pallas_call(
        functools.partial(_attn_kernel, tq=tq, seq=s),
        grid=(b, nq),
        in_specs=[
            pl.BlockSpec((1, N_KV, HEADS_PER_KV, tq, LANES), lambda bb, i: (bb, 0, 0, i, 0)),
            full((N_KV, n_cmp, LANES)), full((N_KV, n_cmp, HEAD_DIM)),
            full((N_KV, s, LANES)), full((N_KV, HEAD_DIM + V_ONES, s)),
            full((N_KV, s + WINDOW, LANES)), full((N_KV, HEAD_DIM + V_ONES, s + WINDOW)),
            pl.BlockSpec((1, LANES, tq), lambda bb, i: (bb, 0, i)),
            const(ovl_t.shape), const(place.shape),
        ],
        out_specs=pl.BlockSpec((1, N_HEADS, HEAD_DIM, tq), lambda bb, i: (bb, 0, 0, i)),
        out_shape=jax.ShapeDtypeStruct((b, N_HEADS, HEAD_DIM, s), BF16),
        compiler_params=_cparams(("parallel", "arbitrary")),
        name="attn",
    )(q, kc, vc, ks, vs, kw, vw, gates, ovl_t, place)


POOL_HALO = 16


def _sigmoid(x):
    return 0.5 * jnp.tanh(0.5 * x) + 0.5


def _mix_kernel(x_ref, m0_ref, m1_ref, u_ref, halo_ref, o_ref, wh_ref, wp_ref, ps_ref, wo_ref,
                gf_ref, wr_ref, br_ref, y_ref, *, tm, seq):
    it = pl.program_id(0) % (seq // tm)
    u = u_ref[...]
    halo = jnp.where(it == 0, 0.0, halo_ref[...])
    ext = jnp.concatenate([halo, u], axis=0)
    sums = []
    acc = ext
    for sh in (1, 2, 4, 8):
        acc = acc + pltpu.roll(acc, sh, axis=0)
        sums.append(acc)
    t_in = it * tm + lax.broadcasted_iota(jnp.int32, (tm, 1), 0)
    pool_parts = []
    for gi, win in enumerate(POOL_WINDOWS):
        sl = slice(gi * POOL_GROUP_WIDTH, (gi + 1) * POOL_GROUP_WIDTH)
        cnt = jnp.minimum(t_in + 1, win).astype(F32)
        pooled = sums[gi][POOL_HALO:, sl] / cnt - u[:, sl]
        pool_parts.append(jnp.dot(pooled.astype(BF16), wp_ref[gi], preferred_element_type=F32))
    pool_branch = jnp.concatenate(pool_parts, axis=1) * ps_ref[...]
    attn_branch = jnp.concatenate(
        [_tn_dot(o_ref[0, h], wh_ref[h]) for h in range(N_HEADS)], axis=1)
    y = _sigmoid(m0_ref[...]) * attn_branch + _sigmoid(m1_ref[...]) * pool_branch
    x1 = x_ref[...] + jnp.dot(y.astype(BF16), wo_ref[...], preferred_element_type=F32)
    y_ref[:, 0:D_MODEL] = x1
    inv = lax.rsqrt(jnp.mean(x1 * x1, axis=-1, keepdims=True) + RMS_EPS)
    ht = x1 * inv * gf_ref[...]
    h_hi = ht.astype(BF16)
    h_lo = (ht - h_hi.astype(F32)).astype(BF16)
    logits = (jnp.dot(h_hi, wr_ref[0], preferred_element_type=F32)
              + jnp.dot(h_lo, wr_ref[0], preferred_element_type=F32)
              + jnp.dot(h_hi, wr_ref[1], preferred_element_type=F32)) + br_ref[...]
    y_ref[:, D_MODEL:D_MODEL + LANES] = _route(logits)


def _mix(x2, proj, o, wh, wp, ps, wo, gf, wr, br, s, tm=256):
    t = x2.shape[0]
    nt = s // tm
    hb = tm // POOL_HALO
    const = lambda shape: pl.BlockSpec(shape, lambda i: (0,) * len(shape))
    return pl.pallas_call(
        functools.partial(_mix_kernel, tm=tm, seq=s),
        grid=(t // tm,),
        in_specs=[
            pl.BlockSpec((tm, D_MODEL), lambda i: (i, 0)),
            pl.BlockSpec((tm, D_MODEL), lambda i: (i, COL_M0 // 2)),
            pl.BlockSpec((tm, D_MODEL), lambda i: (i, COL_M1 // 2)),
            pl.BlockSpec((tm, POOL_WIDTH), lambda i: (i, COL_POOL)),
            pl.BlockSpec((POOL_HALO, POOL_WIDTH), lambda i: (jnp.maximum(i * hb - 1, 0), COL_POOL)),
            pl.BlockSpec((1, N_HEADS, HEAD_DIM, tm), lambda i: (i // nt, 0, 0, i % nt)),
            const(wh.shape), const(wp.shape), const(ps.shape), const(wo.shape),
            const(gf.shape), const(wr.shape), const(br.shape),
        ],
        out_specs=pl.BlockSpec((tm, ROW_WIDTH), lambda i: (i, 0)),
        out_shape=jax.ShapeDtypeStruct((t, ROW_WIDTH), F32),
        compiler_params=_cparams(("parallel",)),
        name="mix",
    )(x2, proj, proj, proj, proj, o, wh, wp, ps, wo, gf, wr, br)


ROUTE_EXPERT_OFF = N_EXPERT_GROUPS
ROW_WIDTH = D_MODEL + LANES
META_GROUP_LANE = EXPERTS_PER_GROUP


def _route(logits):
    lane = lax.broadcasted_iota(jnp.int32, logits.shape, 1)
    is_g = lane < N_EXPERT_GROUPS
    gmax = jnp.max(jnp.where(is_g, logits, -jnp.inf), axis=-1, keepdims=True)
    ge = jnp.where(is_g, jnp.exp(logits - gmax), 0.0)
    gprob = ge / jnp.sum(ge, axis=-1, keepdims=True)
    p_group = jnp.max(jnp.where(is_g, gprob, -jnp.inf), axis=-1, keepdims=True)
    g_sel = jnp.min(jnp.where(is_g & (gprob == p_group), lane, LANES), axis=-1, keepdims=True)
    e_lane = lane - ROUTE_EXPERT_OFF
    in_grp = ((e_lane >= 0) & (e_lane < N_EXPERTS)
              & (jnp.right_shift(e_lane, EXPERTS_PER_GROUP.bit_length() - 1) == g_sel))
    el = jnp.where(in_grp, logits, -jnp.inf)
    v1 = jnp.max(el, axis=-1, keepdims=True)
    i1 = jnp.min(jnp.where(in_grp & (el == v1), lane, LANES), axis=-1, keepdims=True)
    el2 = jnp.where(lane == i1, -jnp.inf, el)
    v2 = jnp.max(el2, axis=-1, keepdims=True)
    i2 = jnp.min(jnp.where(in_grp & (lane != i1) & (el2 == v2), lane, LANES), axis=-1, keepdims=True)
    e2 = jnp.exp(v2 - v1)
    den = 1.0 + e2
    first = ROUTE_EXPERT_OFF + EXPERTS_PER_GROUP * g_sel
    return (jnp.where(lane == i1 - first, (1.0 / den) * p_group, 0.0)
            + jnp.where(lane == i2 - first, (e2 / den) * p_group, 0.0)
            + jnp.where(lane == META_GROUP_LANE, g_sel.astype(F32), 0.0))


def _moe_kernel(tgrp_ref, nval_ref, tok_ref, tok_next_ref, xa_ref, g_ref, wg_ref, wu_ref, wd_ref, out_ref,
                buf_ref, obuf_ref, sem_in, sem_out, *, tm, n_tiles):
    i = pl.program_id(0)
    slot = i % 2

    def row_in(tok_smem, r, s):
        return pltpu.make_async_copy(xa_ref.at[pl.ds(tok_smem[0, 0, r], 1)], buf_ref.at[s, pl.ds(r, 1)],
                                     sem_in.at[s])

    def row_out(r, s):
        return pltpu.make_async_copy(obuf_ref.at[s, pl.ds(r, 1)], out_ref.at[pl.ds(tok_ref[0, 0, r], 1)],
                                     sem_out.at[s])

    def start_gather(tok_smem, s):
        def body(r, carry):
            row_in(tok_smem, r, s).start()
            return carry
        lax.fori_loop(0, tm, body, 0, unroll=8)

    def wait_scatter(tile, s):
        n = nval_ref[tile]

        def chunk(c, carry):
            pltpu.make_async_copy(obuf_ref.at[s, pl.ds(0, WAIT_CHUNK)], out_ref.at[pl.ds(0, WAIT_CHUNK)],
                                  sem_out.at[s]).wait()
            return carry

        def row(r, carry):
            row_out(0, s).wait()
            return carry
        lax.fori_loop(0, jnp.right_shift(n, WAIT_CHUNK.bit_length() - 1), chunk, 0)
        lax.fori_loop(0, n & (WAIT_CHUNK - 1), row, 0)

    @pl.when(i == 0)
    def _():
        start_gather(tok_ref, 0)

    @pl.when((i + 1 < n_tiles) & (nval_ref[jnp.minimum(i + 1, n_tiles - 1)] > 0))
    def _():
        start_gather(tok_next_ref, 1 - slot)

    @pl.when(i >= 2)
    def _():
        wait_scatter(i - 2, slot)

    @pl.when(nval_ref[i] > 0)
    def _():
        pltpu.make_async_copy(xa_ref.at[pl.ds(0, tm)], buf_ref.at[slot], sem_in.at[slot]).wait()
        x = buf_ref[slot, :, 0:D_MODEL]
        meta = buf_ref[slot, :, D_MODEL:ROW_WIDTH]
        inv = lax.rsqrt(jnp.mean(x * x, axis=-1, keepdims=True) + RMS_EPS)
        h = (x * inv * g_ref[...]).astype(BF16)
        hids = []
        for e in range(EXPERTS_PER_GROUP):
            a = jnp.dot(h, wg_ref[e], preferred_element_type=F32)
            up = jnp.dot(h, wu_ref[e], preferred_element_type=F32)
            hids.append(a * _sigmoid(a) * up * meta[:, e:e + 1])
        hid = jnp.concatenate(hids, axis=1).astype(BF16)
        obuf_ref[slot] = x + jnp.dot(hid, wd_ref[0], preferred_element_type=F32)

        n = nval_ref[i]
        n_main = n & ~(SCATTER_UNROLL - 1)

        def chunk(c, carry):
            for u in range(SCATTER_UNROLL):
                row_out(c * SCATTER_UNROLL + u, slot).start()
            return carry

        def row(r, carry):
            row_out(n_main + r, slot).start()
            return carry
        lax.fori_loop(0, jnp.right_shift(n, SCATTER_UNROLL.bit_length() - 1), chunk, 0)
        lax.fori_loop(0, n - n_main, row, 0)

    @pl.when(i == n_tiles - 1)
    def _():
        @pl.when(i >= 1)
        def _():
            wait_scatter(i - 1, 1 - slot)
        wait_scatter(i, slot)


def _moe(xa, g, tgrp, nval, tok, wg, wu, wd, tm):
    t = xa.shape[0]
    n_tiles = tok.shape[0]
    hid_w = EXPERTS_PER_GROUP * D_EXPERT
    smem_tile = lambda off: pl.BlockSpec(
        (1, 1, tm), lambda i, tg, nv: (jnp.minimum(i + off, n_tiles - 1), 0, 0), memory_space=pltpu.SMEM)
    grid_spec = pltpu.PrefetchScalarGridSpec(
        num_scalar_prefetch=2,
        grid=(n_tiles,),
        in_specs=[
            smem_tile(0), smem_tile(1),
            pl.BlockSpec(memory_space=pl.ANY),
            pl.BlockSpec((1, D_MODEL), lambda i, tg, nv: (0, 0)),
            pl.BlockSpec((EXPERTS_PER_GROUP, D_MODEL, D_EXPERT), lambda i, tg, nv: (tg[i], 0, 0)),
            pl.BlockSpec((EXPERTS_PER_GROUP, D_MODEL, D_EXPERT), lambda i, tg, nv: (tg[i], 0, 0)),
            pl.BlockSpec((1, hid_w, D_MODEL), lambda i, tg, nv: (tg[i], 0, 0)),
        ],
        out_specs=pl.BlockSpec(memory_space=pl.ANY),
        scratch_shapes=[pltpu.VMEM((2, tm, ROW_WIDTH), F32), pltpu.VMEM((2, tm, D_MODEL), F32),
                        pltpu.SemaphoreType.DMA((2,)), pltpu.SemaphoreType.DMA((2,))],
    )
    return pl.pallas_call(
        functools.partial(_moe_kernel, tm=tm, n_tiles=n_tiles),
        grid_spec=grid_spec,
        out_shape=jax.ShapeDtypeStruct((t, D_MODEL), F32),
        compiler_params=_cparams(("arbitrary",)),
        name="moe",
    )(tgrp, nval, tok, tok, xa, g, wg, wu, wd)


MOE_TM = 512
WAIT_CHUNK = 64
SCATTER_UNROLL = 8


def _moe_plan(gid, tm):
    t = gid.shape[0]
    n_tiles = t // tm + N_EXPERT_GROUPS
    order = jnp.argsort(gid, stable=True).astype(jnp.int32)
    counts = jnp.sum((gid[:, None] == jnp.arange(N_EXPERT_GROUPS)[None, :]).astype(jnp.int32), axis=0)
    tiles_per = (counts + tm - 1) // tm
    tile_end = jnp.cumsum(tiles_per)
    row_start = jnp.cumsum(counts) - counts
    tile = jnp.arange(n_tiles, dtype=jnp.int32)
    tgrp = jnp.minimum(jnp.sum((tile[:, None] >= tile_end[None, :]).astype(jnp.int32), axis=1),
                       N_EXPERT_GROUPS - 1)
    local = tile - (tile_end - tiles_per)[tgrp]
    nval = jnp.clip(counts[tgrp] - local * tm, 0, tm)
    slot = jnp.arange(tm, dtype=jnp.int32)[None, :]
    src = row_start[tgrp][:, None] + local[:, None] * tm + jnp.where(slot < nval[:, None], slot, 0)
    tok = order[jnp.clip(src, 0, t - 1)]
    return tgrp.astype(jnp.int32), nval.astype(jnp.int32), tok.reshape(n_tiles, 1, tm)


def _rope_lane_freq():
    inv_freq = ROPE_THETA ** (-jnp.arange(0, ROPE_DIM, 2, dtype=F32) / ROPE_DIM)
    d = np.arange(LANES) % HEAD_DIM
    f = jnp.where(jnp.asarray(d < ROPE_DIM), inv_freq[np.asarray(d % (ROPE_DIM // 2))], 0.0)
    return f.reshape(1, LANES).astype(F32)


def _head_of_lane(width):
    return jnp.asarray((np.arange(width) // HEAD_DIM)[:, None] == np.arange(LANES)[None, :], dtype=BF16)


def _overlap_t(n_cmp_pad, n_blk):
    a0 = np.arange(n_cmp_pad)[:, None] * CMP_STRIDE
    b0 = np.arange(n_blk)[None, :] * SEL_BLOCK
    ov = np.clip(np.minimum(a0 + CMP_BLOCK, b0 + SEL_BLOCK) - np.maximum(a0, b0), 0, None) / CMP_BLOCK
    return jnp.asarray(ov.T, dtype=BF16)


def _bias_place(n_blk):
    return jnp.asarray(np.arange(n_blk)[:, None] + BLK_LANE0 == np.arange(LANES)[None, :], dtype=BF16)


def _layer(x, positions, attn_norm_g, w_in, q_norm_g, k_norm_cmp_g, k_norm_slc_g, k_norm_swa_g,
           cmp_pos_emb_k, cmp_w1_k, cmp_w2_k, cmp_pos_emb_v, cmp_w1_v, cmp_w2_v, w_head_out, w_pool,
           pool_scale, w_out, ffn_norm_g, w_router_group, b_router_group, w_router_expert,
           b_router_expert, w_expert_gate, w_expert_up, w_expert_down):
    b, s, d = x.shape
    t = b * s
    x2 = x.reshape(t, d)

    o_q, o_kv, o_g = ATTN_WIDTH, ATTN_WIDTH + 6 * KV_WIDTH, ATTN_WIDTH + 6 * KV_WIDTH + N_HEADS * N_BRANCH
    o_p = o_g + POOL_WIDTH
    used = 2 * ATTN_WIDTH + 2 * D_MODEL + 6 * KV_WIDTH + N_HEADS * N_BRANCH
    w_packed = jnp.concatenate(
        [w_in[:, :o_q], w_in[:, o_g:o_p], w_in[:, o_p:], w_in[:, o_q:o_kv], w_in[:, o_kv:o_g],
         jnp.zeros((d, PROJ_WIDTH - used), w_in.dtype)], axis=1).astype(BF16)
    proj = _proj(x2, attn_norm_g.reshape(1, d), w_packed)

    f_lane = _rope_lane_freq()
    tile_h = lambda g_, n: jnp.tile(g_, n).reshape(1, n * HEAD_DIM)
    q, ks, vs, kw, vw, kc_raw, vc_raw, gates = _prep(
        proj, positions.reshape(t, 1), f_lane, tile_h(q_norm_g, N_HEADS), tile_h(k_norm_slc_g, N_KV),
        tile_h(k_norm_swa_g, N_KV), _head_of_lane(ATTN_WIDTH), _head_of_lane(KV_WIDTH), b, s)

    n_row = s // CMP_STRIDE
    half = CMP_STRIDE * HEAD_DIM
    xk = kc_raw.reshape(b, N_KV, n_row, half)
    xv = vc_raw.reshape(b, N_KV, n_row, half)
    pad2 = lambda w2: jnp.pad(w2, ((0, 0), (0, LANES - HEAD_DIM))).astype(BF16)
    cmp_end = np.minimum(np.arange(n_row) * CMP_STRIDE + CMP_BLOCK - 1, s - 1)
    posc = positions[:, cmp_end].reshape(b, n_row, 1)
    gk = jnp.pad(k_norm_cmp_g, (0, LANES - HEAD_DIM)).reshape(1, LANES)
    kc, vc = _compress(xk, xv, cmp_pos_emb_k.reshape(2, half), cmp_pos_emb_v.reshape(2, half),
                       cmp_w1_k.astype(BF16), pad2(cmp_w2_k), cmp_w1_v.astype(BF16), pad2(cmp_w2_v),
                       gk, posc, f_lane)

    n_blk = s // SEL_BLOCK
    assert n_blk <= PAD_LANE - BLK_LANE0
    front = jnp.zeros((b, N_KV, WINDOW, LANES), BF16).at[..., PAD_LANE].set(1.0)
    kw = jnp.concatenate([front, kw], axis=2)
    vw = jnp.concatenate([jnp.zeros((b, N_KV, HEAD_DIM + V_ONES, WINDOW), BF16), vw], axis=3)
    o = _attn(q, kc, vc, ks, vs, kw, vw, gates, _overlap_t(n_row, n_blk), _bias_place(n_blk))

    wr = jnp.concatenate([w_router_group, w_router_expert,
                          jnp.zeros((d, LANES - N_EXPERT_GROUPS - N_EXPERTS), F32)], axis=1)
    br = jnp.concatenate([b_router_group, b_router_expert,
                          jnp.zeros((LANES - N_EXPERT_GROUPS - N_EXPERTS,), F32)]).reshape(1, LANES)
    gf = ffn_norm_g.reshape(1, d)
    wr_hi = wr.astype(BF16)
    wr2 = jnp.stack([wr_hi, (wr - wr_hi.astype(F32)).astype(BF16)])
    xa = _mix(x2, proj, o, w_head_out.astype(BF16), w_pool.astype(BF16), pool_scale.reshape(1, d),
              w_out.astype(BF16), gf, wr2, br, s)

    tgrp, nval, tok = _moe_plan(xa[:, D_MODEL + META_GROUP_LANE].astype(jnp.int32), MOE_TM)
    wd = w_expert_down.reshape(N_EXPERT_GROUPS, EXPERTS_PER_GROUP * D_EXPERT, d)
    x_out = _moe(xa, gf, tgrp, nval, tok, w_expert_gate.astype(BF16), w_expert_up.astype(BF16),
                 wd.astype(BF16), MOE_TM)
    return x_out.reshape(b, s, d)


def kernel(x, positions, attn_norm_g, w_in, q_norm_g, k_norm_cmp_g, k_norm_slc_g, k_norm_swa_g,
           cmp_pos_emb_k, cmp_w1_k, cmp_w2_k, cmp_pos_emb_v, cmp_w1_v, cmp_w2_v, w_head_out, w_pool,
           pool_scale, w_out, ffn_norm_g, w_router_group, b_router_group, w_router_expert,
           b_router_expert, w_expert_gate, w_expert_up, w_expert_down):
    params = (attn_norm_g, w_in, q_norm_g, k_norm_cmp_g, k_norm_slc_g, k_norm_swa_g, cmp_pos_emb_k,
              cmp_w1_k, cmp_w2_k, cmp_pos_emb_v, cmp_w1_v, cmp_w2_v, w_head_out, w_pool, pool_scale,
              w_out, ffn_norm_g, w_router_group, b_router_group, w_router_expert, b_router_expert,
              w_expert_gate, w_expert_up, w_expert_down)
    for layer in range(attn_norm_g.shape[0]):
        x = _layer(x, positions, *[p[layer] for p in params])
    return x
```

```python
import functools

import jax
import jax.numpy as jnp
import numpy as np
from jax import lax
from jax.experimental import pallas as pl
from jax.experimental.pallas import tpu as pltpu

F32 = jnp.float32
BF16 = jnp.bfloat16

D_MODEL = 2048
N_HEADS = 16
HEAD_DIM = 64
N_KV = 4
HEADS_PER_KV = N_HEADS // N_KV
ATTN_WIDTH = N_HEADS * HEAD_DIM
KV_WIDTH = N_KV * HEAD_DIM
N_BRANCH = 3
CMP_BLOCK = 32
CMP_STRIDE = 16
CMP_HIDDEN = 4 * HEAD_DIM
SEL_BLOCK = 64
N_SEL = 16
N_FORCED_LOCAL = 2
WINDOW = 512
FORCE_BONUS = 1e4
NEG_INF = -1e30
ROPE_THETA = 500000.0
ROPE_DIM = HEAD_DIM // 4
HEAD_OUT_WIDTH = D_MODEL // N_HEADS
POOL_WINDOWS = (2, 4, 8, 16)
POOL_GROUPS = 4
POOL_WIDTH = 1024
POOL_GROUP_WIDTH = POOL_WIDTH // POOL_GROUPS
POOL_OUT_WIDTH = D_MODEL // POOL_GROUPS
N_EXPERT_GROUPS = 4
EXPERTS_PER_GROUP = 4
N_EXPERTS = N_EXPERT_GROUPS * EXPERTS_PER_GROUP
D_EXPERT = 256
RMS_EPS = 1e-6

LANES = 128
Q_SCALE = HEAD_DIM ** -0.5 * float(np.log2(np.e))
BLK_LANE0 = HEAD_DIM
V_ONES = 16
PROJ_WIDTH = 8192
PROJ_TN = 1024
COL_Q, COL_POOL, COL_M0, COL_M1, COL_KVA, COL_KVB = 0, 1, 2, 4, 6, 7
GATE_OFF = 2 * KV_WIDTH
VMEM_LIMIT = 56 * 1024 * 1024


def _cparams(sem):
    return pltpu.CompilerParams(dimension_semantics=sem, vmem_limit_bytes=VMEM_LIMIT)


def _nt_dot(a, b):
    return lax.dot_general(a, b, (((1,), (1,)), ((), ())), preferred_element_type=F32)


def _tn_dot(a, b):
    return lax.dot_general(a, b, (((0,), (0,)), ((), ())), preferred_element_type=F32)


def _split_dot(x, e):
    hi = x.astype(BF16)
    lo = (x - hi.astype(F32)).astype(BF16)
    return (jnp.dot(hi, e, preferred_element_type=F32)
            + jnp.dot(lo, e, preferred_element_type=F32))


def _proj_kernel(x_ref, g_ref, w_ref, o_ref, h_ref):
    @pl.when(pl.program_id(1) == 0)
    def _():
        x = x_ref[...]
        inv = lax.rsqrt(jnp.mean(x * x, axis=-1, keepdims=True) + RMS_EPS)
        h_ref[...] = (x * inv * g_ref[...]).astype(BF16)

    o_ref[...] = jnp.dot(h_ref[...], w_ref[...], preferred_element_type=F32)


def _proj(x2, g, w_packed, tm=1024):
    t = x2.shape[0]
    return pl.pallas_call(
        _proj_kernel,
        grid=(t // tm, PROJ_WIDTH // PROJ_TN),
        in_specs=[
            pl.BlockSpec((tm, D_MODEL), lambda i, j: (i, 0)),
            pl.BlockSpec((1, D_MODEL), lambda i, j: (0, 0)),
            pl.BlockSpec((D_MODEL, PROJ_TN), lambda i, j: (0, j)),
        ],
        out_specs=pl.BlockSpec((tm, PROJ_TN), lambda i, j: (i, j)),
        out_shape=jax.ShapeDtypeStruct((t, PROJ_WIDTH), F32),
        scratch_shapes=[pltpu.VMEM((tm, D_MODEL), BF16)],
        compiler_params=_cparams(("parallel", "arbitrary")),
        name="proj",
    )(x2, g, w_packed)


def _rope(xn, cos, sin):
    w = xn.shape[-1]
    lane = lax.broadcasted_iota(jnp.int32, xn.shape, 1) & (HEAD_DIM - 1)
    up = pltpu.roll(xn, w - ROPE_DIM // 2, axis=1)
    dn = pltpu.roll(xn, ROPE_DIM // 2, axis=1)
    return jnp.where(lane < ROPE_DIM // 2, xn * cos - up * sin,
                     jnp.where(lane < ROPE_DIM, xn * cos + dn * sin, xn))


def _head_norm_rope(x, gain, head_of_lane, cos, sin):
    w = x.shape[-1]
    per_head = _split_dot(x * x, head_of_lane)
    hi = per_head.astype(BF16)
    lo = (per_head - hi.astype(F32)).astype(BF16)
    ssq = _nt_dot(hi, head_of_lane) + _nt_dot(lo, head_of_lane)
    xn = x * lax.rsqrt(ssq * (1.0 / HEAD_DIM) + RMS_EPS) * gain
    reps = w // LANES
    return _rope(xn, jnp.concatenate([cos] * reps, axis=1), jnp.concatenate([sin] * reps, axis=1))


def _widen_head(x, h, upper):
    col = x[:, (h // 2) * LANES:(h // 2 + 1) * LANES]
    if h % 2:
        col = pltpu.roll(col, HEAD_DIM, axis=1)
    lane = lax.broadcasted_iota(jnp.int32, col.shape, 1)
    return jnp.where(lane < HEAD_DIM, col, upper)


def _prep_kernel(q_ref, kva_ref, kvb_ref, pos_ref, f_ref, gq_ref, gs_ref, gw_ref, eq_ref, ek_ref,
                 qo_ref, kso_ref, vso_ref, kwo_ref, vwo_ref, kco_ref, vco_ref, go_ref):
    tm = q_ref.shape[0]
    ang = pos_ref[...].astype(F32) * f_ref[...]
    cos, sin = jnp.cos(ang), jnp.sin(ang)
    q = _head_norm_rope(q_ref[...], gq_ref[...], eq_ref[...], cos, sin) * Q_SCALE
    lane = lax.broadcasted_iota(jnp.int32, (tm, LANES), 1)
    t_in = pl.program_id(1) * tm + lax.broadcasted_iota(jnp.int32, (tm, LANES), 0)
    blk_onehot = (jnp.right_shift(t_in, SEL_BLOCK.bit_length() - 1) == lane - BLK_LANE0).astype(F32)
    for h in range(N_HEADS):
        qo_ref[0, h // HEADS_PER_KV, h % HEADS_PER_KV] = _widen_head(q, h, 0.0).astype(BF16)
    kva = kva_ref[...]
    kvb = kvb_ref[...]
    ks = _head_norm_rope(kva[:, 2 * KV_WIDTH:3 * KV_WIDTH], gs_ref[...], ek_ref[...], cos, sin)
    kw = _head_norm_rope(kvb[:, 0:KV_WIDTH], gw_ref[...], ek_ref[...], cos, sin)
    vs = kva[:, 3 * KV_WIDTH:4 * KV_WIDTH]
    vw = kvb[:, KV_WIDTH:2 * KV_WIDTH]
    kc = kva[:, 0:KV_WIDTH]
    vc = kva[:, KV_WIDTH:2 * KV_WIDTH]
    for g in range(N_KV):
        sl = slice(g * HEAD_DIM, (g + 1) * HEAD_DIM)
        kso_ref[0, g] = _widen_head(ks, g, blk_onehot).astype(BF16)
        kwo_ref[0, g] = _widen_head(kw, g, 0.0).astype(BF16)
        kco_ref[0, g] = kc[:, sl]
        vco_ref[0, g] = vc[:, sl]
    ones = jnp.ones((V_ONES, tm), BF16)
    for v, vo_ref in ((vs, vso_ref), (vw, vwo_ref)):
        vt = v.T.astype(BF16)
        for g in range(N_KV):
            vo_ref[0, g, 0:HEAD_DIM, :] = vt[g * HEAD_DIM:(g + 1) * HEAD_DIM]
            vo_ref[0, g, HEAD_DIM:HEAD_DIM + V_ONES, :] = ones
    go_ref[0] = jax.nn.sigmoid(kvb[:, GATE_OFF:GATE_OFF + LANES]).T


def _prep(proj, pos2, f_lane, gq, gs, gw, e_q, e_k, b, s, tm=512):
    nt = s // tm
    row = lambda bb, i: bb * nt + i
    kv_shape = jax.ShapeDtypeStruct((b, N_KV, s, HEAD_DIM), F32)
    kv_spec = pl.BlockSpec((1, N_KV, tm, HEAD_DIM), lambda bb, i: (bb, 0, i, 0))
    kx_shape = jax.ShapeDtypeStruct((b, N_KV, s, LANES), BF16)
    kx_spec = pl.BlockSpec((1, N_KV, tm, LANES), lambda bb, i: (bb, 0, i, 0))
    vt_shape = jax.ShapeDtypeStruct((b, N_KV, HEAD_DIM + V_ONES, s), BF16)
    vt_spec = pl.BlockSpec((1, N_KV, HEAD_DIM + V_ONES, tm), lambda bb, i: (bb, 0, 0, i))
    const = lambda shape: pl.BlockSpec(shape, lambda bb, i: (0,) * len(shape))
    return pl.pallas_call(
        _prep_kernel,
        grid=(b, nt),
        in_specs=[
            pl.BlockSpec((tm, PROJ_TN), lambda bb, i: (row(bb, i), COL_Q)),
            pl.BlockSpec((tm, PROJ_TN), lambda bb, i: (row(bb, i), COL_KVA)),
            pl.BlockSpec((tm, PROJ_TN), lambda bb, i: (row(bb, i), COL_KVB)),
            pl.BlockSpec((tm, 1), lambda bb, i: (row(bb, i), 0)),
            const((1, LANES)), const((1, ATTN_WIDTH)), const((1, KV_WIDTH)), const((1, KV_WIDTH)),
            const((ATTN_WIDTH, LANES)), const((KV_WIDTH, LANES)),
        ],
        out_specs=[
            pl.BlockSpec((1, N_KV, HEADS_PER_KV, tm, LANES), lambda bb, i: (bb, 0, 0, i, 0)),
            kx_spec, vt_spec, kx_spec, vt_spec, kv_spec, kv_spec,
            pl.BlockSpec((1, LANES, tm), lambda bb, i: (bb, 0, i)),
        ],
        out_shape=[
            jax.ShapeDtypeStruct((b, N_KV, HEADS_PER_KV, s, LANES), BF16),
            kx_shape, vt_shape, kx_shape, vt_shape, kv_shape, kv_shape,
            jax.ShapeDtypeStruct((b, LANES, s), F32),
        ],
        compiler_params=_cparams(("parallel", "parallel")),
        name="prep",
    )(proj, proj, proj, pos2, f_lane, gq, gs, gw, e_q, e_k)


def _compress_one(x, pe_a, pe_b, w1_ref, w2_ref):
    half = CMP_STRIDE * HEAD_DIM
    nxt = pltpu.roll(x, x.shape[0] - 1, axis=0)
    a = (x + pe_a).astype(BF16)
    b = (nxt + pe_b).astype(BF16)
    hid = (jnp.dot(a, w1_ref[0:half, :], preferred_element_type=F32)
           + jnp.dot(b, w1_ref[half:2 * half, :], preferred_element_type=F32))
    hid = hid * jax.nn.sigmoid(hid)
    return jnp.dot(hid.astype(BF16), w2_ref[...], preferred_element_type=F32)


def _compress_kernel(xk_ref, xv_ref, pek_ref, pev_ref, w1k_ref, w2k_ref, w1v_ref, w2v_ref,
                     gk_ref, posc_ref, f_ref, kc_ref, vc_ref):
    ang = posc_ref[0].astype(F32) * f_ref[...]
    cos, sin = jnp.cos(ang), jnp.sin(ang)
    for g in range(N_KV):
        k = _compress_one(xk_ref[0, g], pek_ref[0:1, :], pek_ref[1:2, :], w1k_ref, w2k_ref)
        ssq = jnp.sum(k * k, axis=-1, keepdims=True)
        kn = k * lax.rsqrt(ssq * (1.0 / HEAD_DIM) + RMS_EPS) * gk_ref[...]
        kn = _rope(kn, cos, sin)
        kc_ref[0, g] = kn.astype(BF16)
        v = _compress_one(xv_ref[0, g], pev_ref[0:1, :], pev_ref[1:2, :], w1v_ref, w2v_ref)
        vc_ref[0, g] = v[:, 0:HEAD_DIM].astype(BF16)


def _compress(xk, xv, pek, pev, w1k, w2k, w1v, w2v, gk, posc, f_lane):
    b = xk.shape[0]
    nrow = xk.shape[2]
    half = CMP_STRIDE * HEAD_DIM
    const = lambda shape: pl.BlockSpec(shape, lambda bb: (0,) * len(shape))
    x_spec = pl.BlockSpec((1, N_KV, nrow, half), lambda bb: (bb, 0, 0, 0))
    o_spec = pl.BlockSpec((1, N_KV, nrow, HEAD_DIM), lambda bb: (bb, 0, 0, 0))
    o_shape = jax.ShapeDtypeStruct((b, N_KV, nrow, HEAD_DIM), BF16)
    k_spec = pl.BlockSpec((1, N_KV, nrow, LANES), lambda bb: (bb, 0, 0, 0))
    k_shape = jax.ShapeDtypeStruct((b, N_KV, nrow, LANES), BF16)
    return pl.pallas_call(
        _compress_kernel,
        grid=(b,),
        in_specs=[x_spec, x_spec, const((2, half)), const((2, half)),
                  const((2 * half, CMP_HIDDEN)), const((CMP_HIDDEN, LANES)),
                  const((2 * half, CMP_HIDDEN)), const((CMP_HIDDEN, LANES)),
                  const((1, LANES)), pl.BlockSpec((1, nrow, 1), lambda bb: (bb, 0, 0)), const((1, LANES))],
        out_specs=[k_spec, o_spec],
        out_shape=[k_shape, o_shape],
        compiler_params=_cparams(("parallel",)),
        name="compress",
    )(xk, xv, pek, pev, w1k, w2k, w1v, w2v, gk, posc, f_lane)


def _attn_kernel(q_ref, kc_ref, vc_ref, ks_ref, vs_ref, kw_ref, vw_ref, gate_ref, ovl_ref, place_ref,
                 o_ref, *, tq, seq):
    r = HEADS_PER_KV
    cols = r * tq
    n_cmp = kc_ref.shape[2]
    n_blk = seq // SEL_BLOCK
    i = pl.program_id(1)
    t0 = i * tq
    gates = gate_ref[0]

    kk = lax.broadcasted_iota(jnp.int32, (tq, cols), 0)
    tt = lax.broadcasted_iota(jnp.int32, (tq, cols), 1) & (tq - 1)
    causal_bias = jnp.where(kk <= tt, 0.0, NEG_INF)
    band_bias = jnp.where(kk > tt, 0.0, NEG_INF)

    q_plain, q_sel, o_cmps = [], [], []

    for g in range(N_KV):
        q3 = q_ref[0, g]
        qg = q3.reshape(cols, LANES)

        s = _nt_dot(kc_ref[0, g], qg)
        n_idx = lax.broadcasted_iota(jnp.int32, (n_cmp, cols), 0)
        t_col = t0 + (lax.broadcasted_iota(jnp.int32, (n_cmp, cols), 1) & (tq - 1))
        mask = (n_idx * CMP_STRIDE + (CMP_BLOCK - 1)) <= t_col
        s = jnp.where(mask, s, NEG_INF)
        p = jnp.where(mask, jnp.exp2(s - jnp.max(s, axis=0, keepdims=True)), 0.0)
        l = jnp.sum(p, axis=0, keepdims=True)
        p = p * (1.0 / jnp.where(l > 0.0, l, 1.0))
        o_cmp = _tn_dot(vc_ref[0, g], p.astype(BF16))

        p_sum = p[:, 0:tq] + p[:, tq:2 * tq] + p[:, 2 * tq:3 * tq] + p[:, 3 * tq:4 * tq]
        hi = p_sum.astype(BF16)
        lo = (p_sum - hi.astype(F32)).astype(BF16)
        imp = (jnp.dot(ovl_ref[...], hi, preferred_element_type=F32)
               + jnp.dot(ovl_ref[...], lo, preferred_element_type=F32))
        j_idx = lax.broadcasted_iota(jnp.int32, (n_blk, tq), 0)
        t_idx = t0 + lax.broadcasted_iota(jnp.int32, (n_blk, tq), 1)
        dist = jnp.right_shift(t_idx, SEL_BLOCK.bit_length() - 1) - j_idx
        valid = dist >= 0
        forced = (j_idx == 0) | (valid & (dist < N_FORCED_LOCAL))
        score = jnp.where(valid, imp + FORCE_BONUS * forced.astype(F32), NEG_INF)
        cnt = jnp.zeros((n_blk, tq), F32)
        for jp in range(n_blk):
            row = score[jp:jp + 1, :]
            beats = (row > score) | ((row == score) & (j_idx > jp))
            cnt = cnt + beats.astype(F32)
        selected = (cnt < float(min(N_SEL, n_blk))) & valid
        blk_bias = jnp.where(selected, 0.0, NEG_INF).astype(BF16)
        q_bias = _tn_dot(blk_bias, place_ref[...]).astype(BF16)
        q_plain.append(qg)
        q_sel.append((q3 + q_bias[None]).reshape(cols, LANES))
        o_cmps.append(o_cmp)

    def sel_tile(g, k0, carry, bias):
        m_i, acc = carry
        s = _nt_dot(ks_ref[0, g, pl.ds(k0, tq), :], q_sel[g])
        if bias is not None:
            s = s + bias
        m_new = jnp.maximum(m_i, jnp.max(s, axis=0, keepdims=True))
        p = jnp.exp2((s - m_new).astype(BF16))
        acc = jnp.exp2(m_i - m_new) * acc + jnp.dot(vs_ref[0, g, :, pl.ds(k0, tq)], p,
                                                    preferred_element_type=F32)
        return m_new, acc

    def sel_body(kt, carries):
        k0 = pl.multiple_of(kt * tq, tq)
        return tuple(sel_tile(g, k0, carries[g], None) for g in range(N_KV))

    init = (jnp.full((1, cols), NEG_INF, F32), jnp.zeros((HEAD_DIM + V_ONES, cols), F32))
    carries = lax.fori_loop(0, i, sel_body, (init,) * N_KV)

    for g in range(N_KV):
        _, acc_s = sel_tile(g, pl.multiple_of(t0, tq), carries[g], causal_bias)
        o_slc = acc_s[0:HEAD_DIM] * (1.0 / acc_s[HEAD_DIM:HEAD_DIM + 1])

        n_old = WINDOW // tq
        s_parts, v_parts = [], []
        for j in range(n_old + 1):
            start = t0 + (j - n_old) * tq
            k0 = pl.multiple_of(jnp.maximum(start, 0), tq)
            s = _nt_dot(kw_ref[0, g, pl.ds(k0, tq), :], q_plain[g])
            if j == 0:
                s = s + band_bias
            s = s + (causal_bias if j == n_old else jnp.where(start < 0, NEG_INF, 0.0))
            s_parts.append(s)
            v_parts.append(vw_ref[0, g, :, pl.ds(k0, tq)])
        m = functools.reduce(jnp.maximum, [jnp.max(s, axis=0, keepdims=True) for s in s_parts])
        acc_w = sum(jnp.dot(v, jnp.exp2((s - m).astype(BF16)), preferred_element_type=F32)
                    for s, v in zip(s_parts, v_parts))
        o_swa = acc_w[0:HEAD_DIM] * (1.0 / acc_w[HEAD_DIM:HEAD_DIM + 1])

        def gate_row(c):
            parts = [gates[(g * r + hh) * N_BRANCH + c:(g * r + hh) * N_BRANCH + c + 1, :] for hh in range(r)]
            return jnp.concatenate(parts, axis=1)

        o = gate_row(0) * o_cmps[g] + gate_row(1) * o_slc + gate_row(2) * o_swa
        for hh in range(r):
            o_ref[0, g * r + hh] = o[:, hh * tq:(hh + 1) * tq].astype(BF16)


def _attn(q, kc, vc, ks, vs, kw, vw, gates, ovl_t, place, tq=512):
    b, _, _, s, _ = q.shape
    n_cmp = kc.shape[2]
    nq = s // tq
    assert WINDOW % tq == 0 and tq % SEL_BLOCK == 0
    full = lambda shape: pl.BlockSpec((1,) + shape, lambda bb, i: (bb,) + (0,) * len(shape))
    const = lambda shape: pl.BlockSpec(shape, lambda bb, i: (0,) * len(shape))
    return pl.pallas_call(
        functools.partial(_attn_kernel, tq=tq, seq=s),
        grid=(b, nq),
        in_specs=[
            pl.BlockSpec((1, N_KV, HEADS_PER_KV, tq, LANES), lambda bb, i: (bb, 0, 0, i, 0)),
            full((N_KV, n_cmp, LANES)), full((N_KV, n_cmp, HEAD_DIM)),
            full((N_KV, s, LANES)), full((N_KV, HEAD_DIM + V_ONES, s)),
            full((N_KV, s, LANES)), full((N_KV, HEAD_DIM + V_ONES, s)),
            pl.BlockSpec((1, LANES, tq), lambda bb, i: (bb, 0, i)),
            const(ovl_t.shape), const(place.shape),
        ],
        out_specs=pl.BlockSpec((1, N_HEADS, HEAD_DIM, tq), lambda bb, i: (bb, 0, 0, i)),
        out_shape=jax.ShapeDtypeStruct((b, N_HEADS, HEAD_DIM, s), BF16),
        compiler_params=_cparams(("parallel", "arbitrary")),
        name="attn",
    )(q, kc, vc, ks, vs, kw, vw, gates, ovl_t, place)


POOL_HALO = 16


def _sigmoid(x):
    return 0.5 * jnp.tanh(0.5 * x) + 0.5


def _mix_kernel(x_ref, m0_ref, m1_ref, u_ref, halo_ref, o_ref, wh_ref, wp_ref, ps_ref, wo_ref,
                gf_ref, wr_ref, br_ref, y_ref, *, tm, seq):
    it = pl.program_id(0) % (seq // tm)
    u = u_ref[...]
    halo = jnp.where(it == 0, 0.0, halo_ref[...])
    ext = jnp.concatenate([halo, u], axis=0)
    sums = []
    acc = ext
    for sh in (1, 2, 4, 8):
        acc = acc + pltpu.roll(acc, sh, axis=0)
        sums.append(acc)
    t_in = it * tm + lax.broadcasted_iota(jnp.int32, (tm, 1), 0)
    pool_parts = []
    for gi, win in enumerate(POOL_WINDOWS):
        sl = slice(gi * POOL_GROUP_WIDTH, (gi + 1) * POOL_GROUP_WIDTH)
        cnt = jnp.minimum(t_in + 1, win).astype(F32)
        pooled = sums[gi][POOL_HALO:, sl] / cnt - u[:, sl]
        pool_parts.append(jnp.dot(pooled.astype(BF16), wp_ref[gi], preferred_element_type=F32))
    pool_branch = jnp.concatenate(pool_parts, axis=1) * ps_ref[...]
    attn_branch = jnp.concatenate(
        [_tn_dot(o_ref[0, h], wh_ref[h]) for h in range(N_HEADS)], axis=1)
    y = _sigmoid(m0_ref[...]) * attn_branch + _sigmoid(m1_ref[...]) * pool_branch
    x1 = x_ref[...] + jnp.dot(y.astype(BF16), wo_ref[...], preferred_element_type=F32)
    y_ref[:, 0:D_MODEL] = x1
    inv = lax.rsqrt(jnp.mean(x1 * x1, axis=-1, keepdims=True) + RMS_EPS)
    ht = x1 * inv * gf_ref[...]
    h_hi = ht.astype(BF16)
    h_lo = (ht - h_hi.astype(F32)).astype(BF16)
    logits = (jnp.dot(h_hi, wr_ref[0], preferred_element_type=F32)
              + jnp.dot(h_lo, wr_ref[0], preferred_element_type=F32)
              + jnp.dot(h_hi, wr_ref[1], preferred_element_type=F32)) + br_ref[...]
    y_ref[:, D_MODEL:D_MODEL + LANES] = _route(logits)


def _mix(x2, proj, o, wh, wp, ps, wo, gf, wr, br, s, tm=256):
    t = x2.shape[0]
    nt = s // tm
    hb = tm // POOL_HALO
    const = lambda shape: pl.BlockSpec(shape, lambda i: (0,) * len(shape))
    return pl.pallas_call(
        functools.partial(_mix_kernel, tm=tm, seq=s),
        grid=(t // tm,),
        in_specs=[
            pl.BlockSpec((tm, D_MODEL), lambda i: (i, 0)),
            pl.BlockSpec((tm, D_MODEL), lambda i: (i, COL_M0 // 2)),
            pl.BlockSpec((tm, D_MODEL), lambda i: (i, COL_M1 // 2)),
            pl.BlockSpec((tm, POOL_WIDTH), lambda i: (i, COL_POOL)),
            pl.BlockSpec((POOL_HALO, POOL_WIDTH), lambda i: (jnp.maximum(i * hb - 1, 0), COL_POOL)),
            pl.BlockSpec((1, N_HEADS, HEAD_DIM, tm), lambda i: (i // nt, 0, 0, i % nt)),
            const(wh.shape), const(wp.shape), const(ps.shape), const(wo.shape),
            const(gf.shape), const(wr.shape), const(br.shape),
        ],
        out_specs=pl.BlockSpec((tm, ROW_WIDTH), lambda i: (i, 0)),
        out_shape=jax.ShapeDtypeStruct((t, ROW_WIDTH), F32),
        compiler_params=_cparams(("parallel",)),
        name="mix",
    )(x2, proj, proj, proj, proj, o, wh, wp, ps, wo, gf, wr, br)


ROUTE_EXPERT_OFF = N_EXPERT_GROUPS
ROW_WIDTH = D_MODEL + LANES
META_GROUP_LANE = EXPERTS_PER_GROUP


def _route(logits):
    lane = lax.broadcasted_iota(jnp.int32, logits.shape, 1)
    is_g = lane < N_EXPERT_GROUPS
    gmax = jnp.max(jnp.where(is_g, logits, -jnp.inf), axis=-1, keepdims=True)
    ge = jnp.where(is_g, jnp.exp(logits - gmax), 0.0)
    gprob = ge / jnp.sum(ge, axis=-1, keepdims=True)
    p_group = jnp.max(jnp.where(is_g, gprob, -jnp.inf), axis=-1, keepdims=True)
    g_sel = jnp.min(jnp.where(is_g & (gprob == p_group), lane, LANES), axis=-1, keepdims=True)
    e_lane = lane - ROUTE_EXPERT_OFF
    in_grp = ((e_lane >= 0) & (e_lane < N_EXPERTS)
              & (jnp.right_shift(e_lane, EXPERTS_PER_GROUP.bit_length() - 1) == g_sel))
    el = jnp.where(in_grp, logits, -jnp.inf)
    v1 = jnp.max(el, axis=-1, keepdims=True)
    i1 = jnp.min(jnp.where(in_grp & (el == v1), lane, LANES), axis=-1, keepdims=True)
    el2 = jnp.where(lane == i1, -jnp.inf, el)
    v2 = jnp.max(el2, axis=-1, keepdims=True)
    i2 = jnp.min(jnp.where(in_grp & (lane != i1) & (el2 == v2), lane, LANES), axis=-1, keepdims=True)
    e2 = jnp.exp(v2 - v1)
    den = 1.0 + e2
    first = ROUTE_EXPERT_OFF + EXPERTS_PER_GROUP * g_sel
    return (jnp.where(lane == i1 - first, (1.0 / den) * p_group, 0.0)
            + jnp.where(lane == i2 - first, (e2 / den) * p_group, 0.0)
            + jnp.where(lane == META_GROUP_LANE, g_sel.astype(F32), 0.0))


def _moe_kernel(tgrp_ref, nval_ref, tok_ref, tok_next_ref, xa_ref, g_ref, wg_ref, wu_ref, wd_ref, out_ref,
                buf_ref, obuf_ref, sem_in, sem_out, *, tm, n_tiles):
    i = pl.program_id(0)
    slot = i % 2

    def row_in(tok_smem, r, s):
        return pltpu.make_async_copy(xa_ref.at[pl.ds(tok_smem[0, 0, r], 1)], buf_ref.at[s, pl.ds(r, 1)],
                                     sem_in.at[s])

    def row_out(r, s):
        return pltpu.make_async_copy(obuf_ref.at[s, pl.ds(r, 1)], out_ref.at[pl.ds(tok_ref[0, 0, r], 1)],
                                     sem_out.at[s])

    def start_gather(tok_smem, s):
        def body(r, carry):
            row_in(tok_smem, r, s).start()
            return carry
        lax.fori_loop(0, tm, body, 0, unroll=8)

    def wait_scatter(tile, s):
        n = nval_ref[tile]

        def chunk(c, carry):
            pltpu.make_async_copy(obuf_ref.at[s, pl.ds(0, WAIT_CHUNK)], out_ref.at[pl.ds(0, WAIT_CHUNK)],
                                  sem_out.at[s]).wait()
            return carry

        def row(r, carry):
            row_out(0, s).wait()
            return carry
        lax.fori_loop(0, jnp.right_shift(n, WAIT_CHUNK.bit_length() - 1), chunk, 0)
        lax.fori_loop(0, n & (WAIT_CHUNK - 1), row, 0)

    @pl.when(i == 0)
    def _():
        start_gather(tok_ref, 0)

    nxt = jnp.minimum(i + 1, n_tiles - 1)
    tile_in = pltpu.make_async_copy(xa_ref.at[pl.ds(0, tm)], buf_ref.at[slot], sem_in.at[slot])
    next_in = pltpu.make_async_copy(xa_ref.at[pl.ds(0, tm)], buf_ref.at[1 - slot], sem_in.at[1 - slot])
    n_seg = 2 * EXPERTS_PER_GROUP

    def prefetch_segment(seg):
        for r in range(seg * tm // n_seg, (seg + 1) * tm // n_seg):
            row_in(tok_next_ref, r, 1 - slot).start()

    @pl.when(i >= 2)
    def _():
        wait_scatter(i - 2, slot)

    @pl.when(nval_ref[i] > 0)
    def _():
        tile_in.wait()
        x = buf_ref[slot, :, 0:D_MODEL]
        meta = buf_ref[slot, :, D_MODEL:ROW_WIDTH]
        inv = lax.rsqrt(jnp.mean(x * x, axis=-1, keepdims=True) + RMS_EPS)
        h = (x * inv * g_ref[...]).astype(BF16)
        hids = []
        for e in range(EXPERTS_PER_GROUP):
            a = jnp.dot(h, wg_ref[e], preferred_element_type=F32)
            prefetch_segment(2 * e)
            up = jnp.dot(h, wu_ref[e], preferred_element_type=F32)
            prefetch_segment(2 * e + 1)
            hids.append(a * _sigmoid(a) * up * meta[:, e:e + 1])
        hid = jnp.concatenate(hids, axis=1).astype(BF16)
        obuf_ref[slot] = x + jnp.dot(hid, wd_ref[0], preferred_element_type=F32)

        n = nval_ref[i]
        n_main = n & ~(SCATTER_UNROLL - 1)

        def chunk(c, carry):
            for u in range(SCATTER_UNROLL):
                row_out(c * SCATTER_UNROLL + u, slot).start()
            return carry

        def row(r, carry):
            row_out(n_main + r, slot).start()
            return carry
        lax.fori_loop(0, jnp.right_shift(n, SCATTER_UNROLL.bit_length() - 1), chunk, 0)
        lax.fori_loop(0, n - n_main, row, 0)

        @pl.when((i + 1 >= n_tiles) | (nval_ref[nxt] == 0))
        def _():
            next_in.wait()

    @pl.when(i == n_tiles - 1)
    def _():
        @pl.when(i >= 1)
        def _():
            wait_scatter(i - 1, 1 - slot)
        wait_scatter(i, slot)


def _moe(xa, g, tgrp, nval, tok, wg, wu, wd, tm):
    t = xa.shape[0]
    n_tiles = tok.shape[0]
    hid_w = EXPERTS_PER_GROUP * D_EXPERT
    smem_tile = lambda off: pl.BlockSpec(
        (1, 1, tm), lambda i, tg, nv: (jnp.minimum(i + off, n_tiles - 1), 0, 0), memory_space=pltpu.SMEM)
    grid_spec = pltpu.PrefetchScalarGridSpec(
        num_scalar_prefetch=2,
        grid=(n_tiles,),
        in_specs=[
            smem_tile(0), smem_tile(1),
            pl.BlockSpec(memory_space=pl.ANY),
            pl.BlockSpec((1, D_MODEL), lambda i, tg, nv: (0, 0)),
            pl.BlockSpec((EXPERTS_PER_GROUP, D_MODEL, D_EXPERT), lambda i, tg, nv: (tg[i], 0, 0)),
            pl.BlockSpec((EXPERTS_PER_GROUP, D_MODEL, D_EXPERT), lambda i, tg, nv: (tg[i], 0, 0)),
            pl.BlockSpec((1, hid_w, D_MODEL), lambda i, tg, nv: (tg[i], 0, 0)),
        ],
        out_specs=pl.BlockSpec(memory_space=pl.ANY),
        scratch_shapes=[pltpu.VMEM((2, tm, ROW_WIDTH), F32), pltpu.VMEM((2, tm, D_MODEL), F32),
                        pltpu.SemaphoreType.DMA((2,)), pltpu.SemaphoreType.DMA((2,))],
    )
    return pl.pallas_call(
        functools.partial(_moe_kernel, tm=tm, n_tiles=n_tiles),
        grid_spec=grid_spec,
        out_shape=jax.ShapeDtypeStruct((t, D_MODEL), F32),
        compiler_params=_cparams(("arbitrary",)),
        name="moe",
    )(tgrp, nval, tok, tok, xa, g, wg, wu, wd)


MOE_TM = 512
WAIT_CHUNK = 64
SCATTER_UNROLL = 8


def _moe_plan(gid, tm):
    t = gid.shape[0]
    n_tiles = t // tm + N_EXPERT_GROUPS
    order = jnp.argsort(gid, stable=True).astype(jnp.int32)
    counts = jnp.sum((gid[:, None] == jnp.arange(N_EXPERT_GROUPS)[None, :]).astype(jnp.int32), axis=0)
    tiles_per = (counts + tm - 1) // tm
    tile_end = jnp.cumsum(tiles_per)
    row_start = jnp.cumsum(counts) - counts
    tile = jnp.arange(n_tiles, dtype=jnp.int32)
    tgrp = jnp.minimum(jnp.sum((tile[:, None] >= tile_end[None, :]).astype(jnp.int32), axis=1),
                       N_EXPERT_GROUPS - 1)
    local = tile - (tile_end - tiles_per)[tgrp]
    nval = jnp.clip(counts[tgrp] - local * tm, 0, tm)
    slot = jnp.arange(tm, dtype=jnp.int32)[None, :]
    src = row_start[tgrp][:, None] + local[:, None] * tm + jnp.where(slot < nval[:, None], slot, 0)
    tok = order[jnp.clip(src, 0, t - 1)]
    return tgrp.astype(jnp.int32), nval.astype(jnp.int32), tok.reshape(n_tiles, 1, tm)


def _rope_lane_freq():
    inv_freq = ROPE_THETA ** (-jnp.arange(0, ROPE_DIM, 2, dtype=F32) / ROPE_DIM)
    d = np.arange(LANES) % HEAD_DIM
    f = jnp.where(jnp.asarray(d < ROPE_DIM), inv_freq[np.asarray(d % (ROPE_DIM // 2))], 0.0)
    return f.reshape(1, LANES).astype(F32)


def _head_of_lane(width):
    return jnp.asarray((np.arange(width) // HEAD_DIM)[:, None] == np.arange(LANES)[None, :], dtype=BF16)


def _overlap_t(n_cmp_pad, n_blk):
    a0 = np.arange(n_cmp_pad)[:, None] * CMP_STRIDE
    b0 = np.arange(n_blk)[None, :] * SEL_BLOCK
    ov = np.clip(np.minimum(a0 + CMP_BLOCK, b0 + SEL_BLOCK) - np.maximum(a0, b0), 0, None) / CMP_BLOCK
    return jnp.asarray(ov.T, dtype=BF16)


def _bias_place(n_blk):
    return jnp.asarray(np.arange(n_blk)[:, None] + BLK_LANE0 == np.arange(LANES)[None, :], dtype=BF16)


def _layer(x, positions, attn_norm_g, w_in, q_norm_g, k_norm_cmp_g, k_norm_slc_g, k_norm_swa_g,
           cmp_pos_emb_k, cmp_w1_k, cmp_w2_k, cmp_pos_emb_v, cmp_w1_v, cmp_w2_v, w_head_out, w_pool,
           pool_scale, w_out, ffn_norm_g, w_router_group, b_router_group, w_router_expert,
           b_router_expert, w_expert_gate, w_expert_up, w_expert_down):
    b, s, d = x.shape
    t = b * s
    x2 = x.reshape(t, d)

    o_q, o_kv, o_g = ATTN_WIDTH, ATTN_WIDTH + 6 * KV_WIDTH, ATTN_WIDTH + 6 * KV_WIDTH + N_HEADS * N_BRANCH
    o_p = o_g + POOL_WIDTH
    used = 2 * ATTN_WIDTH + 2 * D_MODEL + 6 * KV_WIDTH + N_HEADS * N_BRANCH
    w_packed = jnp.concatenate(
        [w_in[:, :o_q], w_in[:, o_g:o_p], w_in[:, o_p:], w_in[:, o_q:o_kv], w_in[:, o_kv:o_g],
         jnp.zeros((d, PROJ_WIDTH - used), w_in.dtype)], axis=1).astype(BF16)
    proj = _proj(x2, attn_norm_g.reshape(1, d), w_packed)

    f_lane = _rope_lane_freq()
    tile_h = lambda g_, n: jnp.tile(g_, n).reshape(1, n * HEAD_DIM)
    q, ks, vs, kw, vw, kc_raw, vc_raw, gates = _prep(
        proj, positions.reshape(t, 1), f_lane, tile_h(q_norm_g, N_HEADS), tile_h(k_norm_slc_g, N_KV),
        tile_h(k_norm_swa_g, N_KV), _head_of_lane(ATTN_WIDTH), _head_of_lane(KV_WIDTH), b, s)

    n_row = s // CMP_STRIDE
    half = CMP_STRIDE * HEAD_DIM
    xk = kc_raw.reshape(b, N_KV, n_row, half)
    xv = vc_raw.reshape(b, N_KV, n_row, half)
    pad2 = lambda w2: jnp.pad(w2, ((0, 0), (0, LANES - HEAD_DIM))).astype(BF16)
    cmp_end = np.minimum(np.arange(n_row) * CMP_STRIDE + CMP_BLOCK - 1, s - 1)
    posc = positions[:, cmp_end].reshape(b, n_row, 1)
    gk = jnp.pad(k_norm_cmp_g, (0, LANES - HEAD_DIM)).reshape(1, LANES)
    kc, vc = _compress(xk, xv, cmp_pos_emb_k.reshape(2, half), cmp_pos_emb_v.reshape(2, half),
                       cmp_w1_k.astype(BF16), pad2(cmp_w2_k), cmp_w1_v.astype(BF16), pad2(cmp_w2_v),
                       gk, posc, f_lane)

    n_blk = s // SEL_BLOCK
    assert n_blk <= LANES - BLK_LANE0
    o = _attn(q, kc, vc, ks, vs, kw, vw, gates, _overlap_t(n_row, n_blk), _bias_place(n_blk))

    wr = jnp.concatenate([w_router_group, w_router_expert,
                          jnp.zeros((d, LANES - N_EXPERT_GROUPS - N_EXPERTS), F32)], axis=1)
    br = jnp.concatenate([b_router_group, b_router_expert,
                          jnp.zeros((LANES - N_EXPERT_GROUPS - N_EXPERTS,), F32)]).reshape(1, LANES)
    gf = ffn_norm_g.reshape(1, d)
    wr_hi = wr.astype(BF16)
    wr2 = jnp.stack([wr_hi, (wr - wr_hi.astype(F32)).astype(BF16)])
    xa = _mix(x2, proj, o, w_head_out.astype(BF16), w_pool.astype(BF16), pool_scale.reshape(1, d),
              w_out.astype(BF16), gf, wr2, br, s)

    tgrp, nval, tok = _moe_plan(xa[:, D_MODEL + META_GROUP_LANE].astype(jnp.int32), MOE_TM)
    wd = w_expert_down.reshape(N_EXPERT_GROUPS, EXPERTS_PER_GROUP * D_EXPERT, d)
    x_out = _moe(xa, gf, tgrp, nval, tok, w_expert_gate.astype(BF16), w_expert_up.astype(BF16),
                 wd.astype(BF16), MOE_TM)
    return x_out.reshape(b, s, d)


def kernel(x, positions, attn_norm_g, w_in, q_norm_g, k_norm_cmp_g, k_norm_slc_g, k_norm_swa_g,
           cmp_pos_emb_k, cmp_w1_k, cmp_w2_k, cmp_pos_emb_v, cmp_w1_v, cmp_w2_v, w_head_out, w_pool,
           pool_scale, w_out, ffn_norm_g, w_router_group, b_router_group, w_router_expert,
           b_router_expert, w_expert_gate, w_expert_up, w_expert_down):
    params = (attn_norm_g, w_in, q_norm_g, k_norm_cmp_g, k_norm_slc_g, k_norm_swa_g, cmp_pos_emb_k,
              cmp_w1_k, cmp_w2_k, cmp_pos_emb_v, cmp_w1_v, cmp_w2_v, w_head_out, w_pool, pool_scale,
              w_out, ffn_norm_g, w_router_group, b_router_group, w_router_expert, b_router_expert,
              w_expert_gate, w_expert_up, w_expert_down)
    for layer in range(attn_norm_g.shape[0]):
        x = _layer(x, positions, *[p[layer] for p in params])
    return x
```

```python
import functools

import jax
import jax.numpy as jnp
import numpy as np
from jax import lax
from jax.experimental import pallas as pl
from jax.experimental.pallas import tpu as pltpu

F32 = jnp.float32
BF16 = jnp.bfloat16

D_MODEL = 2048
N_HEADS = 16
HEAD_DIM = 64
N_KV = 4
HEADS_PER_KV = N_HEADS // N_KV
ATTN_WIDTH = N_HEADS * HEAD_DIM
KV_WIDTH = N_KV * HEAD_DIM
N_BRANCH = 3
CMP_BLOCK = 32
CMP_STRIDE = 16
CMP_HIDDEN = 4 * HEAD_DIM
SEL_BLOCK = 64
N_SEL = 16
N_FORCED_LOCAL = 2
WINDOW = 512
FORCE_BONUS = 1e4
NEG_INF = -1e30
ROPE_THETA = 500000.0
ROPE_DIM = HEAD_DIM // 4
HEAD_OUT_WIDTH = D_MODEL // N_HEADS
POOL_WINDOWS = (2, 4, 8, 16)
POOL_GROUPS = 4
POOL_WIDTH = 1024
POOL_GROUP_WIDTH = POOL_WIDTH // POOL_GROUPS
POOL_OUT_WIDTH = D_MODEL // POOL_GROUPS
N_EXPERT_GROUPS = 4
EXPERTS_PER_GROUP = 4
N_EXPERTS = N_EXPERT_GROUPS * EXPERTS_PER_GROUP
D_EXPERT = 256
RMS_EPS = 1e-6

LANES = 128
Q_SCALE = HEAD_DIM ** -0.5 * float(np.log2(np.e))
BLK_LANE0 = HEAD_DIM
V_ONES = 16
PROJ_WIDTH = 8192
PROJ_TN = 1024
COL_Q, COL_POOL, COL_M0, COL_M1, COL_KVA, COL_KVB = 0, 1, 2, 4, 6, 7
GATE_OFF = 2 * KV_WIDTH
VMEM_LIMIT = 56 * 1024 * 1024


def _cparams(sem):
    return pltpu.CompilerParams(dimension_semantics=sem, vmem_limit_bytes=VMEM_LIMIT)


def _nt_dot(a, b):
    return lax.dot_general(a, b, (((1,), (1,)), ((), ())), preferred_element_type=F32)


def _tn_dot(a, b):
    return lax.dot_general(a, b, (((0,), (0,)), ((), ())), preferred_element_type=F32)


def _split_dot(x, e):
    hi = x.astype(BF16)
    lo = (x - hi.astype(F32)).astype(BF16)
    return (jnp.dot(hi, e, preferred_element_type=F32)
            + jnp.dot(lo, e, preferred_element_type=F32))


def _proj_kernel(x_ref, g_ref, w_ref, o_ref, h_ref):
    @pl.when(pl.program_id(1) == 0)
    def _():
        x = x_ref[...]
        inv = lax.rsqrt(jnp.mean(x * x, axis=-1, keepdims=True) + RMS_EPS)
        h_ref[...] = (x * inv * g_ref[...]).astype(BF16)

    o_ref[...] = jnp.dot(h_ref[...], w_ref[...], preferred_element_type=F32)


def _proj(x2, g, w_packed, tm=1024, tn=2048):
    t = x2.shape[0]
    return pl.pallas_call(
        _proj_kernel,
        grid=(t // tm, PROJ_WIDTH // tn),
        in_specs=[
            pl.BlockSpec((tm, D_MODEL), lambda i, j: (i, 0)),
            pl.BlockSpec((1, D_MODEL), lambda i, j: (0, 0)),
            pl.BlockSpec((D_MODEL, tn), lambda i, j: (0, j)),
        ],
        out_specs=pl.BlockSpec((tm, tn), lambda i, j: (i, j)),
        out_shape=jax.ShapeDtypeStruct((t, PROJ_WIDTH), F32),
        scratch_shapes=[pltpu.VMEM((tm, D_MODEL), BF16)],
        compiler_params=_cparams(("parallel", "arbitrary")),
        name="proj",
    )(x2, g, w_packed)


def _rope(xn, cos, sin):
    w = xn.shape[-1]
    lane = lax.broadcasted_iota(jnp.int32, xn.shape, 1) & (HEAD_DIM - 1)
    up = pltpu.roll(xn, w - ROPE_DIM // 2, axis=1)
    dn = pltpu.roll(xn, ROPE_DIM // 2, axis=1)
    return jnp.where(lane < ROPE_DIM // 2, xn * cos - up * sin,
                     jnp.where(lane < ROPE_DIM, xn * cos + dn * sin, xn))


def _head_norm_rope(x, gain, head_of_lane, cos, sin):
    w = x.shape[-1]
    per_head = _split_dot(x * x, head_of_lane)
    hi = per_head.astype(BF16)
    lo = (per_head - hi.astype(F32)).astype(BF16)
    ssq = _nt_dot(hi, head_of_lane) + _nt_dot(lo, head_of_lane)
    xn = x * lax.rsqrt(ssq * (1.0 / HEAD_DIM) + RMS_EPS) * gain
    reps = w // LANES
    return _rope(xn, jnp.concatenate([cos] * reps, axis=1), jnp.concatenate([sin] * reps, axis=1))


def _widen_head(x, h, upper):
    col = x[:, (h // 2) * LANES:(h // 2 + 1) * LANES]
    if h % 2:
        col = pltpu.roll(col, HEAD_DIM, axis=1)
    lane = lax.broadcasted_iota(jnp.int32, col.shape, 1)
    return jnp.where(lane < HEAD_DIM, col, upper)


def _prep_kernel(q_ref, kva_ref, kvb_ref, pos_ref, f_ref, gq_ref, gs_ref, gw_ref, eq_ref, ek_ref,
                 qo_ref, kso_ref, vso_ref, kwo_ref, vwo_ref, kco_ref, vco_ref, go_ref):
    tm = q_ref.shape[0]
    ang = pos_ref[...].astype(F32) * f_ref[...]
    cos, sin = jnp.cos(ang), jnp.sin(ang)
    q = _head_norm_rope(q_ref[...], gq_ref[...], eq_ref[...], cos, sin) * Q_SCALE
    lane = lax.broadcasted_iota(jnp.int32, (tm, LANES), 1)
    t_in = pl.program_id(1) * tm + lax.broadcasted_iota(jnp.int32, (tm, LANES), 0)
    blk_onehot = (jnp.right_shift(t_in, SEL_BLOCK.bit_length() - 1) == lane - BLK_LANE0).astype(F32)
    for h in range(N_HEADS):
        qo_ref[0, h // HEADS_PER_KV, h % HEADS_PER_KV] = _widen_head(q, h, 0.0).astype(BF16)
    kva = kva_ref[...]
    kvb = kvb_ref[...]
    ks = _head_norm_rope(kva[:, 2 * KV_WIDTH:3 * KV_WIDTH], gs_ref[...], ek_ref[...], cos, sin)
    kw = _head_norm_rope(kvb[:, 0:KV_WIDTH], gw_ref[...], ek_ref[...], cos, sin)
    vs = kva[:, 3 * KV_WIDTH:4 * KV_WIDTH]
    vw = kvb[:, KV_WIDTH:2 * KV_WIDTH]
    kc = kva[:, 0:KV_WIDTH]
    vc = kva[:, KV_WIDTH:2 * KV_WIDTH]
    for g in range(N_KV):
        sl = slice(g * HEAD_DIM, (g + 1) * HEAD_DIM)
        kso_ref[0, g] = _widen_head(ks, g, blk_onehot).astype(BF16)
        kwo_ref[0, g] = _widen_head(kw, g, 0.0).astype(BF16)
        kco_ref[0, g] = kc[:, sl]
        vco_ref[0, g] = vc[:, sl]
    ones = jnp.ones((V_ONES, tm), BF16)
    for v, vo_ref in ((vs, vso_ref), (vw, vwo_ref)):
        vt = v.T.astype(BF16)
        for g in range(N_KV):
            vo_ref[0, g, 0:HEAD_DIM, :] = vt[g * HEAD_DIM:(g + 1) * HEAD_DIM]
            vo_ref[0, g, HEAD_DIM:HEAD_DIM + V_ONES, :] = ones
    go_ref[0] = jax.nn.sigmoid(kvb[:, GATE_OFF:GATE_OFF + LANES]).T


def _prep(proj, pos2, f_lane, gq, gs, gw, e_q, e_k, b, s, tm=512):
    nt = s // tm
    row = lambda bb, i: bb * nt + i
    kv_shape = jax.ShapeDtypeStruct((b, N_KV, s, HEAD_DIM), F32)
    kv_spec = pl.BlockSpec((1, N_KV, tm, HEAD_DIM), lambda bb, i: (bb, 0, i, 0))
    kx_shape = jax.ShapeDtypeStruct((b, N_KV, s, LANES), BF16)
    kx_spec = pl.BlockSpec((1, N_KV, tm, LANES), lambda bb, i: (bb, 0, i, 0))
    vt_shape = jax.ShapeDtypeStruct((b, N_KV, HEAD_DIM + V_ONES, s), BF16)
    vt_spec = pl.BlockSpec((1, N_KV, HEAD_DIM + V_ONES, tm), lambda bb, i: (bb, 0, 0, i))
    const = lambda shape: pl.BlockSpec(shape, lambda bb, i: (0,) * len(shape))
    return pl.pallas_call(
        _prep_kernel,
        grid=(b, nt),
        in_specs=[
            pl.BlockSpec((tm, PROJ_TN), lambda bb, i: (row(bb, i), COL_Q)),
            pl.BlockSpec((tm, PROJ_TN), lambda bb, i: (row(bb, i), COL_KVA)),
            pl.BlockSpec((tm, PROJ_TN), lambda bb, i: (row(bb, i), COL_KVB)),
            pl.BlockSpec((tm, 1), lambda bb, i: (row(bb, i), 0)),
            const((1, LANES)), const((1, ATTN_WIDTH)), const((1, KV_WIDTH)), const((1, KV_WIDTH)),
            const((ATTN_WIDTH, LANES)), const((KV_WIDTH, LANES)),
        ],
        out_specs=[
            pl.BlockSpec((1, N_KV, HEADS_PER_KV, tm, LANES), lambda bb, i: (bb, 0, 0, i, 0)),
            kx_spec, vt_spec, kx_spec, vt_spec, kv_spec, kv_spec,
            pl.BlockSpec((1, LANES, tm), lambda bb, i: (bb, 0, i)),
        ],
        out_shape=[
            jax.ShapeDtypeStruct((b, N_KV, HEADS_PER_KV, s, LANES), BF16),
            kx_shape, vt_shape, kx_shape, vt_shape, kv_shape, kv_shape,
            jax.ShapeDtypeStruct((b, LANES, s), F32),
        ],
        compiler_params=_cparams(("parallel", "parallel")),
        name="prep",
    )(proj, proj, proj, pos2, f_lane, gq, gs, gw, e_q, e_k)


def _compress_one(x, pe_a, pe_b, w1_ref, w2_ref):
    half = CMP_STRIDE * HEAD_DIM
    nxt = pltpu.roll(x, x.shape[0] - 1, axis=0)
    a = (x + pe_a).astype(BF16)
    b = (nxt + pe_b).astype(BF16)
    hid = (jnp.dot(a, w1_ref[0:half, :], preferred_element_type=F32)
           + jnp.dot(b, w1_ref[half:2 * half, :], preferred_element_type=F32))
    hid = hid * jax.nn.sigmoid(hid)
    return jnp.dot(hid.astype(BF16), w2_ref[...], preferred_element_type=F32)


def _compress_kernel(xk_ref, xv_ref, pek_ref, pev_ref, w1k_ref, w2k_ref, w1v_ref, w2v_ref,
                     gk_ref, posc_ref, f_ref, kc_ref, vc_ref):
    ang = posc_ref[0].astype(F32) * f_ref[...]
    cos, sin = jnp.cos(ang), jnp.sin(ang)
    for g in range(N_KV):
        k = _compress_one(xk_ref[0, g], pek_ref[0:1, :], pek_ref[1:2, :], w1k_ref, w2k_ref)
        ssq = jnp.sum(k * k, axis=-1, keepdims=True)
        kn = k * lax.rsqrt(ssq * (1.0 / HEAD_DIM) + RMS_EPS) * gk_ref[...]
        kn = _rope(kn, cos, sin)
        kc_ref[0, g] = kn.astype(BF16)
        v = _compress_one(xv_ref[0, g], pev_ref[0:1, :], pev_ref[1:2, :], w1v_ref, w2v_ref)
        vc_ref[0, g] = v[:, 0:HEAD_DIM].astype(BF16)


def _compress(xk, xv, pek, pev, w1k, w2k, w1v, w2v, gk, posc, f_lane):
    b = xk.shape[0]
    nrow = xk.shape[2]
    half = CMP_STRIDE * HEAD_DIM
    const = lambda shape: pl.BlockSpec(shape, lambda bb: (0,) * len(shape))
    x_spec = pl.BlockSpec((1, N_KV, nrow, half), lambda bb: (bb, 0, 0, 0))
    o_spec = pl.BlockSpec((1, N_KV, nrow, HEAD_DIM), lambda bb: (bb, 0, 0, 0))
    o_shape = jax.ShapeDtypeStruct((b, N_KV, nrow, HEAD_DIM), BF16)
    k_spec = pl.BlockSpec((1, N_KV, nrow, LANES), lambda bb: (bb, 0, 0, 0))
    k_shape = jax.ShapeDtypeStruct((b, N_KV, nrow, LANES), BF16)
    return pl.pallas_call(
        _compress_kernel,
        grid=(b,),
        in_specs=[x_spec, x_spec, const((2, half)), const((2, half)),
                  const((2 * half, CMP_HIDDEN)), const((CMP_HIDDEN, LANES)),
                  const((2 * half, CMP_HIDDEN)), const((CMP_HIDDEN, LANES)),
                  const((1, LANES)), pl.BlockSpec((1, nrow, 1), lambda bb: (bb, 0, 0)), const((1, LANES))],
        out_specs=[k_spec, o_spec],
        out_shape=[k_shape, o_shape],
        compiler_params=_cparams(("parallel",)),
        name="compress",
    )(xk, xv, pek, pev, w1k, w2k, w1v, w2v, gk, posc, f_lane)


def _attn_kernel(q_ref, kc_ref, vc_ref, ks_ref, vs_ref, kw_ref, vw_ref, gate_ref, ovl_ref, place_ref,
                 o_ref, *, tq, seq):
    r = HEADS_PER_KV
    cols = r * tq
    n_cmp = kc_ref.shape[2]
    n_blk = seq // SEL_BLOCK
    i = pl.program_id(1)
    t0 = i * tq
    gates = gate_ref[0]

    kk = lax.broadcasted_iota(jnp.int32, (tq, cols), 0)
    tt = lax.broadcasted_iota(jnp.int32, (tq, cols), 1) & (tq - 1)
    causal_bias = jnp.where(kk <= tt, 0.0, NEG_INF)
    band_bias = jnp.where(kk > tt, 0.0, NEG_INF)

    q_plain, q_sel, o_cmps = [], [], []

    for g in range(N_KV):
        q3 = q_ref[0, g]
        qg = q3.reshape(cols, LANES)

        s = _nt_dot(kc_ref[0, g], qg)
        n_idx = lax.broadcasted_iota(jnp.int32, (n_cmp, cols), 0)
        t_col = t0 + (lax.broadcasted_iota(jnp.int32, (n_cmp, cols), 1) & (tq - 1))
        mask = (n_idx * CMP_STRIDE + (CMP_BLOCK - 1)) <= t_col
        s = jnp.where(mask, s, NEG_INF)
        p = jnp.where(mask, jnp.exp2(s - jnp.max(s, axis=0, keepdims=True)), 0.0)
        l = jnp.sum(p, axis=0, keepdims=True)
        p = p * (1.0 / jnp.where(l > 0.0, l, 1.0))
        o_cmp = _tn_dot(vc_ref[0, g], p.astype(BF16))

        p_sum = p[:, 0:tq] + p[:, tq:2 * tq] + p[:, 2 * tq:3 * tq] + p[:, 3 * tq:4 * tq]
        hi = p_sum.astype(BF16)
        lo = (p_sum - hi.astype(F32)).astype(BF16)
        imp = (jnp.dot(ovl_ref[...], hi, preferred_element_type=F32)
               + jnp.dot(ovl_ref[...], lo, preferred_element_type=F32))
        j_idx = lax.broadcasted_iota(jnp.int32, (n_blk, tq), 0)
        t_idx = t0 + lax.broadcasted_iota(jnp.int32, (n_blk, tq), 1)
        dist = jnp.right_shift(t_idx, SEL_BLOCK.bit_length() - 1) - j_idx
        valid = dist >= 0
        forced = (j_idx == 0) | (valid & (dist < N_FORCED_LOCAL))
        score = jnp.where(valid, imp + FORCE_BONUS * forced.astype(F32), NEG_INF)
        cnt = jnp.zeros((n_blk, tq), F32)
        for jp in range(n_blk):
            row = score[jp:jp + 1, :]
            beats = (row > score) | ((row == score) & (j_idx > jp))
            cnt = cnt + beats.astype(F32)
        selected = (cnt < float(min(N_SEL, n_blk))) & valid
        blk_bias = jnp.where(selected, 0.0, NEG_INF).astype(BF16)
        q_bias = _tn_dot(blk_bias, place_ref[...]).astype(BF16)
        q_plain.append(qg)
        q_sel.append((q3 + q_bias[None]).reshape(cols, LANES))
        o_cmps.append(o_cmp)

    def sel_tile(g, k0, carry, bias):
        m_i, acc = carry
        s = _nt_dot(ks_ref[0, g, pl.ds(k0, tq), :], q_sel[g])
        if bias is not None:
            s = s + bias
        m_new = jnp.maximum(m_i, jnp.max(s, axis=0, keepdims=True))
        p = jnp.exp2((s - m_new).astype(BF16))
        acc = jnp.exp2(m_i - m_new) * acc + jnp.dot(vs_ref[0, g, :, pl.ds(k0, tq)], p,
                                                    preferred_element_type=F32)
        return m_new, acc

    def sel_body(kt, carries):
        k0 = pl.multiple_of(kt * tq, tq)
        return tuple(sel_tile(g, k0, carries[g], None) for g in range(N_KV))

    init = (jnp.full((1, cols), NEG_INF, F32), jnp.zeros((HEAD_DIM + V_ONES, cols), F32))
    carries = lax.fori_loop(0, i, sel_body, (init,) * N_KV)

    n_old = WINDOW // tq
    own = pl.multiple_of(t0, tq)
    oldest_bias = band_bias + jnp.where(t0 < WINDOW, NEG_INF, 0.0)
    for g in range(N_KV):
        starts = [t0 + (j - n_old) * tq for j in range(n_old + 1)]
        k0s = [pl.multiple_of(jnp.maximum(st, 0), tq) for st in starts]
        k_all = jnp.concatenate([ks_ref[0, g, pl.ds(own, tq), :]]
                                + [kw_ref[0, g, pl.ds(k0, tq), :] for k0 in k0s], axis=0)
        s_all = _nt_dot(k_all, q_sel[g])

        m_i, acc = carries[g]
        s = s_all[0:tq] + causal_bias
        m_new = jnp.maximum(m_i, jnp.max(s, axis=0, keepdims=True))
        acc_s = jnp.exp2(m_i - m_new) * acc + jnp.dot(vs_ref[0, g, :, pl.ds(own, tq)],
                                                      jnp.exp2((s - m_new).astype(BF16)),
                                                      preferred_element_type=F32)
        o_slc = acc_s[0:HEAD_DIM] * (1.0 / acc_s[HEAD_DIM:HEAD_DIM + 1])

        s_parts, v_parts = [], []
        for j in range(n_old + 1):
            s = s_all[(j + 1) * tq:(j + 2) * tq]
            if j == 0:
                s = s + oldest_bias
            elif j < n_old:
                s = s + jnp.where(starts[j] < 0, NEG_INF, 0.0)
            else:
                s = s + causal_bias
            s_parts.append(s)
            v_parts.append(vw_ref[0, g, :, pl.ds(k0s[j], tq)])
        m = functools.reduce(jnp.maximum, [jnp.max(s, axis=0, keepdims=True) for s in s_parts])
        acc_w = sum(jnp.dot(v, jnp.exp2((s - m).astype(BF16)), preferred_element_type=F32)
                    for s, v in zip(s_parts, v_parts))
        o_swa = acc_w[0:HEAD_DIM] * (1.0 / acc_w[HEAD_DIM:HEAD_DIM + 1])

        def gate_row(c):
            parts = [gates[(g * r + hh) * N_BRANCH + c:(g * r + hh) * N_BRANCH + c + 1, :] for hh in range(r)]
            return jnp.concatenate(parts, axis=1)

        o = gate_row(0) * o_cmps[g] + gate_row(1) * o_slc + gate_row(2) * o_swa
        for hh in range(r):
            o_ref[0, g * r + hh] = o[:, hh * tq:(hh + 1) * tq].astype(BF16)


def _attn(q, kc, vc, ks, vs, kw, vw, gates, ovl_t, place, tq=512):
    b, _, _, s, _ = q.shape
    n_cmp = kc.shape[2]
    nq = s // tq
    assert WINDOW % tq == 0 and tq % SEL_BLOCK == 0
    full = lambda shape: pl.BlockSpec((1,) + shape, lambda bb, i: (bb,) + (0,) * len(shape))
    const = lambda shape: pl.BlockSpec(shape, lambda bb, i: (0,) * len(shape))
    return pl.pallas_call(
        functools.partial(_attn_kernel, tq=tq, seq=s),
        grid=(b, nq),
        in_specs=[
            pl.BlockSpec((1, N_KV, HEADS_PER_KV, tq, LANES), lambda bb, i: (bb, 0, 0, i, 0)),
            full((N_KV, n_cmp, LANES)), full((N_KV, n_cmp, HEAD_DIM)),
            full((N_KV, s, LANES)), full((N_KV, HEAD_DIM + V_ONES, s)),
            full((N_KV, s, LANES)), full((N_KV, HEAD_DIM + V_ONES, s)),
            pl.BlockSpec((1, LANES, tq), lambda bb, i: (bb, 0, i)),
            const(ovl_t.shape), const(place.shape),
        ],
        out_specs=pl.BlockSpec((1, N_HEADS, HEAD_DIM, tq), lambda bb, i: (bb, 0, 0, i)),
        out_shape=jax.ShapeDtypeStruct((b, N_HEADS, HEAD_DIM, s), BF16),
        compiler_params=_cparams(("parallel", "arbitrary")),
        name="attn",
    )(q, kc, vc, ks, vs, kw, vw, gates, ovl_t, place)


POOL_HALO = 16


def _sigmoid(x):
    return 0.5 * jnp.tanh(0.5 * x) + 0.5


def _mix_kernel(x_ref, m0_ref, m1_ref, u_ref, halo_ref, o_ref, wh_ref, wp_ref, ps_ref, wo_ref,
                gf_ref, wr_ref, br_ref, y_ref, *, tm, seq):
    it = pl.program_id(0) % (seq // tm)
    u = u_ref[...]
    halo = jnp.where(it == 0, 0.0, halo_ref[...])
    ext = jnp.concatenate([halo, u], axis=0)
    sums = []
    acc = ext
    for sh in (1, 2, 4, 8):
        acc = acc + pltpu.roll(acc, sh, axis=0)
        sums.append(acc)
    t_in = it * tm + lax.broadcasted_iota(jnp.int32, (tm, 1), 0)
    pool_parts = []
    for gi, win in enumerate(POOL_WINDOWS):
        sl = slice(gi * POOL_GROUP_WIDTH, (gi + 1) * POOL_GROUP_WIDTH)
        cnt = jnp.minimum(t_in + 1, win).astype(F32)
        pooled = sums[gi][POOL_HALO:, sl] / cnt - u[:, sl]
        pool_parts.append(jnp.dot(pooled.astype(BF16), wp_ref[gi], preferred_element_type=F32))
    pool_branch = jnp.concatenate(pool_parts, axis=1) * ps_ref[...]
    attn_branch = jnp.concatenate(
        [_tn_dot(o_ref[0, h], wh_ref[h]) for h in range(N_HEADS)], axis=1)
    y = _sigmoid(m0_ref[...]) * attn_branch + _sigmoid(m1_ref[...]) * pool_branch
    x1 = x_ref[...] + jnp.dot(y.astype(BF16), wo_ref[...], preferred_element_type=F32)
    y_ref[:, 0:D_MODEL] = x1
    inv = lax.rsqrt(jnp.mean(x1 * x1, axis=-1, keepdims=True) + RMS_EPS)
    ht = x1 * inv * gf_ref[...]
    h_hi = ht.astype(BF16)
    h_lo = (ht - h_hi.astype(F32)).astype(BF16)
    logits = (jnp.dot(h_hi, wr_ref[0], preferred_element_type=F32)
              + jnp.dot(h_lo, wr_ref[0], preferred_element_type=F32)
              + jnp.dot(h_hi, wr_ref[1], preferred_element_type=F32)) + br_ref[...]
    y_ref[:, D_MODEL:D_MODEL + LANES] = _route(logits)


def _mix(x2, proj, o, wh, wp, ps, wo, gf, wr, br, s, tm=256):
    t = x2.shape[0]
    nt = s // tm
    hb = tm // POOL_HALO
    const = lambda shape: pl.BlockSpec(shape, lambda i: (0,) * len(shape))
    return pl.pallas_call(
        functools.partial(_mix_kernel, tm=tm, seq=s),
        grid=(t // tm,),
        in_specs=[
            pl.BlockSpec((tm, D_MODEL), lambda i: (i, 0)),
            pl.BlockSpec((tm, D_MODEL), lambda i: (i, COL_M0 // 2)),
            pl.BlockSpec((tm, D_MODEL), lambda i: (i, COL_M1 // 2)),
            pl.BlockSpec((tm, POOL_WIDTH), lambda i: (i, COL_POOL)),
            pl.BlockSpec((POOL_HALO, POOL_WIDTH), lambda i: (jnp.maximum(i * hb - 1, 0), COL_POOL)),
            pl.BlockSpec((1, N_HEADS, HEAD_DIM, tm), lambda i: (i // nt, 0, 0, i % nt)),
            const(wh.shape), const(wp.shape), const(ps.shape), const(wo.shape),
            const(gf.shape), const(wr.shape), const(br.shape),
        ],
        out_specs=pl.BlockSpec((tm, ROW_WIDTH), lambda i: (i, 0)),
        out_shape=jax.ShapeDtypeStruct((t, ROW_WIDTH), F32),
        compiler_params=_cparams(("parallel",)),
        name="mix",
    )(x2, proj, proj, proj, proj, o, wh, wp, ps, wo, gf, wr, br)


ROUTE_EXPERT_OFF = N_EXPERT_GROUPS
ROW_WIDTH = D_MODEL + LANES
META_GROUP_LANE = EXPERTS_PER_GROUP


def _route(logits):
    lane = lax.broadcasted_iota(jnp.int32, logits.shape, 1)
    is_g = lane < N_EXPERT_GROUPS
    gmax = jnp.max(jnp.where(is_g, logits, -jnp.inf), axis=-1, keepdims=True)
    ge = jnp.where(is_g, jnp.exp(logits - gmax), 0.0)
    gprob = ge / jnp.sum(ge, axis=-1, keepdims=True)
    p_group = jnp.max(jnp.where(is_g, gprob, -jnp.inf), axis=-1, keepdims=True)
    g_sel = jnp.min(jnp.where(is_g & (gprob == p_group), lane, LANES), axis=-1, keepdims=True)
    e_lane = lane - ROUTE_EXPERT_OFF
    in_grp = ((e_lane >= 0) & (e_lane < N_EXPERTS)
              & (jnp.right_shift(e_lane, EXPERTS_PER_GROUP.bit_length() - 1) == g_sel))
    el = jnp.where(in_grp, logits, -jnp.inf)
    v1 = jnp.max(el, axis=-1, keepdims=True)
    i1 = jnp.min(jnp.where(in_grp & (el == v1), lane, LANES), axis=-1, keepdims=True)
    el2 = jnp.where(lane == i1, -jnp.inf, el)
    v2 = jnp.max(el2, axis=-1, keepdims=True)
    i2 = jnp.min(jnp.where(in_grp & (lane != i1) & (el2 == v2), lane, LANES), axis=-1, keepdims=True)
    e2 = jnp.exp(v2 - v1)
    den = 1.0 + e2
    first = ROUTE_EXPERT_OFF + EXPERTS_PER_GROUP * g_sel
    return (jnp.where(lane == i1 - first, (1.0 / den) * p_group, 0.0)
            + jnp.where(lane == i2 - first, (e2 / den) * p_group, 0.0)
            + jnp.where(lane == META_GROUP_LANE, g_sel.astype(F32), 0.0))


def _moe_kernel(tgrp_ref, nval_ref, tok_ref, tok_next_ref, xa_ref, g_ref, wg_ref, wu_ref, wd_ref, out_ref,
                buf_ref, obuf_ref, sem_in, sem_out, *, tm, n_tiles):
    i = pl.program_id(0)
    slot = i % 2

    def row_in(tok_smem, r, s):
        return pltpu.make_async_copy(xa_ref.at[pl.ds(tok_smem[0, 0, r], 1)], buf_ref.at[s, pl.ds(r, 1)],
                                     sem_in.at[s])

    def row_out(r, s):
        return pltpu.make_async_copy(obuf_ref.at[s, pl.ds(r, 1)], out_ref.at[pl.ds(tok_ref[0, 0, r], 1)],
                                     sem_out.at[s])

    def start_gather(tok_smem, s):
        def body(r, carry):
            row_in(tok_smem, r, s).start()
            return carry
        lax.fori_loop(0, tm, body, 0, unroll=8)

    def wait_scatter(tile, s):
        n = nval_ref[tile]

        def chunk(c, carry):
            pltpu.make_async_copy(obuf_ref.at[s, pl.ds(0, WAIT_CHUNK)], out_ref.at[pl.ds(0, WAIT_CHUNK)],
                                  sem_out.at[s]).wait()
            return carry

        def row(r, carry):
            row_out(0, s).wait()
            return carry
        lax.fori_loop(0, jnp.right_shift(n, WAIT_CHUNK.bit_length() - 1), chunk, 0)
        lax.fori_loop(0, n & (WAIT_CHUNK - 1), row, 0)

    @pl.when(i == 0)
    def _():
        start_gather(tok_ref, 0)

    nxt = jnp.minimum(i + 1, n_tiles - 1)
    tile_in = pltpu.make_async_copy(xa_ref.at[pl.ds(0, tm)], buf_ref.at[slot], sem_in.at[slot])
    next_in = pltpu.make_async_copy(xa_ref.at[pl.ds(0, tm)], buf_ref.at[1 - slot], sem_in.at[1 - slot])
    n_seg = 2 * EXPERTS_PER_GROUP

    def prefetch_segment(seg):
        for r in range(seg * tm // n_seg, (seg + 1) * tm // n_seg):
            row_in(tok_next_ref, r, 1 - slot).start()

    @pl.when(i >= 2)
    def _():
        wait_scatter(i - 2, slot)

    @pl.when(nval_ref[i] > 0)
    def _():
        tile_in.wait()
        x = buf_ref[slot, :, 0:D_MODEL]
        meta = buf_ref[slot, :, D_MODEL:ROW_WIDTH]
        inv = lax.rsqrt(jnp.mean(x * x, axis=-1, keepdims=True) + RMS_EPS)
        h = (x * inv * g_ref[...]).astype(BF16)
        hids = []
        for e in range(EXPERTS_PER_GROUP):
            a = jnp.dot(h, wg_ref[e], preferred_element_type=F32)
            prefetch_segment(2 * e)
            up = jnp.dot(h, wu_ref[e], preferred_element_type=F32)
            prefetch_segment(2 * e + 1)
            hids.append(a * _sigmoid(a) * up * meta[:, e:e + 1])
        hid = jnp.concatenate(hids, axis=1).astype(BF16)
        obuf_ref[slot] = x + jnp.dot(hid, wd_ref[0], preferred_element_type=F32)

        n = nval_ref[i]
        n_main = n & ~(SCATTER_UNROLL - 1)

        def chunk(c, carry):
            for u in range(SCATTER_UNROLL):
                row_out(c * SCATTER_UNROLL + u, slot).start()
            return carry

        def row(r, carry):
            row_out(n_main + r, slot).start()
            return carry
        lax.fori_loop(0, jnp.right_shift(n, SCATTER_UNROLL.bit_length() - 1), chunk, 0)
        lax.fori_loop(0, n - n_main, row, 0)

        @pl.when((i + 1 >= n_tiles) | (nval_ref[nxt] == 0))
        def _():
            next_in.wait()

    @pl.when(i == n_tiles - 1)
    def _():
        @pl.when(i >= 1)
        def _():
            wait_scatter(i - 1, 1 - slot)
        wait_scatter(i, slot)


def _moe(xa, g, tgrp, nval, tok, wg, wu, wd, tm):
    t = xa.shape[0]
    n_tiles = tok.shape[0]
    hid_w = EXPERTS_PER_GROUP * D_EXPERT
    smem_tile = lambda off: pl.BlockSpec(
        (1, 1, tm), lambda i, tg, nv: (jnp.minimum(i + off, n_tiles - 1), 0, 0), memory_space=pltpu.SMEM)
    grid_spec = pltpu.PrefetchScalarGridSpec(
        num_scalar_prefetch=2,
        grid=(n_tiles,),
        in_specs=[
            smem_tile(0), smem_tile(1),
            pl.BlockSpec(memory_space=pl.ANY),
            pl.BlockSpec((1, D_MODEL), lambda i, tg, nv: (0, 0)),
            pl.BlockSpec((EXPERTS_PER_GROUP, D_MODEL, D_EXPERT), lambda i, tg, nv: (tg[i], 0, 0)),
            pl.BlockSpec((EXPERTS_PER_GROUP, D_MODEL, D_EXPERT), lambda i, tg, nv: (tg[i], 0, 0)),
            pl.BlockSpec((1, hid_w, D_MODEL), lambda i, tg, nv: (tg[i], 0, 0)),
        ],
        out_specs=pl.BlockSpec(memory_space=pl.ANY),
        scratch_shapes=[pltpu.VMEM((2, tm, ROW_WIDTH), F32), pltpu.VMEM((2, tm, D_MODEL), F32),
                        pltpu.SemaphoreType.DMA((2,)), pltpu.SemaphoreType.DMA((2,))],
    )
    return pl.pallas_call(
        functools.partial(_moe_kernel, tm=tm, n_tiles=n_tiles),
        grid_spec=grid_spec,
        out_shape=jax.ShapeDtypeStruct((t, D_MODEL), F32),
        compiler_params=_cparams(("arbitrary",)),
        name="moe",
    )(tgrp, nval, tok, tok, xa, g, wg, wu, wd)


MOE_TM = 512
WAIT_CHUNK = 64
SCATTER_UNROLL = 8


def _moe_plan(gid, tm):
    t = gid.shape[0]
    n_tiles = t // tm + N_EXPERT_GROUPS
    order = jnp.argsort(gid, stable=True).astype(jnp.int32)
    counts = jnp.sum((gid[:, None] == jnp.arange(N_EXPERT_GROUPS)[None, :]).astype(jnp.int32), axis=0)
    tiles_per = (counts + tm - 1) // tm
    tile_end = jnp.cumsum(tiles_per)
    row_start = jnp.cumsum(counts) - counts
    tile = jnp.arange(n_tiles, dtype=jnp.int32)
    tgrp = jnp.minimum(jnp.sum((tile[:, None] >= tile_end[None, :]).astype(jnp.int32), axis=1),
                       N_EXPERT_GROUPS - 1)
    local = tile - (tile_end - tiles_per)[tgrp]
    nval = jnp.clip(counts[tgrp] - local * tm, 0, tm)
    slot = jnp.arange(tm, dtype=jnp.int32)[None, :]
    src = row_start[tgrp][:, None] + local[:, None] * tm + jnp.where(slot < nval[:, None], slot, 0)
    tok = order[jnp.clip(src, 0, t - 1)]
    return tgrp.astype(jnp.int32), nval.astype(jnp.int32), tok.reshape(n_tiles, 1, tm)


def _rope_lane_freq():
    inv_freq = ROPE_THETA ** (-jnp.arange(0, ROPE_DIM, 2, dtype=F32) / ROPE_DIM)
    d = np.arange(LANES) % HEAD_DIM
    f = jnp.where(jnp.asarray(d < ROPE_DIM), inv_freq[np.asarray(d % (ROPE_DIM // 2))], 0.0)
    return f.reshape(1, LANES).astype(F32)


def _head_of_lane(width):
    return jnp.asarray((np.arange(width) // HEAD_DIM)[:, None] == np.arange(LANES)[None, :], dtype=BF16)


def _overlap_t(n_cmp_pad, n_blk):
    a0 = np.arange(n_cmp_pad)[:, None] * CMP_STRIDE
    b0 = np.arange(n_blk)[None, :] * SEL_BLOCK
    ov = np.clip(np.minimum(a0 + CMP_BLOCK, b0 + SEL_BLOCK) - np.maximum(a0, b0), 0, None) / CMP_BLOCK
    return jnp.asarray(ov.T, dtype=BF16)


def _bias_place(n_blk):
    return jnp.asarray(np.arange(n_blk)[:, None] + BLK_LANE0 == np.arange(LANES)[None, :], dtype=BF16)


def _layer(x, positions, attn_norm_g, w_in, q_norm_g, k_norm_cmp_g, k_norm_slc_g, k_norm_swa_g,
           cmp_pos_emb_k, cmp_w1_k, cmp_w2_k, cmp_pos_emb_v, cmp_w1_v, cmp_w2_v, w_head_out, w_pool,
           pool_scale, w_out, ffn_norm_g, w_router_group, b_router_group, w_router_expert,
           b_router_expert, w_expert_gate, w_expert_up, w_expert_down):
    b, s, d = x.shape
    t = b * s
    x2 = x.reshape(t, d)

    o_q, o_kv, o_g = ATTN_WIDTH, ATTN_WIDTH + 6 * KV_WIDTH, ATTN_WIDTH + 6 * KV_WIDTH + N_HEADS * N_BRANCH
    o_p = o_g + POOL_WIDTH
    used = 2 * ATTN_WIDTH + 2 * D_MODEL + 6 * KV_WIDTH + N_HEADS * N_BRANCH
    w_packed = jnp.concatenate(
        [w_in[:, :o_q], w_in[:, o_g:o_p], w_in[:, o_p:], w_in[:, o_q:o_kv], w_in[:, o_kv:o_g],
         jnp.zeros((d, PROJ_WIDTH - used), w_in.dtype)], axis=1).astype(BF16)
    proj = _proj(x2, attn_norm_g.reshape(1, d), w_packed)

    f_lane = _rope_lane_freq()
    tile_h = lambda g_, n: jnp.tile(g_, n).reshape(1, n * HEAD_DIM)
    q, ks, vs, kw, vw, kc_raw, vc_raw, gates = _prep(
        proj, positions.reshape(t, 1), f_lane, tile_h(q_norm_g, N_HEADS), tile_h(k_norm_slc_g, N_KV),
        tile_h(k_norm_swa_g, N_KV), _head_of_lane(ATTN_WIDTH), _head_of_lane(KV_WIDTH), b, s)

    n_row = s // CMP_STRIDE
    half = CMP_STRIDE * HEAD_DIM
    xk = kc_raw.reshape(b, N_KV, n_row, half)
    xv = vc_raw.reshape(b, N_KV, n_row, half)
    pad2 = lambda w2: jnp.pad(w2, ((0, 0), (0, LANES - HEAD_DIM))).astype(BF16)
    cmp_end = np.minimum(np.arange(n_row) * CMP_STRIDE + CMP_BLOCK - 1, s - 1)
    posc = positions[:, cmp_end].reshape(b, n_row, 1)
    gk = jnp.pad(k_norm_cmp_g, (0, LANES - HEAD_DIM)).reshape(1, LANES)
    kc, vc = _compress(xk, xv, cmp_pos_emb_k.reshape(2, half), cmp_pos_emb_v.reshape(2, half),
                       cmp_w1_k.astype(BF16), pad2(cmp_w2_k), cmp_w1_v.astype(BF16), pad2(cmp_w2_v),
                       gk, posc, f_lane)

    n_blk = s // SEL_BLOCK
    assert n_blk <= LANES - BLK_LANE0
    o = _attn(q, kc, vc, ks, vs, kw, vw, gates, _overlap_t(n_row, n_blk), _bias_place(n_blk))

    wr = jnp.concatenate([w_router_group, w_router_expert,
                          jnp.zeros((d, LANES - N_EXPERT_GROUPS - N_EXPERTS), F32)], axis=1)
    br = jnp.concatenate([b_router_group, b_router_expert,
                          jnp.zeros((LANES - N_EXPERT_GROUPS - N_EXPERTS,), F32)]).reshape(1, LANES)
    gf = ffn_norm_g.reshape(1, d)
    wr_hi = wr.astype(BF16)
    wr2 = jnp.stack([wr_hi, (wr - wr_hi.astype(F32)).astype(BF16)])
    xa = _mix(x2, proj, o, w_head_out.astype(BF16), w_pool.astype(BF16), pool_scale.reshape(1, d),
              w_out.astype(BF16), gf, wr2, br, s)

    tgrp, nval, tok = _moe_plan(xa[:, D_MODEL + META_GROUP_LANE].astype(jnp.int32), MOE_TM)
    wd = w_expert_down.reshape(N_EXPERT_GROUPS, EXPERTS_PER_GROUP * D_EXPERT, d)
    x_out = _moe(xa, gf, tgrp, nval, tok, w_expert_gate.astype(BF16), w_expert_up.astype(BF16),
                 wd.astype(BF16), MOE_TM)
    return x_out.reshape(b, s, d)


def kernel(x, positions, attn_norm_g, w_in, q_norm_g, k_norm_cmp_g, k_norm_slc_g, k_norm_swa_g,
           cmp_pos_emb_k, cmp_w1_k, cmp_w2_k, cmp_pos_emb_v, cmp_w1_v, cmp_w2_v, w_head_out, w_pool,
           pool_scale, w_out, ffn_norm_g, w_router_group, b_router_group, w_router_expert,
           b_router_expert, w_expert_gate, w_expert_up, w_expert_down):
    params = (attn_norm_g, w_in, q_norm_g, k_norm_cmp_g, k_norm_slc_g, k_norm_swa_g, cmp_pos_emb_k,
              cmp_w1_k, cmp_w2_k, cmp_pos_emb_v, cmp_w1_v, cmp_w2_v, w_head_out, w_pool, pool_scale,
              w_out, ffn_norm_g, w_router_group, b_router_group, w_router_expert, b_router_expert,
              w_expert_gate, w_expert_up, w_expert_down)
    for layer in range(attn_norm_g.shape[0]):
        x = _layer(x, positions, *[p[layer] for p in params])
    return x
```

```python
import functools

import jax
import jax.numpy as jnp
import numpy as np
from jax import lax
from jax.experimental import pallas as pl
from jax.experimental.pallas import tpu as pltpu

F32 = jnp.float32
BF16 = jnp.bfloat16

D_MODEL = 2048
N_HEADS = 16
HEAD_DIM = 64
N_KV = 4
HEADS_PER_KV = N_HEADS // N_KV
ATTN_WIDTH = N_HEADS * HEAD_DIM
KV_WIDTH = N_KV * HEAD_DIM
N_BRANCH = 3
CMP_BLOCK = 32
CMP_STRIDE = 16
CMP_HIDDEN = 4 * HEAD_DIM
SEL_BLOCK = 64
N_SEL = 16
N_FORCED_LOCAL = 2
WINDOW = 512
FORCE_BONUS = 1e4
NEG_INF = -1e30
ROPE_THETA = 500000.0
ROPE_DIM = HEAD_DIM // 4
HEAD_OUT_WIDTH = D_MODEL // N_HEADS
POOL_WINDOWS = (2, 4, 8, 16)
POOL_GROUPS = 4
POOL_WIDTH = 1024
POOL_GROUP_WIDTH = POOL_WIDTH // POOL_GROUPS
POOL_OUT_WIDTH = D_MODEL // POOL_GROUPS
N_EXPERT_GROUPS = 4
EXPERTS_PER_GROUP = 4
N_EXPERTS = N_EXPERT_GROUPS * EXPERTS_PER_GROUP
D_EXPERT = 256
RMS_EPS = 1e-6

LANES = 128
Q_SCALE = HEAD_DIM ** -0.5 * float(np.log2(np.e))
BLK_LANE0 = HEAD_DIM
V_ONES = 16
PROJ_WIDTH = 8192
PROJ_TN = 1024
COL_Q, COL_POOL, COL_M0, COL_M1, COL_KVA, COL_KVB = 0, 1, 2, 4, 6, 7
GATE_OFF = 2 * KV_WIDTH

V7X_VMEM_BYTES = 64 * 1024 * 1024
VMEM_LIMIT = V7X_VMEM_BYTES - 8 * 1024 * 1024
PROJ_TM, PROJ_STEP_TN = 1024, 2048
PREP_TM = 512
ATTN_TQ = 512
MIX_TM = 512
MOE_TM = 512


def _cparams(sem):
    return pltpu.CompilerParams(dimension_semantics=sem, vmem_limit_bytes=VMEM_LIMIT)


def _nt_dot(a, b):
    return lax.dot_general(a, b, (((1,), (1,)), ((), ())), preferred_element_type=F32)


def _tn_dot(a, b):
    return lax.dot_general(a, b, (((0,), (0,)), ((), ())), preferred_element_type=F32)


def _split_dot(x, e):
    hi = x.astype(BF16)
    lo = (x - hi.astype(F32)).astype(BF16)
    return (jnp.dot(hi, e, preferred_element_type=F32)
            + jnp.dot(lo, e, preferred_element_type=F32))


def _proj_kernel(x_ref, g_ref, w_ref, o_ref, h_ref):
    @pl.when(pl.program_id(1) == 0)
    def _():
        x = x_ref[...]
        inv = lax.rsqrt(jnp.mean(x * x, axis=-1, keepdims=True) + RMS_EPS)
        h_ref[...] = (x * inv * g_ref[...]).astype(BF16)

    o_ref[...] = jnp.dot(h_ref[...], w_ref[...], preferred_element_type=F32)


def _proj(x2, g, w_packed, tm=PROJ_TM, tn=PROJ_STEP_TN):
    t = x2.shape[0]
    return pl.pallas_call(
        _proj_kernel,
        grid=(t // tm, PROJ_WIDTH // tn),
        in_specs=[
            pl.BlockSpec((tm, D_MODEL), lambda i, j: (i, 0)),
            pl.BlockSpec((1, D_MODEL), lambda i, j: (0, 0)),
            pl.BlockSpec((D_MODEL, tn), lambda i, j: (0, j)),
        ],
        out_specs=pl.BlockSpec((tm, tn), lambda i, j: (i, j)),
        out_shape=jax.ShapeDtypeStruct((t, PROJ_WIDTH), F32),
        scratch_shapes=[pltpu.VMEM((tm, D_MODEL), BF16)],
        compiler_params=_cparams(("parallel", "arbitrary")),
        name="proj",
    )(x2, g, w_packed)


def _rope(xn, cos, sin):
    w = xn.shape[-1]
    lane = lax.broadcasted_iota(jnp.int32, xn.shape, 1) & (HEAD_DIM - 1)
    up = pltpu.roll(xn, w - ROPE_DIM // 2, axis=1)
    dn = pltpu.roll(xn, ROPE_DIM // 2, axis=1)
    return jnp.where(lane < ROPE_DIM // 2, xn * cos - up * sin,
                     jnp.where(lane < ROPE_DIM, xn * cos + dn * sin, xn))


def _head_norm_rope(x, gain, head_of_lane, cos, sin):
    w = x.shape[-1]
    per_head = _split_dot(x * x, head_of_lane)
    hi = per_head.astype(BF16)
    lo = (per_head - hi.astype(F32)).astype(BF16)
    ssq = _nt_dot(hi, head_of_lane) + _nt_dot(lo, head_of_lane)
    xn = x * lax.rsqrt(ssq * (1.0 / HEAD_DIM) + RMS_EPS) * gain
    reps = w // LANES
    return _rope(xn, jnp.concatenate([cos] * reps, axis=1), jnp.concatenate([sin] * reps, axis=1))


def _widen_head(x, h, upper):
    col = x[:, (h // 2) * LANES:(h // 2 + 1) * LANES]
    if h % 2:
        col = pltpu.roll(col, HEAD_DIM, axis=1)
    lane = lax.broadcasted_iota(jnp.int32, col.shape, 1)
    return jnp.where(lane < HEAD_DIM, col, upper)


def _prep_kernel(q_ref, kva_ref, kvb_ref, pos_ref, f_ref, gq_ref, gs_ref, gw_ref, eq_ref, ek_ref,
                 qo_ref, kso_ref, vso_ref, kwo_ref, vwo_ref, kco_ref, vco_ref, go_ref):
    tm = q_ref.shape[0]
    ang = pos_ref[...].astype(F32) * f_ref[...]
    cos, sin = jnp.cos(ang), jnp.sin(ang)
    q = _head_norm_rope(q_ref[...], gq_ref[...], eq_ref[...], cos, sin) * Q_SCALE
    lane = lax.broadcasted_iota(jnp.int32, (tm, LANES), 1)
    t_in = pl.program_id(1) * tm + lax.broadcasted_iota(jnp.int32, (tm, LANES), 0)
    blk_onehot = (jnp.right_shift(t_in, SEL_BLOCK.bit_length() - 1) == lane - BLK_LANE0).astype(F32)
    for h in range(N_HEADS):
        qo_ref[0, h // HEADS_PER_KV, h % HEADS_PER_KV] = _widen_head(q, h, 0.0).astype(BF16)
    kva = kva_ref[...]
    kvb = kvb_ref[...]
    ks = _head_norm_rope(kva[:, 2 * KV_WIDTH:3 * KV_WIDTH], gs_ref[...], ek_ref[...], cos, sin)
    kw = _head_norm_rope(kvb[:, 0:KV_WIDTH], gw_ref[...], ek_ref[...], cos, sin)
    vs = kva[:, 3 * KV_WIDTH:4 * KV_WIDTH]
    vw = kvb[:, KV_WIDTH:2 * KV_WIDTH]
    kc = kva[:, 0:KV_WIDTH]
    vc = kva[:, KV_WIDTH:2 * KV_WIDTH]
    for g in range(N_KV):
        sl = slice(g * HEAD_DIM, (g + 1) * HEAD_DIM)
        kso_ref[0, g] = _widen_head(ks, g, blk_onehot).astype(BF16)
        kwo_ref[0, g] = _widen_head(kw, g, 0.0).astype(BF16)
        kco_ref[0, g] = kc[:, sl]
        vco_ref[0, g] = vc[:, sl]
    ones = jnp.ones((V_ONES, tm), BF16)
    for v, vo_ref in ((vs, vso_ref), (vw, vwo_ref)):
        vt = v.T.astype(BF16)
        for g in range(N_KV):
            vo_ref[0, g, 0:HEAD_DIM, :] = vt[g * HEAD_DIM:(g + 1) * HEAD_DIM]
            vo_ref[0, g, HEAD_DIM:HEAD_DIM + V_ONES, :] = ones
    go_ref[0] = jax.nn.sigmoid(kvb[:, GATE_OFF:GATE_OFF + LANES]).T


def _prep(proj, pos2, f_lane, gq, gs, gw, e_q, e_k, b, s, tm=PREP_TM):
    nt = s // tm
    row = lambda bb, i: bb * nt + i
    kv_shape = jax.ShapeDtypeStruct((b, N_KV, s, HEAD_DIM), F32)
    kv_spec = pl.BlockSpec((1, N_KV, tm, HEAD_DIM), lambda bb, i: (bb, 0, i, 0))
    kx_shape = jax.ShapeDtypeStruct((b, N_KV, s, LANES), BF16)
    kx_spec = pl.BlockSpec((1, N_KV, tm, LANES), lambda bb, i: (bb, 0, i, 0))
    vt_shape = jax.ShapeDtypeStruct((b, N_KV, HEAD_DIM + V_ONES, s), BF16)
    vt_spec = pl.BlockSpec((1, N_KV, HEAD_DIM + V_ONES, tm), lambda bb, i: (bb, 0, 0, i))
    const = lambda shape: pl.BlockSpec(shape, lambda bb, i: (0,) * len(shape))
    return pl.pallas_call(
        _prep_kernel,
        grid=(b, nt),
        in_specs=[
            pl.BlockSpec((tm, PROJ_TN), lambda bb, i: (row(bb, i), COL_Q)),
            pl.BlockSpec((tm, PROJ_TN), lambda bb, i: (row(bb, i), COL_KVA)),
            pl.BlockSpec((tm, PROJ_TN), lambda bb, i: (row(bb, i), COL_KVB)),
            pl.BlockSpec((tm, 1), lambda bb, i: (row(bb, i), 0)),
            const((1, LANES)), const((1, ATTN_WIDTH)), const((1, KV_WIDTH)), const((1, KV_WIDTH)),
            const((ATTN_WIDTH, LANES)), const((KV_WIDTH, LANES)),
        ],
        out_specs=[
            pl.BlockSpec((1, N_KV, HEADS_PER_KV, tm, LANES), lambda bb, i: (bb, 0, 0, i, 0)),
            kx_spec, vt_spec, kx_spec, vt_spec, kv_spec, kv_spec,
            pl.BlockSpec((1, LANES, tm), lambda bb, i: (bb, 0, i)),
        ],
        out_shape=[
            jax.ShapeDtypeStruct((b, N_KV, HEADS_PER_KV, s, LANES), BF16),
            kx_shape, vt_shape, kx_shape, vt_shape, kv_shape, kv_shape,
            jax.ShapeDtypeStruct((b, LANES, s), F32),
        ],
        compiler_params=_cparams(("parallel", "parallel")),
        name="prep",
    )(proj, proj, proj, pos2, f_lane, gq, gs, gw, e_q, e_k)


def _compress_one(x, pe_a, pe_b, w1_ref, w2_ref):
    half = CMP_STRIDE * HEAD_DIM
    nxt = pltpu.roll(x, x.shape[0] - 1, axis=0)
    a = (x + pe_a).astype(BF16)
    b = (nxt + pe_b).astype(BF16)
    hid = (jnp.dot(a, w1_ref[0:half, :], preferred_element_type=F32)
           + jnp.dot(b, w1_ref[half:2 * half, :], preferred_element_type=F32))
    hid = hid * jax.nn.sigmoid(hid)
    return jnp.dot(hid.astype(BF16), w2_ref[...], preferred_element_type=F32)


def _compress_kernel(xk_ref, xv_ref, pek_ref, pev_ref, w1k_ref, w2k_ref, w1v_ref, w2v_ref,
                     gk_ref, posc_ref, f_ref, kc_ref, vc_ref):
    ang = posc_ref[0].astype(F32) * f_ref[...]
    cos, sin = jnp.cos(ang), jnp.sin(ang)
    for g in range(N_KV):
        k = _compress_one(xk_ref[0, g], pek_ref[0:1, :], pek_ref[1:2, :], w1k_ref, w2k_ref)
        ssq = jnp.sum(k * k, axis=-1, keepdims=True)
        kn = k * lax.rsqrt(ssq * (1.0 / HEAD_DIM) + RMS_EPS) * gk_ref[...]
        kn = _rope(kn, cos, sin)
        kc_ref[0, g] = kn.astype(BF16)
        v = _compress_one(xv_ref[0, g], pev_ref[0:1, :], pev_ref[1:2, :], w1v_ref, w2v_ref)
        vc_ref[0, g] = v[:, 0:HEAD_DIM].astype(BF16)


def _compress(xk, xv, pek, pev, w1k, w2k, w1v, w2v, gk, posc, f_lane):
    b = xk.shape[0]
    nrow = xk.shape[2]
    half = CMP_STRIDE * HEAD_DIM
    const = lambda shape: pl.BlockSpec(shape, lambda bb: (0,) * len(shape))
    x_spec = pl.BlockSpec((1, N_KV, nrow, half), lambda bb: (bb, 0, 0, 0))
    o_spec = pl.BlockSpec((1, N_KV, nrow, HEAD_DIM), lambda bb: (bb, 0, 0, 0))
    o_shape = jax.ShapeDtypeStruct((b, N_KV, nrow, HEAD_DIM), BF16)
    k_spec = pl.BlockSpec((1, N_KV, nrow, LANES), lambda bb: (bb, 0, 0, 0))
    k_shape = jax.ShapeDtypeStruct((b, N_KV, nrow, LANES), BF16)
    return pl.pallas_call(
        _compress_kernel,
        grid=(b,),
        in_specs=[x_spec, x_spec, const((2, half)), const((2, half)),
                  const((2 * half, CMP_HIDDEN)), const((CMP_HIDDEN, LANES)),
                  const((2 * half, CMP_HIDDEN)), const((CMP_HIDDEN, LANES)),
                  const((1, LANES)), pl.BlockSpec((1, nrow, 1), lambda bb: (bb, 0, 0)), const((1, LANES))],
        out_specs=[k_spec, o_spec],
        out_shape=[k_shape, o_shape],
        compiler_params=_cparams(("parallel",)),
        name="compress",
    )(xk, xv, pek, pev, w1k, w2k, w1v, w2v, gk, posc, f_lane)


def _attn_kernel(q_ref, kc_ref, vc_ref, ks_ref, vs_ref, kw_ref, vw_ref, gate_ref, ovl_ref, place_ref,
                 o_ref, *, tq, seq):
    r = HEADS_PER_KV
    cols = r * tq
    n_cmp = kc_ref.shape[2]
    n_blk = seq // SEL_BLOCK
    i = pl.program_id(1)
    t0 = i * tq
    gates = gate_ref[0]

    kk = lax.broadcasted_iota(jnp.int32, (tq, cols), 0)
    tt = lax.broadcasted_iota(jnp.int32, (tq, cols), 1) & (tq - 1)
    causal_bias = jnp.where(kk <= tt, 0.0, NEG_INF)
    band_bias = jnp.where(kk > tt, 0.0, NEG_INF)

    q_plain, q_sel, o_cmps = [], [], []

    for g in range(N_KV):
        q3 = q_ref[0, g]
        qg = q3.reshape(cols, LANES)

        s = _nt_dot(kc_ref[0, g], qg)
        n_idx = lax.broadcasted_iota(jnp.int32, (n_cmp, cols), 0)
        t_col = t0 + (lax.broadcasted_iota(jnp.int32, (n_cmp, cols), 1) & (tq - 1))
        mask = (n_idx * CMP_STRIDE + (CMP_BLOCK - 1)) <= t_col
        s = jnp.where(mask, s, NEG_INF)
        p = jnp.where(mask, jnp.exp2(s - jnp.max(s, axis=0, keepdims=True)), 0.0)
        l = jnp.sum(p, axis=0, keepdims=True)
        p = p * (1.0 / jnp.where(l > 0.0, l, 1.0))
        o_cmp = _tn_dot(vc_ref[0, g], p.astype(BF16))

        p_sum = p[:, 0:tq] + p[:, tq:2 * tq] + p[:, 2 * tq:3 * tq] + p[:, 3 * tq:4 * tq]
        hi = p_sum.astype(BF16)
        lo = (p_sum - hi.astype(F32)).astype(BF16)
        imp = (jnp.dot(ovl_ref[...], hi, preferred_element_type=F32)
               + jnp.dot(ovl_ref[...], lo, preferred_element_type=F32))
        j_idx = lax.broadcasted_iota(jnp.int32, (n_blk, tq), 0)
        t_idx = t0 + lax.broadcasted_iota(jnp.int32, (n_blk, tq), 1)
        dist = jnp.right_shift(t_idx, SEL_BLOCK.bit_length() - 1) - j_idx
        valid = dist >= 0
        forced = (j_idx == 0) | (valid & (dist < N_FORCED_LOCAL))
        score = jnp.where(valid, imp + FORCE_BONUS * forced.astype(F32), NEG_INF)
        cnt = jnp.zeros((n_blk, tq), F32)
        for jp in range(n_blk):
            row = score[jp:jp + 1, :]
            beats = (row > score) | ((row == score) & (j_idx > jp))
            cnt = cnt + beats.astype(F32)
        selected = (cnt < float(min(N_SEL, n_blk))) & valid
        blk_bias = jnp.where(selected, 0.0, NEG_INF).astype(BF16)
        q_bias = _tn_dot(blk_bias, place_ref[...]).astype(BF16)
        q_plain.append(qg)
        q_sel.append((q3 + q_bias[None]).reshape(cols, LANES))
        o_cmps.append(o_cmp)

    def sel_tile(g, k0, carry, bias):
        m_i, acc = carry
        s = _nt_dot(ks_ref[0, g, pl.ds(k0, tq), :], q_sel[g])
        if bias is not None:
            s = s + bias
        m_new = jnp.maximum(m_i, jnp.max(s, axis=0, keepdims=True))
        p = jnp.exp2((s - m_new).astype(BF16))
        acc = jnp.exp2(m_i - m_new) * acc + jnp.dot(vs_ref[0, g, :, pl.ds(k0, tq)], p,
                                                    preferred_element_type=F32)
        return m_new, acc

    def sel_body(kt, carries):
        k0 = pl.multiple_of(kt * tq, tq)
        return tuple(sel_tile(g, k0, carries[g], None) for g in range(N_KV))

    init = (jnp.full((1, cols), NEG_INF, F32), jnp.zeros((HEAD_DIM + V_ONES, cols), F32))
    carries = lax.fori_loop(0, i, sel_body, (init,) * N_KV)

    n_old = WINDOW // tq
    own = pl.multiple_of(t0, tq)
    oldest_bias = band_bias + jnp.where(t0 < WINDOW, NEG_INF, 0.0)
    for g in range(N_KV):
        starts = [t0 + (j - n_old) * tq for j in range(n_old + 1)]
        k0s = [pl.multiple_of(jnp.maximum(st, 0), tq) for st in starts]
        k_all = jnp.concatenate([ks_ref[0, g, pl.ds(own, tq), :]]
                                + [kw_ref[0, g, pl.ds(k0, tq), :] for k0 in k0s], axis=0)
        s_all = _nt_dot(k_all, q_sel[g])

        m_i, acc = carries[g]
        s = s_all[0:tq] + causal_bias
        m_new = jnp.maximum(m_i, jnp.max(s, axis=0, keepdims=True))
        acc_s = jnp.exp2(m_i - m_new) * acc + jnp.dot(vs_ref[0, g, :, pl.ds(own, tq)],
                                                      jnp.exp2((s - m_new).astype(BF16)),
                                                      preferred_element_type=F32)
        o_slc = acc_s[0:HEAD_DIM] * (1.0 / acc_s[HEAD_DIM:HEAD_DIM + 1])

        s_parts, v_parts = [], []
        for j in range(n_old + 1):
            s = s_all[(j + 1) * tq:(j + 2) * tq]
            if j == 0:
                s = s + oldest_bias
            elif j < n_old:
                s = s + jnp.where(starts[j] < 0, NEG_INF, 0.0)
            else:
                s = s + causal_bias
            s_parts.append(s)
            v_parts.append(vw_ref[0, g, :, pl.ds(k0s[j], tq)])
        m = functools.reduce(jnp.maximum, [jnp.max(s, axis=0, keepdims=True) for s in s_parts])
        acc_w = sum(jnp.dot(v, jnp.exp2((s - m).astype(BF16)), preferred_element_type=F32)
                    for s, v in zip(s_parts, v_parts))
        o_swa = acc_w[0:HEAD_DIM] * (1.0 / acc_w[HEAD_DIM:HEAD_DIM + 1])

        def gate_row(c):
            parts = [gates[(g * r + hh) * N_BRANCH + c:(g * r + hh) * N_BRANCH + c + 1, :] for hh in range(r)]
            return jnp.concatenate(parts, axis=1)

        o = gate_row(0) * o_cmps[g] + gate_row(1) * o_slc + gate_row(2) * o_swa
        for hh in range(r):
            o_ref[0, g * r + hh] = o[:, hh * tq:(hh + 1) * tq].astype(BF16)


def _attn(q, kc, vc, ks, vs, kw, vw, gates, ovl_t, place, tq=ATTN_TQ):
    b, _, _, s, _ = q.shape
    n_cmp = kc.shape[2]
    nq = s // tq
    assert WINDOW % tq == 0 and tq % SEL_BLOCK == 0
    full = lambda shape: pl.BlockSpec((1,) + shape, lambda bb, i: (bb,) + (0,) * len(shape))
    const = lambda shape: pl.BlockSpec(shape, lambda bb, i: (0,) * len(shape))
    return pl.pallas_call(
        functools.partial(_attn_kernel, tq=tq, seq=s),
        grid=(b, nq),
        in_specs=[
            pl.BlockSpec((1, N_KV, HEADS_PER_KV, tq, LANES), lambda bb, i: (bb, 0, 0, i, 0)),
            full((N_KV, n_cmp, LANES)), full((N_KV, n_cmp, HEAD_DIM)),
            full((N_KV, s, LANES)), full((N_KV, HEAD_DIM + V_ONES, s)),
            full((N_KV, s, LANES)), full((N_KV, HEAD_DIM + V_ONES, s)),
            pl.BlockSpec((1, LANES, tq), lambda bb, i: (bb, 0, i)),
            const(ovl_t.shape), const(place.shape),
        ],
        out_specs=pl.BlockSpec((1, N_HEADS, HEAD_DIM, tq), lambda bb, i: (bb, 0, 0, i)),
        out_shape=jax.ShapeDtypeStruct((b, N_HEADS, HEAD_DIM, s), BF16),
        compiler_params=_cparams(("parallel", "arbitrary")),
        name="attn",
    )(q, kc, vc, ks, vs, kw, vw, gates, ovl_t, place)


POOL_HALO = 16


def _sigmoid(x):
    return 0.5 * jnp.tanh(0.5 * x) + 0.5


def _mix_kernel(x_ref, m0_ref, m1_ref, u_ref, halo_ref, o_ref, wh_ref, wp_ref, ps_ref, wo_ref,
                gf_ref, wr_ref, br_ref, y_ref, *, tm, seq):
    it = pl.program_id(0) % (seq // tm)
    u = u_ref[...]
    halo = jnp.where(it == 0, 0.0, halo_ref[...])
    ext = jnp.concatenate([halo, u], axis=0)
    sums = []
    acc = ext
    for sh in (1, 2, 4, 8):
        acc = acc + pltpu.roll(acc, sh, axis=0)
        sums.append(acc)
    t_in = it * tm + lax.broadcasted_iota(jnp.int32, (tm, 1), 0)
    pool_parts = []
    for gi, win in enumerate(POOL_WINDOWS):
        sl = slice(gi * POOL_GROUP_WIDTH, (gi + 1) * POOL_GROUP_WIDTH)
        cnt = jnp.minimum(t_in + 1, win).astype(F32)
        pooled = sums[gi][POOL_HALO:, sl] / cnt - u[:, sl]
        pool_parts.append(jnp.dot(pooled.astype(BF16), wp_ref[gi], preferred_element_type=F32))
    pool_branch = jnp.concatenate(pool_parts, axis=1) * ps_ref[...]
    attn_branch = jnp.concatenate(
        [_tn_dot(o_ref[0, h], wh_ref[h]) for h in range(N_HEADS)], axis=1)
    y = _sigmoid(m0_ref[...]) * attn_branch + _sigmoid(m1_ref[...]) * pool_branch
    x1 = x_ref[...] + jnp.dot(y.astype(BF16), wo_ref[...], preferred_element_type=F32)
    y_ref[:, 0:D_MODEL] = x1
    inv = lax.rsqrt(jnp.mean(x1 * x1, axis=-1, keepdims=True) + RMS_EPS)
    ht = x1 * inv * gf_ref[...]
    h_hi = ht.astype(BF16)
    h_lo = (ht - h_hi.astype(F32)).astype(BF16)
    logits = (jnp.dot(h_hi, wr_ref[0], preferred_element_type=F32)
              + jnp.dot(h_lo, wr_ref[0], preferred_element_type=F32)
              + jnp.dot(h_hi, wr_ref[1], preferred_element_type=F32)) + br_ref[...]
    y_ref[:, D_MODEL:D_MODEL + LANES] = _route(logits)


def _mix(x2, proj, o, wh, wp, ps, wo, gf, wr, br, s, tm=MIX_TM):
    t = x2.shape[0]
    nt = s // tm
    hb = tm // POOL_HALO
    const = lambda shape: pl.BlockSpec(shape, lambda i: (0,) * len(shape))
    return pl.pallas_call(
        functools.partial(_mix_kernel, tm=tm, seq=s),
        grid=(t // tm,),
        in_specs=[
            pl.BlockSpec((tm, D_MODEL), lambda i: (i, 0)),
            pl.BlockSpec((tm, D_MODEL), lambda i: (i, COL_M0 // 2)),
            pl.BlockSpec((tm, D_MODEL), lambda i: (i, COL_M1 // 2)),
            pl.BlockSpec((tm, POOL_WIDTH), lambda i: (i, COL_POOL)),
            pl.BlockSpec((POOL_HALO, POOL_WIDTH), lambda i: (jnp.maximum(i * hb - 1, 0), COL_POOL)),
            pl.BlockSpec((1, N_HEADS, HEAD_DIM, tm), lambda i: (i // nt, 0, 0, i % nt)),
            const(wh.shape), const(wp.shape), const(ps.shape), const(wo.shape),
            const(gf.shape), const(wr.shape), const(br.shape),
        ],
        out_specs=pl.BlockSpec((tm, ROW_WIDTH), lambda i: (i, 0)),
        out_shape=jax.ShapeDtypeStruct((t, ROW_WIDTH), F32),
        compiler_params=_cparams(("parallel",)),
        name="mix",
    )(x2, proj, proj, proj, proj, o, wh, wp, ps, wo, gf, wr, br)


ROUTE_EXPERT_OFF = N_EXPERT_GROUPS
ROW_WIDTH = D_MODEL + LANES
META_GROUP_LANE = EXPERTS_PER_GROUP


def _route(logits):
    lane = lax.broadcasted_iota(jnp.int32, logits.shape, 1)
    is_g = lane < N_EXPERT_GROUPS
    gmax = jnp.max(jnp.where(is_g, logits, -jnp.inf), axis=-1, keepdims=True)
    ge = jnp.where(is_g, jnp.exp(logits - gmax), 0.0)
    gprob = ge / jnp.sum(ge, axis=-1, keepdims=True)
    p_group = jnp.max(jnp.where(is_g, gprob, -jnp.inf), axis=-1, keepdims=True)
    g_sel = jnp.min(jnp.where(is_g & (gprob == p_group), lane, LANES), axis=-1, keepdims=True)
    e_lane = lane - ROUTE_EXPERT_OFF
    in_grp = ((e_lane >= 0) & (e_lane < N_EXPERTS)
              & (jnp.right_shift(e_lane, EXPERTS_PER_GROUP.bit_length() - 1) == g_sel))
    el = jnp.where(in_grp, logits, -jnp.inf)
    v1 = jnp.max(el, axis=-1, keepdims=True)
    i1 = jnp.min(jnp.where(in_grp & (el == v1), lane, LANES), axis=-1, keepdims=True)
    el2 = jnp.where(lane == i1, -jnp.inf, el)
    v2 = jnp.max(el2, axis=-1, keepdims=True)
    i2 = jnp.min(jnp.where(in_grp & (lane != i1) & (el2 == v2), lane, LANES), axis=-1, keepdims=True)
    e2 = jnp.exp(v2 - v1)
    den = 1.0 + e2
    first = ROUTE_EXPERT_OFF + EXPERTS_PER_GROUP * g_sel
    return (jnp.where(lane == i1 - first, (1.0 / den) * p_group, 0.0)
            + jnp.where(lane == i2 - first, (e2 / den) * p_group, 0.0)
            + jnp.where(lane == META_GROUP_LANE, g_sel.astype(F32), 0.0))


def _moe_kernel(tgrp_ref, nval_ref, tok_ref, tok_next_ref, xa_ref, g_ref, wg_ref, wu_ref, wd_ref, out_ref,
                buf_ref, obuf_ref, sem_in, sem_out, *, tm, n_tiles):
    i = pl.program_id(0)
    slot = i % 2

    def row_in(tok_smem, r, s):
        return pltpu.make_async_copy(xa_ref.at[pl.ds(tok_smem[0, 0, r], 1)], buf_ref.at[s, pl.ds(r, 1)],
                                     sem_in.at[s])

    def row_out(r, s):
        return pltpu.make_async_copy(obuf_ref.at[s, pl.ds(r, 1)], out_ref.at[pl.ds(tok_ref[0, 0, r], 1)],
                                     sem_out.at[s])

    def start_gather(tok_smem, s):
        def body(r, carry):
            row_in(tok_smem, r, s).start()
            return carry
        lax.fori_loop(0, tm, body, 0, unroll=8)

    def wait_scatter(tile, s):
        n = nval_ref[tile]

        def chunk(c, carry):
            pltpu.make_async_copy(obuf_ref.at[s, pl.ds(0, WAIT_CHUNK)], out_ref.at[pl.ds(0, WAIT_CHUNK)],
                                  sem_out.at[s]).wait()
            return carry

        def row(r, carry):
            row_out(0, s).wait()
            return carry
        lax.fori_loop(0, jnp.right_shift(n, WAIT_CHUNK.bit_length() - 1), chunk, 0)
        lax.fori_loop(0, n & (WAIT_CHUNK - 1), row, 0)

    @pl.when(i == 0)
    def _():
        start_gather(tok_ref, 0)

    nxt = jnp.minimum(i + 1, n_tiles - 1)
    tile_in = pltpu.make_async_copy(xa_ref.at[pl.ds(0, tm)], buf_ref.at[slot], sem_in.at[slot])
    next_in = pltpu.make_async_copy(xa_ref.at[pl.ds(0, tm)], buf_ref.at[1 - slot], sem_in.at[1 - slot])
    n_seg = 2 * EXPERTS_PER_GROUP

    def prefetch_segment(seg):
        for r in range(seg * tm // n_seg, (seg + 1) * tm // n_seg):
            row_in(tok_next_ref, r, 1 - slot).start()

    @pl.when(i >= 2)
    def _():
        wait_scatter(i - 2, slot)

    @pl.when(nval_ref[i] > 0)
    def _():
        tile_in.wait()
        x = buf_ref[slot, :, 0:D_MODEL]
        meta = buf_ref[slot, :, D_MODEL:ROW_WIDTH]
        inv = lax.rsqrt(jnp.mean(x * x, axis=-1, keepdims=True) + RMS_EPS)
        h = (x * inv * g_ref[...]).astype(BF16)
        hids = []
        for e in range(EXPERTS_PER_GROUP):
            a = jnp.dot(h, wg_ref[e], preferred_element_type=F32)
            prefetch_segment(2 * e)
            up = jnp.dot(h, wu_ref[e], preferred_element_type=F32)
            prefetch_segment(2 * e + 1)
            hids.append(a * _sigmoid(a) * up * meta[:, e:e + 1])
        hid = jnp.concatenate(hids, axis=1).astype(BF16)
        obuf_ref[slot] = x + jnp.dot(hid, wd_ref[0], preferred_element_type=F32)

        n = nval_ref[i]
        n_main = n & ~(SCATTER_UNROLL - 1)

        def chunk(c, carry):
            for u in range(SCATTER_UNROLL):
                row_out(c * SCATTER_UNROLL + u, slot).start()
            return carry

        def row(r, carry):
            row_out(n_main + r, slot).start()
            return carry
        lax.fori_loop(0, jnp.right_shift(n, SCATTER_UNROLL.bit_length() - 1), chunk, 0)
        lax.fori_loop(0, n - n_main, row, 0)

        @pl.when((i + 1 >= n_tiles) | (nval_ref[nxt] == 0))
        def _():
            next_in.wait()

    @pl.when(i == n_tiles - 1)
    def _():
        @pl.when(i >= 1)
        def _():
            wait_scatter(i - 1, 1 - slot)
        wait_scatter(i, slot)


def _moe(xa, g, tgrp, nval, tok, wg, wu, wd, tm):
    t = xa.shape[0]
    n_tiles = tok.shape[0]
    hid_w = EXPERTS_PER_GROUP * D_EXPERT
    smem_tile = lambda off: pl.BlockSpec(
        (1, 1, tm), lambda i, tg, nv: (jnp.minimum(i + off, n_tiles - 1), 0, 0), memory_space=pltpu.SMEM)
    grid_spec = pltpu.PrefetchScalarGridSpec(
        num_scalar_prefetch=2,
        grid=(n_tiles,),
        in_specs=[
            smem_tile(0), smem_tile(1),
            pl.BlockSpec(memory_space=pl.ANY),
            pl.BlockSpec((1, D_MODEL), lambda i, tg, nv: (0, 0)),
            pl.BlockSpec((EXPERTS_PER_GROUP, D_MODEL, D_EXPERT), lambda i, tg, nv: (tg[i], 0, 0)),
            pl.BlockSpec((EXPERTS_PER_GROUP, D_MODEL, D_EXPERT), lambda i, tg, nv: (tg[i], 0, 0)),
            pl.BlockSpec((1, hid_w, D_MODEL), lambda i, tg, nv: (tg[i], 0, 0)),
        ],
        out_specs=pl.BlockSpec(memory_space=pl.ANY),
        scratch_shapes=[pltpu.VMEM((2, tm, ROW_WIDTH), F32), pltpu.VMEM((2, tm, D_MODEL), F32),
                        pltpu.SemaphoreType.DMA((2,)), pltpu.SemaphoreType.DMA((2,))],
    )
    return pl.pallas_call(
        functools.partial(_moe_kernel, tm=tm, n_tiles=n_tiles),
        grid_spec=grid_spec,
        out_shape=jax.ShapeDtypeStruct((t, D_MODEL), F32),
        compiler_params=_cparams(("arbitrary",)),
        name="moe",
    )(tgrp, nval, tok, tok, xa, g, wg, wu, wd)


WAIT_CHUNK = 64
SCATTER_UNROLL = 8


def _moe_plan(gid, tm):
    t = gid.shape[0]
    n_tiles = t // tm + N_EXPERT_GROUPS
    order = jnp.argsort(gid, stable=True).astype(jnp.int32)
    counts = jnp.sum((gid[:, None] == jnp.arange(N_EXPERT_GROUPS)[None, :]).astype(jnp.int32), axis=0)
    tiles_per = (counts + tm - 1) // tm
    tile_end = jnp.cumsum(tiles_per)
    row_start = jnp.cumsum(counts) - counts
    tile = jnp.arange(n_tiles, dtype=jnp.int32)
    tgrp = jnp.minimum(jnp.sum((tile[:, None] >= tile_end[None, :]).astype(jnp.int32), axis=1),
                       N_EXPERT_GROUPS - 1)
    local = tile - (tile_end - tiles_per)[tgrp]
    nval = jnp.clip(counts[tgrp] - local * tm, 0, tm)
    slot = jnp.arange(tm, dtype=jnp.int32)[None, :]
    src = row_start[tgrp][:, None] + local[:, None] * tm + jnp.where(slot < nval[:, None], slot, 0)
    tok = order[jnp.clip(src, 0, t - 1)]
    return tgrp.astype(jnp.int32), nval.astype(jnp.int32), tok.reshape(n_tiles, 1, tm)


def _rope_lane_freq():
    inv_freq = ROPE_THETA ** (-jnp.arange(0, ROPE_DIM, 2, dtype=F32) / ROPE_DIM)
    d = np.arange(LANES) % HEAD_DIM
    f = jnp.where(jnp.asarray(d < ROPE_DIM), inv_freq[np.asarray(d % (ROPE_DIM // 2))], 0.0)
    return f.reshape(1, LANES).astype(F32)


def _head_of_lane(width):
    return jnp.asarray((np.arange(width) // HEAD_DIM)[:, None] == np.arange(LANES)[None, :], dtype=BF16)


def _overlap_t(n_cmp_pad, n_blk):
    a0 = np.arange(n_cmp_pad)[:, None] * CMP_STRIDE
    b0 = np.arange(n_blk)[None, :] * SEL_BLOCK
    ov = np.clip(np.minimum(a0 + CMP_BLOCK, b0 + SEL_BLOCK) - np.maximum(a0, b0), 0, None) / CMP_BLOCK
    return jnp.asarray(ov.T, dtype=BF16)


def _bias_place(n_blk):
    return jnp.asarray(np.arange(n_blk)[:, None] + BLK_LANE0 == np.arange(LANES)[None, :], dtype=BF16)


def _layer(x, positions, attn_norm_g, w_in, q_norm_g, k_norm_cmp_g, k_norm_slc_g, k_norm_swa_g,
           cmp_pos_emb_k, cmp_w1_k, cmp_w2_k, cmp_pos_emb_v, cmp_w1_v, cmp_w2_v, w_head_out, w_pool,
           pool_scale, w_out, ffn_norm_g, w_router_group, b_router_group, w_router_expert,
           b_router_expert, w_expert_gate, w_expert_up, w_expert_down):
    b, s, d = x.shape
    t = b * s
    x2 = x.reshape(t, d)

    o_q, o_kv, o_g = ATTN_WIDTH, ATTN_WIDTH + 6 * KV_WIDTH, ATTN_WIDTH + 6 * KV_WIDTH + N_HEADS * N_BRANCH
    o_p = o_g + POOL_WIDTH
    used = 2 * ATTN_WIDTH + 2 * D_MODEL + 6 * KV_WIDTH + N_HEADS * N_BRANCH
    w_packed = jnp.concatenate(
        [w_in[:, :o_q], w_in[:, o_g:o_p], w_in[:, o_p:], w_in[:, o_q:o_kv], w_in[:, o_kv:o_g],
         jnp.zeros((d, PROJ_WIDTH - used), w_in.dtype)], axis=1).astype(BF16)
    proj = _proj(x2, attn_norm_g.reshape(1, d), w_packed)

    f_lane = _rope_lane_freq()
    tile_h = lambda g_, n: jnp.tile(g_, n).reshape(1, n * HEAD_DIM)
    q, ks, vs, kw, vw, kc_raw, vc_raw, gates = _prep(
        proj, positions.reshape(t, 1), f_lane, tile_h(q_norm_g, N_HEADS), tile_h(k_norm_slc_g, N_KV),
        tile_h(k_norm_swa_g, N_KV), _head_of_lane(ATTN_WIDTH), _head_of_lane(KV_WIDTH), b, s)

    n_row = s // CMP_STRIDE
    half = CMP_STRIDE * HEAD_DIM
    xk = kc_raw.reshape(b, N_KV, n_row, half)
    xv = vc_raw.reshape(b, N_KV, n_row, half)
    pad2 = lambda w2: jnp.pad(w2, ((0, 0), (0, LANES - HEAD_DIM))).astype(BF16)
    cmp_end = np.minimum(np.arange(n_row) * CMP_STRIDE + CMP_BLOCK - 1, s - 1)
    posc = positions[:, cmp_end].reshape(b, n_row, 1)
    gk = jnp.pad(k_norm_cmp_g, (0, LANES - HEAD_DIM)).reshape(1, LANES)
    kc, vc = _compress(xk, xv, cmp_pos_emb_k.reshape(2, half), cmp_pos_emb_v.reshape(2, half),
                       cmp_w1_k.astype(BF16), pad2(cmp_w2_k), cmp_w1_v.astype(BF16), pad2(cmp_w2_v),
                       gk, posc, f_lane)

    n_blk = s // SEL_BLOCK
    assert n_blk <= LANES - BLK_LANE0
    o = _attn(q, kc, vc, ks, vs, kw, vw, gates, _overlap_t(n_row, n_blk), _bias_place(n_blk))

    wr = jnp.concatenate([w_router_group, w_router_expert,
                          jnp.zeros((d, LANES - N_EXPERT_GROUPS - N_EXPERTS), F32)], axis=1)
    br = jnp.concatenate([b_router_group, b_router_expert,
                          jnp.zeros((LANES - N_EXPERT_GROUPS - N_EXPERTS,), F32)]).reshape(1, LANES)
    gf = ffn_norm_g.reshape(1, d)
    wr_hi = wr.astype(BF16)
    wr2 = jnp.stack([wr_hi, (wr - wr_hi.astype(F32)).astype(BF16)])
    xa = _mix(x2, proj, o, w_head_out.astype(BF16), w_pool.astype(BF16), pool_scale.reshape(1, d),
              w_out.astype(BF16), gf, wr2, br, s)

    tgrp, nval, tok = _moe_plan(xa[:, D_MODEL + META_GROUP_LANE].astype(jnp.int32), MOE_TM)
    wd = w_expert_down.reshape(N_EXPERT_GROUPS, EXPERTS_PER_GROUP * D_EXPERT, d)
    x_out = _moe(xa, gf, tgrp, nval, tok, w_expert_gate.astype(BF16), w_expert_up.astype(BF16),
                 wd.astype(BF16), MOE_TM)
    return x_out.reshape(b, s, d)


def kernel(x, positions, attn_norm_g, w_in, q_norm_g, k_norm_cmp_g, k_norm_slc_g, k_norm_swa_g,
           cmp_pos_emb_k, cmp_w1_k, cmp_w2_k, cmp_pos_emb_v, cmp_w1_v, cmp_w2_v, w_head_out, w_pool,
           pool_scale, w_out, ffn_norm_g, w_router_group, b_router_group, w_router_expert,
           b_router_expert, w_expert_gate, w_expert_up, w_expert_down):
    params = (attn_norm_g, w_in, q_norm_g, k_norm_cmp_g, k_norm_slc_g, k_norm_swa_g, cmp_pos_emb_k,
              cmp_w1_k, cmp_w2_k, cmp_pos_emb_v, cmp_w1_v, cmp_w2_v, w_head_out, w_pool, pool_scale,
              w_out, ffn_norm_g, w_router_group, b_router_group, w_router_expert, b_router_expert,
              w_expert_gate, w_expert_up, w_expert_down)
    for layer in range(attn_norm_g.shape[0]):
        x = _layer(x, positions, *[p[layer] for p in params])
    return x
```

```python
import functools

import jax
import jax.numpy as jnp
import numpy as np
from jax import lax
from jax.experimental import pallas as pl
from jax.experimental.pallas import tpu as pltpu

F32 = jnp.float32
BF16 = jnp.bfloat16

D_MODEL = 2048
N_HEADS = 16
HEAD_DIM = 64
N_KV = 4
HEADS_PER_KV = N_HEADS // N_KV
ATTN_WIDTH = N_HEADS * HEAD_DIM
KV_WIDTH = N_KV * HEAD_DIM
N_BRANCH = 3
CMP_BLOCK = 32
CMP_STRIDE = 16
CMP_HIDDEN = 4 * HEAD_DIM
SEL_BLOCK = 64
N_SEL = 16
N_FORCED_LOCAL = 2
WINDOW = 512
FORCE_BONUS = 1e4
NEG_INF = -1e30
ROPE_THETA = 500000.0
ROPE_DIM = HEAD_DIM // 4
HEAD_OUT_WIDTH = D_MODEL // N_HEADS
POOL_WINDOWS = (2, 4, 8, 16)
POOL_GROUPS = 4
POOL_WIDTH = 1024
POOL_GROUP_WIDTH = POOL_WIDTH // POOL_GROUPS
POOL_OUT_WIDTH = D_MODEL // POOL_GROUPS
N_EXPERT_GROUPS = 4
EXPERTS_PER_GROUP = 4
N_EXPERTS = N_EXPERT_GROUPS * EXPERTS_PER_GROUP
D_EXPERT = 256
RMS_EPS = 1e-6

LANES = 128
Q_SCALE = HEAD_DIM ** -0.5 * float(np.log2(np.e))
BLK_LANE0 = HEAD_DIM
V_ONES = 16
PROJ_WIDTH = 8192
PROJ_TN = 1024
COL_Q, COL_POOL, COL_M0, COL_M1, COL_KVA, COL_KVB = 0, 1, 2, 4, 6, 7
GATE_OFF = 2 * KV_WIDTH

V7X_VMEM_BYTES = 64 * 1024 * 1024
VMEM_LIMIT = V7X_VMEM_BYTES - 8 * 1024 * 1024
PROJ_TM, PROJ_STEP_TN = 1024, 2048
PREP_TM = 512
ATTN_TQ = 512
MIX_TM = 512
MOE_TM = 512


def _cparams(sem):
    return pltpu.CompilerParams(dimension_semantics=sem, vmem_limit_bytes=VMEM_LIMIT)


def _nt_dot(a, b):
    return lax.dot_general(a, b, (((1,), (1,)), ((), ())), preferred_element_type=F32)


def _tn_dot(a, b):
    return lax.dot_general(a, b, (((0,), (0,)), ((), ())), preferred_element_type=F32)


def _split_dot(x, e):
    hi = x.astype(BF16)
    lo = (x - hi.astype(F32)).astype(BF16)
    return (jnp.dot(hi, e, preferred_element_type=F32)
            + jnp.dot(lo, e, preferred_element_type=F32))


def _proj_kernel(x_ref, g_ref, w_ref, o_ref, h_ref):
    @pl.when(pl.program_id(1) == 0)
    def _():
        x = x_ref[...]
        inv = lax.rsqrt(jnp.mean(x * x, axis=-1, keepdims=True) + RMS_EPS)
        h_ref[...] = (x * inv * g_ref[...]).astype(BF16)

    o_ref[...] = jnp.dot(h_ref[...], w_ref[...], preferred_element_type=F32)


def _proj(x2, g, w_packed, tm=PROJ_TM, tn=PROJ_STEP_TN):
    t = x2.shape[0]
    return pl.pallas_call(
        _proj_kernel,
        grid=(t // tm, PROJ_WIDTH // tn),
        in_specs=[
            pl.BlockSpec((tm, D_MODEL), lambda i, j: (i, 0)),
            pl.BlockSpec((1, D_MODEL), lambda i, j: (0, 0)),
            pl.BlockSpec((D_MODEL, tn), lambda i, j: (0, j)),
        ],
        out_specs=pl.BlockSpec((tm, tn), lambda i, j: (i, j)),
        out_shape=jax.ShapeDtypeStruct((t, PROJ_WIDTH), F32),
        scratch_shapes=[pltpu.VMEM((tm, D_MODEL), BF16)],
        compiler_params=_cparams(("parallel", "arbitrary")),
        name="proj",
    )(x2, g, w_packed)


def _rope(xn, cos, sin):
    w = xn.shape[-1]
    lane = lax.broadcasted_iota(jnp.int32, xn.shape, 1) & (HEAD_DIM - 1)
    up = pltpu.roll(xn, w - ROPE_DIM // 2, axis=1)
    dn = pltpu.roll(xn, ROPE_DIM // 2, axis=1)
    return jnp.where(lane < ROPE_DIM // 2, xn * cos - up * sin,
                     jnp.where(lane < ROPE_DIM, xn * cos + dn * sin, xn))


def _head_norm_rope(x, gain, head_of_lane, cos, sin):
    w = x.shape[-1]
    per_head = _split_dot(x * x, head_of_lane)
    hi = per_head.astype(BF16)
    lo = (per_head - hi.astype(F32)).astype(BF16)
    ssq = _nt_dot(hi, head_of_lane) + _nt_dot(lo, head_of_lane)
    xn = x * lax.rsqrt(ssq * (1.0 / HEAD_DIM) + RMS_EPS) * gain
    reps = w // LANES
    return _rope(xn, jnp.concatenate([cos] * reps, axis=1), jnp.concatenate([sin] * reps, axis=1))


def _widen_head(x, h, upper):
    col = x[:, (h // 2) * LANES:(h // 2 + 1) * LANES]
    if h % 2:
        col = pltpu.roll(col, HEAD_DIM, axis=1)
    lane = lax.broadcasted_iota(jnp.int32, col.shape, 1)
    return jnp.where(lane < HEAD_DIM, col, upper)


def _prep_kernel(q_ref, kva_ref, kvb_ref, pos_ref, f_ref, gq_ref, gs_ref, gw_ref, eq_ref, ek_ref,
                 qo_ref, kso_ref, vso_ref, kwo_ref, vwo_ref, kco_ref, vco_ref, go_ref):
    tm = q_ref.shape[0]
    ang = pos_ref[...].astype(F32) * f_ref[...]
    cos, sin = jnp.cos(ang), jnp.sin(ang)
    q = _head_norm_rope(q_ref[...], gq_ref[...], eq_ref[...], cos, sin) * Q_SCALE
    lane = lax.broadcasted_iota(jnp.int32, (tm, LANES), 1)
    t_in = pl.program_id(1) * tm + lax.broadcasted_iota(jnp.int32, (tm, LANES), 0)
    blk_onehot = (jnp.right_shift(t_in, SEL_BLOCK.bit_length() - 1) == lane - BLK_LANE0).astype(F32)
    for h in range(N_HEADS):
        qo_ref[0, h // HEADS_PER_KV, h % HEADS_PER_KV] = _widen_head(q, h, 0.0).astype(BF16)
    kva = kva_ref[...]
    kvb = kvb_ref[...]
    ks = _head_norm_rope(kva[:, 2 * KV_WIDTH:3 * KV_WIDTH], gs_ref[...], ek_ref[...], cos, sin)
    kw = _head_norm_rope(kvb[:, 0:KV_WIDTH], gw_ref[...], ek_ref[...], cos, sin)
    vs = kva[:, 3 * KV_WIDTH:4 * KV_WIDTH]
    vw = kvb[:, KV_WIDTH:2 * KV_WIDTH]
    kc = kva[:, 0:KV_WIDTH]
    vc = kva[:, KV_WIDTH:2 * KV_WIDTH]
    for g in range(N_KV):
        sl = slice(g * HEAD_DIM, (g + 1) * HEAD_DIM)
        kso_ref[0, g] = _widen_head(ks, g, blk_onehot).astype(BF16)
        kwo_ref[0, g] = _widen_head(kw, g, 0.0).astype(BF16)
        kco_ref[0, g] = kc[:, sl]
        vco_ref[0, g] = vc[:, sl]
    ones = jnp.ones((V_ONES, tm), BF16)
    for v, vo_ref in ((vs, vso_ref), (vw, vwo_ref)):
        vt = v.T.astype(BF16)
        for g in range(N_KV):
            vo_ref[0, g, 0:HEAD_DIM, :] = vt[g * HEAD_DIM:(g + 1) * HEAD_DIM]
            vo_ref[0, g, HEAD_DIM:HEAD_DIM + V_ONES, :] = ones
    go_ref[0] = jax.nn.sigmoid(kvb[:, GATE_OFF:GATE_OFF + LANES]).T


def _prep(proj, pos2, f_lane, gq, gs, gw, e_q, e_k, b, s, tm=PREP_TM):
    nt = s // tm
    row = lambda bb, i: bb * nt + i
    kv_shape = jax.ShapeDtypeStruct((b, N_KV, s, HEAD_DIM), F32)
    kv_spec = pl.BlockSpec((1, N_KV, tm, HEAD_DIM), lambda bb, i: (bb, 0, i, 0))
    kx_shape = jax.ShapeDtypeStruct((b, N_KV, s, LANES), BF16)
    kx_spec = pl.BlockSpec((1, N_KV, tm, LANES), lambda bb, i: (bb, 0, i, 0))
    vt_shape = jax.ShapeDtypeStruct((b, N_KV, HEAD_DIM + V_ONES, s), BF16)
    vt_spec = pl.BlockSpec((1, N_KV, HEAD_DIM + V_ONES, tm), lambda bb, i: (bb, 0, 0, i))
    const = lambda shape: pl.BlockSpec(shape, lambda bb, i: (0,) * len(shape))
    return pl.pallas_call(
        _prep_kernel,
        grid=(b, nt),
        in_specs=[
            pl.BlockSpec((tm, PROJ_TN), lambda bb, i: (row(bb, i), COL_Q)),
            pl.BlockSpec((tm, PROJ_TN), lambda bb, i: (row(bb, i), COL_KVA)),
            pl.BlockSpec((tm, PROJ_TN), lambda bb, i: (row(bb, i), COL_KVB)),
            pl.BlockSpec((tm, 1), lambda bb, i: (row(bb, i), 0)),
            const((1, LANES)), const((1, ATTN_WIDTH)), const((1, KV_WIDTH)), const((1, KV_WIDTH)),
            const((ATTN_WIDTH, LANES)), const((KV_WIDTH, LANES)),
        ],
        out_specs=[
            pl.BlockSpec((1, N_KV, HEADS_PER_KV, tm, LANES), lambda bb, i: (bb, 0, 0, i, 0)),
            kx_spec, vt_spec, kx_spec, vt_spec, kv_spec, kv_spec,
            pl.BlockSpec((1, LANES, tm), lambda bb, i: (bb, 0, i)),
        ],
        out_shape=[
            jax.ShapeDtypeStruct((b, N_KV, HEADS_PER_KV, s, LANES), BF16),
            kx_shape, vt_shape, kx_shape, vt_shape, kv_shape, kv_shape,
            jax.ShapeDtypeStruct((b, LANES, s), F32),
        ],
        compiler_params=_cparams(("parallel", "parallel")),
        name="prep",
    )(proj, proj, proj, pos2, f_lane, gq, gs, gw, e_q, e_k)


def _compress_one(x, pe_a, pe_b, w1_ref, w2_ref):
    half = CMP_STRIDE * HEAD_DIM
    nxt = pltpu.roll(x, x.shape[0] - 1, axis=0)
    a = (x + pe_a).astype(BF16)
    b = (nxt + pe_b).astype(BF16)
    hid = (jnp.dot(a, w1_ref[0:half, :], preferred_element_type=F32)
           + jnp.dot(b, w1_ref[half:2 * half, :], preferred_element_type=F32))
    hid = hid * jax.nn.sigmoid(hid)
    return jnp.dot(hid.astype(BF16), w2_ref[...], preferred_element_type=F32)


def _compress_kernel(xk_ref, xv_ref, pek_ref, pev_ref, w1k_ref, w2k_ref, w1v_ref, w2v_ref,
                     gk_ref, posc_ref, f_ref, kc_ref, vc_ref):
    ang = posc_ref[0].astype(F32) * f_ref[...]
    cos, sin = jnp.cos(ang), jnp.sin(ang)
    for g in range(N_KV):
        k = _compress_one(xk_ref[0, g], pek_ref[0:1, :], pek_ref[1:2, :], w1k_ref, w2k_ref)
        ssq = jnp.sum(k * k, axis=-1, keepdims=True)
        kn = k * lax.rsqrt(ssq * (1.0 / HEAD_DIM) + RMS_EPS) * gk_ref[...]
        kn = _rope(kn, cos, sin)
        kc_ref[0, g] = kn.astype(BF16)
        v = _compress_one(xv_ref[0, g], pev_ref[0:1, :], pev_ref[1:2, :], w1v_ref, w2v_ref)
        vc_ref[0, g] = v[:, 0:HEAD_DIM].astype(BF16)


def _compress(xk, xv, pek, pev, w1k, w2k, w1v, w2v, gk, posc, f_lane):
    b = xk.shape[0]
    nrow = xk.shape[2]
    half = CMP_STRIDE * HEAD_DIM
    const = lambda shape: pl.BlockSpec(shape, lambda bb: (0,) * len(shape))
    x_spec = pl.BlockSpec((1, N_KV, nrow, half), lambda bb: (bb, 0, 0, 0))
    o_spec = pl.BlockSpec((1, N_KV, nrow, HEAD_DIM), lambda bb: (bb, 0, 0, 0))
    o_shape = jax.ShapeDtypeStruct((b, N_KV, nrow, HEAD_DIM), BF16)
    k_spec = pl.BlockSpec((1, N_KV, nrow, LANES), lambda bb: (bb, 0, 0, 0))
    k_shape = jax.ShapeDtypeStruct((b, N_KV, nrow, LANES), BF16)
    return pl.pallas_call(
        _compress_kernel,
        grid=(b,),
        in_specs=[x_spec, x_spec, const((2, half)), const((2, half)),
                  const((2 * half, CMP_HIDDEN)), const((CMP_HIDDEN, LANES)),
                  const((2 * half, CMP_HIDDEN)), const((CMP_HIDDEN, LANES)),
                  const((1, LANES)), pl.BlockSpec((1, nrow, 1), lambda bb: (bb, 0, 0)), const((1, LANES))],
        out_specs=[k_spec, o_spec],
        out_shape=[k_shape, o_shape],
        compiler_params=_cparams(("parallel",)),
        name="compress",
    )(xk, xv, pek, pev, w1k, w2k, w1v, w2v, gk, posc, f_lane)


def _attn_kernel(q_ref, kc_ref, vc_ref, ks_ref, vs_ref, kw_ref, vw_ref, gate_ref, ovl_ref, place_ref,
                 o_ref, *, tq, seq):
    r = HEADS_PER_KV
    cols = r * tq
    n_cmp = kc_ref.shape[2]
    n_blk = seq // SEL_BLOCK
    i = pl.program_id(1)
    t0 = i * tq
    gates = gate_ref[0]

    kk = lax.broadcasted_iota(jnp.int32, (tq, cols), 0)
    tt = lax.broadcasted_iota(jnp.int32, (tq, cols), 1) & (tq - 1)
    causal_bias = jnp.where(kk <= tt, 0.0, NEG_INF)
    band_bias = jnp.where(kk > tt, 0.0, NEG_INF)

    q_sel, o_cmps = [], []

    for g in range(N_KV):
        q3 = q_ref[0, g]
        qg = q3.reshape(cols, LANES)

        s = _nt_dot(kc_ref[0, g], qg)
        n_idx = lax.broadcasted_iota(jnp.int32, (n_cmp, cols), 0)
        t_col = t0 + (lax.broadcasted_iota(jnp.int32, (n_cmp, cols), 1) & (tq - 1))
        mask = (n_idx * CMP_STRIDE + (CMP_BLOCK - 1)) <= t_col
        s = jnp.where(mask, s, NEG_INF)
        p = jnp.where(mask, jnp.exp2(s - jnp.max(s, axis=0, keepdims=True)), 0.0)
        l = jnp.sum(p, axis=0, keepdims=True)
        p = p * (1.0 / jnp.where(l > 0.0, l, 1.0))
        o_cmp = _tn_dot(vc_ref[0, g], p.astype(BF16))

        p_sum = p[:, 0:tq] + p[:, tq:2 * tq] + p[:, 2 * tq:3 * tq] + p[:, 3 * tq:4 * tq]
        hi = p_sum.astype(BF16)
        lo = (p_sum - hi.astype(F32)).astype(BF16)
        imp = (jnp.dot(ovl_ref[...], hi, preferred_element_type=F32)
               + jnp.dot(ovl_ref[...], lo, preferred_element_type=F32))
        j_idx = lax.broadcasted_iota(jnp.int32, (n_blk, tq), 0)
        t_idx = t0 + lax.broadcasted_iota(jnp.int32, (n_blk, tq), 1)
        dist = jnp.right_shift(t_idx, SEL_BLOCK.bit_length() - 1) - j_idx
        valid = dist >= 0
        forced = (j_idx == 0) | (valid & (dist < N_FORCED_LOCAL))
        score = jnp.where(valid, imp + FORCE_BONUS * forced.astype(F32), NEG_INF)
        cnt = jnp.zeros((n_blk, tq), F32)
        for jp in range(n_blk):
            row = score[jp:jp + 1, :]
            beats = (row > score) | ((row == score) & (j_idx > jp))
            cnt = cnt + beats.astype(F32)
        selected = (cnt < float(min(N_SEL, n_blk))) & valid
        blk_bias = jnp.where(selected, 0.0, NEG_INF).astype(BF16)
        q_bias = _tn_dot(blk_bias, place_ref[...]).astype(BF16)
        q_sel.append((q3 + q_bias[None]).reshape(cols, LANES))
        o_cmps.append(o_cmp)

    def sel_tile(g, k0, carry, bias):
        m_i, acc = carry
        s = _nt_dot(ks_ref[0, g, pl.ds(k0, tq), :], q_sel[g])
        if bias is not None:
            s = s + bias
        m_new = jnp.maximum(m_i, jnp.max(s, axis=0, keepdims=True))
        p = jnp.exp2((s - m_new).astype(BF16))
        acc = jnp.exp2(m_i - m_new) * acc + jnp.dot(vs_ref[0, g, :, pl.ds(k0, tq)], p,
                                                    preferred_element_type=F32)
        return m_new, acc

    def sel_body(kt, carries):
        k0 = pl.multiple_of(kt * tq, tq)
        return tuple(sel_tile(g, k0, carries[g], None) for g in range(N_KV))

    init = (jnp.full((1, cols), NEG_INF, F32), jnp.zeros((HEAD_DIM + V_ONES, cols), F32))
    carries = lax.fori_loop(0, i, sel_body, (init,) * N_KV)

    n_old = WINDOW // tq
    own = pl.multiple_of(t0, tq)
    oldest_bias = band_bias + jnp.where(t0 < WINDOW, NEG_INF, 0.0)
    for g in range(N_KV):
        starts = [t0 + (j - n_old) * tq for j in range(n_old + 1)]
        k0s = [pl.multiple_of(jnp.maximum(st, 0), tq) for st in starts]
        k_all = jnp.concatenate([ks_ref[0, g, pl.ds(own, tq), :]]
                                + [kw_ref[0, g, pl.ds(k0, tq), :] for k0 in k0s], axis=0)
        s_all = _nt_dot(k_all, q_sel[g])

        m_i, acc = carries[g]
        s = s_all[0:tq] + causal_bias
        m_new = jnp.maximum(m_i, jnp.max(s, axis=0, keepdims=True))
        acc_s = jnp.exp2(m_i - m_new) * acc + jnp.dot(vs_ref[0, g, :, pl.ds(own, tq)],
                                                      jnp.exp2((s - m_new).astype(BF16)),
                                                      preferred_element_type=F32)
        o_slc = acc_s[0:HEAD_DIM] * (1.0 / acc_s[HEAD_DIM:HEAD_DIM + 1])

        s_parts, v_parts = [], []
        for j in range(n_old + 1):
            s = s_all[(j + 1) * tq:(j + 2) * tq]
            if j == 0:
                s = s + oldest_bias
            elif j < n_old:
                s = s + jnp.where(starts[j] < 0, NEG_INF, 0.0)
            else:
                s = s + causal_bias
            s_parts.append(s)
            v_parts.append(vw_ref[0, g, :, pl.ds(k0s[j], tq)])
        m = functools.reduce(jnp.maximum, [jnp.max(s, axis=0, keepdims=True) for s in s_parts])
        acc_w = sum(jnp.dot(v, jnp.exp2((s - m).astype(BF16)), preferred_element_type=F32)
                    for s, v in zip(s_parts, v_parts))
        o_swa = acc_w[0:HEAD_DIM] * (1.0 / acc_w[HEAD_DIM:HEAD_DIM + 1])

        def gate_row(c):
            parts = [gates[(g * r + hh) * N_BRANCH + c:(g * r + hh) * N_BRANCH + c + 1, :] for hh in range(r)]
            return jnp.concatenate(parts, axis=1)

        o = gate_row(0) * o_cmps[g] + gate_row(1) * o_slc + gate_row(2) * o_swa
        for hh in range(r):
            o_ref[0, g * r + hh] = o[:, hh * tq:(hh + 1) * tq].astype(BF16)


def _attn(q, kc, vc, ks, vs, kw, vw, gates, ovl_t, place, tq=ATTN_TQ):
    b, _, _, s, _ = q.shape
    n_cmp = kc.shape[2]
    nq = s // tq
    assert WINDOW % tq == 0 and tq % SEL_BLOCK == 0
    full = lambda shape: pl.BlockSpec((1,) + shape, lambda bb, i: (bb,) + (0,) * len(shape))
    const = lambda shape: pl.BlockSpec(shape, lambda bb, i: (0,) * len(shape))
    return pl.pallas_call(
        functools.partial(_attn_kernel, tq=tq, seq=s),
        grid=(b, nq),
        in_specs=[
            pl.BlockSpec((1, N_KV, HEADS_PER_KV, tq, LANES), lambda bb, i: (bb, 0, 0, i, 0)),
            full((N_KV, n_cmp, LANES)), full((N_KV, n_cmp, HEAD_DIM)),
            full((N_KV, s, LANES)), full((N_KV, HEAD_DIM + V_ONES, s)),
            full((N_KV, s, LANES)), full((N_KV, HEAD_DIM + V_ONES, s)),
            pl.BlockSpec((1, LANES, tq), lambda bb, i: (bb, 0, i)),
            const(ovl_t.shape), const(place.shape),
        ],
        out_specs=pl.BlockSpec((1, N_HEADS, HEAD_DIM, tq), lambda bb, i: (bb, 0, 0, i)),
        out_shape=jax.ShapeDtypeStruct((b, N_HEADS, HEAD_DIM, s), BF16),
        compiler_params=_cparams(("parallel", "arbitrary")),
        name="attn",
    )(q, kc, vc, ks, vs, kw, vw, gates, ovl_t, place)


POOL_HALO = 16


def _sigmoid(x):
    return 0.5 * jnp.tanh(0.5 * x) + 0.5


def _mix_kernel(x_ref, m0_ref, m1_ref, u_ref, halo_ref, o_ref, wh_ref, wp_ref, ps_ref, wo_ref,
                gf_ref, wr_ref, br_ref, y_ref, *, tm, seq):
    it = pl.program_id(0) % (seq // tm)
    u = u_ref[...]
    halo = jnp.where(it == 0, 0.0, halo_ref[...])
    ext = jnp.concatenate([halo, u], axis=0)
    sums = []
    acc = ext
    for sh in (1, 2, 4, 8):
        acc = acc + pltpu.roll(acc, sh, axis=0)
        sums.append(acc)
    t_in = it * tm + lax.broadcasted_iota(jnp.int32, (tm, 1), 0)
    pool_parts = []
    for gi, win in enumerate(POOL_WINDOWS):
        sl = slice(gi * POOL_GROUP_WIDTH, (gi + 1) * POOL_GROUP_WIDTH)
        cnt = jnp.minimum(t_in + 1, win).astype(F32)
        pooled = sums[gi][POOL_HALO:, sl] / cnt - u[:, sl]
        pool_parts.append(jnp.dot(pooled.astype(BF16), wp_ref[gi], preferred_element_type=F32))
    pool_branch = jnp.concatenate(pool_parts, axis=1) * ps_ref[...]
    attn_branch = jnp.concatenate(
        [_tn_dot(o_ref[0, h], wh_ref[h]) for h in range(N_HEADS)], axis=1)
    y = _sigmoid(m0_ref[...]) * attn_branch + _sigmoid(m1_ref[...]) * pool_branch
    x1 = x_ref[...] + jnp.dot(y.astype(BF16), wo_ref[...], preferred_element_type=F32)
    y_ref[:, 0:D_MODEL] = x1
    inv = lax.rsqrt(jnp.mean(x1 * x1, axis=-1, keepdims=True) + RMS_EPS)
    ht = x1 * inv * gf_ref[...]
    h_hi = ht.astype(BF16)
    h_lo = (ht - h_hi.astype(F32)).astype(BF16)
    logits = (jnp.dot(h_hi, wr_ref[0], preferred_element_type=F32)
              + jnp.dot(h_lo, wr_ref[0], preferred_element_type=F32)
              + jnp.dot(h_hi, wr_ref[1], preferred_element_type=F32)) + br_ref[...]
    y_ref[:, D_MODEL:D_MODEL + LANES] = _route(logits)


def _mix(x2, proj, o, wh, wp, ps, wo, gf, wr, br, s, tm=MIX_TM):
    t = x2.shape[0]
    nt = s // tm
    hb = tm // POOL_HALO
    const = lambda shape: pl.BlockSpec(shape, lambda i: (0,) * len(shape))
    return pl.pallas_call(
        functools.partial(_mix_kernel, tm=tm, seq=s),
        grid=(t // tm,),
        in_specs=[
            pl.BlockSpec((tm, D_MODEL), lambda i: (i, 0)),
            pl.BlockSpec((tm, D_MODEL), lambda i: (i, COL_M0 // 2)),
            pl.BlockSpec((tm, D_MODEL), lambda i: (i, COL_M1 // 2)),
            pl.BlockSpec((tm, POOL_WIDTH), lambda i: (i, COL_POOL)),
            pl.BlockSpec((POOL_HALO, POOL_WIDTH), lambda i: (jnp.maximum(i * hb - 1, 0), COL_POOL)),
            pl.BlockSpec((1, N_HEADS, HEAD_DIM, tm), lambda i: (i // nt, 0, 0, i % nt)),
            const(wh.shape), const(wp.shape), const(ps.shape), const(wo.shape),
            const(gf.shape), const(wr.shape), const(br.shape),
        ],
        out_specs=pl.BlockSpec((tm, ROW_WIDTH), lambda i: (i, 0)),
        out_shape=jax.ShapeDtypeStruct((t, ROW_WIDTH), F32),
        compiler_params=_cparams(("parallel",)),
        name="mix",
    )(x2, proj, proj, proj, proj, o, wh, wp, ps, wo, gf, wr, br)


ROUTE_EXPERT_OFF = N_EXPERT_GROUPS
ROW_WIDTH = D_MODEL + LANES
META_GROUP_LANE = EXPERTS_PER_GROUP


def _route(logits):
    lane = lax.broadcasted_iota(jnp.int32, logits.shape, 1)
    is_g = lane < N_EXPERT_GROUPS
    gmax = jnp.max(jnp.where(is_g, logits, -jnp.inf), axis=-1, keepdims=True)
    ge = jnp.where(is_g, jnp.exp(logits - gmax), 0.0)
    gprob = ge / jnp.sum(ge, axis=-1, keepdims=True)
    p_group = jnp.max(jnp.where(is_g, gprob, -jnp.inf), axis=-1, keepdims=True)
    g_sel = jnp.min(jnp.where(is_g & (gprob == p_group), lane, LANES), axis=-1, keepdims=True)
    e_lane = lane - ROUTE_EXPERT_OFF
    in_grp = ((e_lane >= 0) & (e_lane < N_EXPERTS)
              & (jnp.right_shift(e_lane, EXPERTS_PER_GROUP.bit_length() - 1) == g_sel))
    el = jnp.where(in_grp, logits, -jnp.inf)
    v1 = jnp.max(el, axis=-1, keepdims=True)
    i1 = jnp.min(jnp.where(in_grp & (el == v1), lane, LANES), axis=-1, keepdims=True)
    el2 = jnp.where(lane == i1, -jnp.inf, el)
    v2 = jnp.max(el2, axis=-1, keepdims=True)
    i2 = jnp.min(jnp.where(in_grp & (lane != i1) & (el2 == v2), lane, LANES), axis=-1, keepdims=True)
    e2 = jnp.exp(v2 - v1)
    den = 1.0 + e2
    first = ROUTE_EXPERT_OFF + EXPERTS_PER_GROUP * g_sel
    return (jnp.where(lane == i1 - first, (1.0 / den) * p_group, 0.0)
            + jnp.where(lane == i2 - first, (e2 / den) * p_group, 0.0)
            + jnp.where(lane == META_GROUP_LANE, g_sel.astype(F32), 0.0))


def _moe_kernel(tgrp_ref, nval_ref, tok_ref, tok_next_ref, xa_ref, g_ref, wg_ref, wu_ref, wd_ref, out_ref,
                buf_ref, obuf_ref, sem_in, sem_out, *, tm, n_tiles):
    i = pl.program_id(0)
    slot = i % 2

    def row_in(tok_smem, r, s):
        return pltpu.make_async_copy(xa_ref.at[pl.ds(tok_smem[0, 0, r], 1)], buf_ref.at[s, pl.ds(r, 1)],
                                     sem_in.at[s])

    def row_out(r, s):
        return pltpu.make_async_copy(obuf_ref.at[s, pl.ds(r, 1)], out_ref.at[pl.ds(tok_ref[0, 0, r], 1)],
                                     sem_out.at[s])

    def start_gather(tok_smem, s):
        def body(r, carry):
            row_in(tok_smem, r, s).start()
            return carry
        lax.fori_loop(0, tm, body, 0, unroll=8)

    def wait_scatter(tile, s):
        n = nval_ref[tile]

        def chunk(c, carry):
            pltpu.make_async_copy(obuf_ref.at[s, pl.ds(0, WAIT_CHUNK)], out_ref.at[pl.ds(0, WAIT_CHUNK)],
                                  sem_out.at[s]).wait()
            return carry

        def row(r, carry):
            row_out(0, s).wait()
            return carry
        lax.fori_loop(0, jnp.right_shift(n, WAIT_CHUNK.bit_length() - 1), chunk, 0)
        lax.fori_loop(0, n & (WAIT_CHUNK - 1), row, 0)

    @pl.when(i == 0)
    def _():
        start_gather(tok_ref, 0)

    nxt = jnp.minimum(i + 1, n_tiles - 1)
    tile_in = pltpu.make_async_copy(xa_ref.at[pl.ds(0, tm)], buf_ref.at[slot], sem_in.at[slot])
    next_in = pltpu.make_async_copy(xa_ref.at[pl.ds(0, tm)], buf_ref.at[1 - slot], sem_in.at[1 - slot])
    n_seg = 2 * EXPERTS_PER_GROUP

    def prefetch_segment(seg):
        for r in range(seg * tm // n_seg, (seg + 1) * tm // n_seg):
            row_in(tok_next_ref, r, 1 - slot).start()

    @pl.when(i >= 2)
    def _():
        wait_scatter(i - 2, slot)

    @pl.when(nval_ref[i] > 0)
    def _():
        tile_in.wait()
        x = buf_ref[slot, :, 0:D_MODEL]
        meta = buf_ref[slot, :, D_MODEL:ROW_WIDTH]
        inv = lax.rsqrt(jnp.mean(x * x, axis=-1, keepdims=True) + RMS_EPS)
        h = (x * inv * g_ref[...]).astype(BF16)
        hids = []
        for e in range(EXPERTS_PER_GROUP):
            a = jnp.dot(h, wg_ref[e], preferred_element_type=F32)
            prefetch_segment(2 * e)
            up = jnp.dot(h, wu_ref[e], preferred_element_type=F32)
            prefetch_segment(2 * e + 1)
            hids.append(a * _sigmoid(a) * up * meta[:, e:e + 1])
        hid = jnp.concatenate(hids, axis=1).astype(BF16)
        obuf_ref[slot] = x + jnp.dot(hid, wd_ref[0], preferred_element_type=F32)

        n = nval_ref[i]
        n_main = n & ~(SCATTER_UNROLL - 1)

        def chunk(c, carry):
            for u in range(SCATTER_UNROLL):
                row_out(c * SCATTER_UNROLL + u, slot).start()
            return carry

        def row(r, carry):
            row_out(n_main + r, slot).start()
            return carry
        lax.fori_loop(0, jnp.right_shift(n, SCATTER_UNROLL.bit_length() - 1), chunk, 0)
        lax.fori_loop(0, n - n_main, row, 0)

        @pl.when((i + 1 >= n_tiles) | (nval_ref[nxt] == 0))
        def _():
            next_in.wait()

    @pl.when(i == n_tiles - 1)
    def _():
        @pl.when(i >= 1)
        def _():
            wait_scatter(i - 1, 1 - slot)
        wait_scatter(i, slot)


def _moe(xa, g, tgrp, nval, tok, wg, wu, wd, tm):
    t = xa.shape[0]
    n_tiles = tok.shape[0]
    hid_w = EXPERTS_PER_GROUP * D_EXPERT
    smem_tile = lambda off: pl.BlockSpec(
        (1, 1, tm), lambda i, tg, nv: (jnp.minimum(i + off, n_tiles - 1), 0, 0), memory_space=pltpu.SMEM)
    grid_spec = pltpu.PrefetchScalarGridSpec(
        num_scalar_prefetch=2,
        grid=(n_tiles,),
        in_specs=[
            smem_tile(0), smem_tile(1),
            pl.BlockSpec(memory_space=pl.ANY),
            pl.BlockSpec((1, D_MODEL), lambda i, tg, nv: (0, 0)),
            pl.BlockSpec((EXPERTS_PER_GROUP, D_MODEL, D_EXPERT), lambda i, tg, nv: (tg[i], 0, 0)),
            pl.BlockSpec((EXPERTS_PER_GROUP, D_MODEL, D_EXPERT), lambda i, tg, nv: (tg[i], 0, 0)),
            pl.BlockSpec((1, hid_w, D_MODEL), lambda i, tg, nv: (tg[i], 0, 0)),
        ],
        out_specs=pl.BlockSpec(memory_space=pl.ANY),
        scratch_shapes=[pltpu.VMEM((2, tm, ROW_WIDTH), F32), pltpu.VMEM((2, tm, D_MODEL), F32),
                        pltpu.SemaphoreType.DMA((2,)), pltpu.SemaphoreType.DMA((2,))],
    )
    return pl.pallas_call(
        functools.partial(_moe_kernel, tm=tm, n_tiles=n_tiles),
        grid_spec=grid_spec,
        out_shape=jax.ShapeDtypeStruct((t, D_MODEL), F32),
        compiler_params=_cparams(("arbitrary",)),
        name="moe",
    )(tgrp, nval, tok, tok, xa, g, wg, wu, wd)


WAIT_CHUNK = 64
SCATTER_UNROLL = 8


def _moe_plan(gid, tm):
    t = gid.shape[0]
    n_tiles = t // tm + N_EXPERT_GROUPS
    order = jnp.argsort(gid, stable=True).astype(jnp.int32)
    counts = jnp.sum((gid[:, None] == jnp.arange(N_EXPERT_GROUPS)[None, :]).astype(jnp.int32), axis=0)
    tiles_per = (counts + tm - 1) // tm
    tile_end = jnp.cumsum(tiles_per)
    row_start = jnp.cumsum(counts) - counts
    tile = jnp.arange(n_tiles, dtype=jnp.int32)
    tgrp = jnp.minimum(jnp.sum((tile[:, None] >= tile_end[None, :]).astype(jnp.int32), axis=1),
                       N_EXPERT_GROUPS - 1)
    local = tile - (tile_end - tiles_per)[tgrp]
    nval = jnp.clip(counts[tgrp] - local * tm, 0, tm)
    slot = jnp.arange(tm, dtype=jnp.int32)[None, :]
    src = row_start[tgrp][:, None] + local[:, None] * tm + jnp.where(slot < nval[:, None], slot, 0)
    tok = order[jnp.clip(src, 0, t - 1)]
    return tgrp.astype(jnp.int32), nval.astype(jnp.int32), tok.reshape(n_tiles, 1, tm)


def _rope_lane_freq():
    inv_freq = ROPE_THETA ** (-jnp.arange(0, ROPE_DIM, 2, dtype=F32) / ROPE_DIM)
    d = np.arange(LANES) % HEAD_DIM
    f = jnp.where(jnp.asarray(d < ROPE_DIM), inv_freq[np.asarray(d % (ROPE_DIM // 2))], 0.0)
    return f.reshape(1, LANES).astype(F32)


def _head_of_lane(width):
    return jnp.asarray((np.arange(width) // HEAD_DIM)[:, None] == np.arange(LANES)[None, :], dtype=BF16)


def _overlap_t(n_cmp_pad, n_blk):
    a0 = np.arange(n_cmp_pad)[:, None] * CMP_STRIDE
    b0 = np.arange(n_blk)[None, :] * SEL_BLOCK
    ov = np.clip(np.minimum(a0 + CMP_BLOCK, b0 + SEL_BLOCK) - np.maximum(a0, b0), 0, None) / CMP_BLOCK
    return jnp.asarray(ov.T, dtype=BF16)


def _bias_place(n_blk):
    return jnp.asarray(np.arange(n_blk)[:, None] + BLK_LANE0 == np.arange(LANES)[None, :], dtype=BF16)


def _layer(x, positions, attn_norm_g, w_in, q_norm_g, k_norm_cmp_g, k_norm_slc_g, k_norm_swa_g,
           cmp_pos_emb_k, cmp_w1_k, cmp_w2_k, cmp_pos_emb_v, cmp_w1_v, cmp_w2_v, w_head_out, w_pool,
           pool_scale, w_out, ffn_norm_g, w_router_group, b_router_group, w_router_expert,
           b_router_expert, w_expert_gate, w_expert_up, w_expert_down):
    b, s, d = x.shape
    t = b * s
    x2 = x.reshape(t, d)

    o_q, o_kv, o_g = ATTN_WIDTH, ATTN_WIDTH + 6 * KV_WIDTH, ATTN_WIDTH + 6 * KV_WIDTH + N_HEADS * N_BRANCH
    o_p = o_g + POOL_WIDTH
    used = 2 * ATTN_WIDTH + 2 * D_MODEL + 6 * KV_WIDTH + N_HEADS * N_BRANCH
    w_packed = jnp.concatenate(
        [w_in[:, :o_q], w_in[:, o_g:o_p], w_in[:, o_p:], w_in[:, o_q:o_kv], w_in[:, o_kv:o_g],
         jnp.zeros((d, PROJ_WIDTH - used), w_in.dtype)], axis=1).astype(BF16)
    proj = _proj(x2, attn_norm_g.reshape(1, d), w_packed)

    f_lane = _rope_lane_freq()
    tile_h = lambda g_, n: jnp.tile(g_, n).reshape(1, n * HEAD_DIM)
    q, ks, vs, kw, vw, kc_raw, vc_raw, gates = _prep(
        proj, positions.reshape(t, 1), f_lane, tile_h(q_norm_g, N_HEADS), tile_h(k_norm_slc_g, N_KV),
        tile_h(k_norm_swa_g, N_KV), _head_of_lane(ATTN_WIDTH), _head_of_lane(KV_WIDTH), b, s)

    n_row = s // CMP_STRIDE
    half = CMP_STRIDE * HEAD_DIM
    xk = kc_raw.reshape(b, N_KV, n_row, half)
    xv = vc_raw.reshape(b, N_KV, n_row, half)
    pad2 = lambda w2: jnp.pad(w2, ((0, 0), (0, LANES - HEAD_DIM))).astype(BF16)
    cmp_end = np.minimum(np.arange(n_row) * CMP_STRIDE + CMP_BLOCK - 1, s - 1)
    posc = positions[:, cmp_end].reshape(b, n_row, 1)
    gk = jnp.pad(k_norm_cmp_g, (0, LANES - HEAD_DIM)).reshape(1, LANES)
    kc, vc = _compress(xk, xv, cmp_pos_emb_k.reshape(2, half), cmp_pos_emb_v.reshape(2, half),
                       cmp_w1_k.astype(BF16), pad2(cmp_w2_k), cmp_w1_v.astype(BF16), pad2(cmp_w2_v),
                       gk, posc, f_lane)

    n_blk = s // SEL_BLOCK
    assert n_blk <= LANES - BLK_LANE0
    o = _attn(q, kc, vc, ks, vs, kw, vw, gates, _overlap_t(n_row, n_blk), _bias_place(n_blk))

    wr = jnp.concatenate([w_router_group, w_router_expert,
                          jnp.zeros((d, LANES - N_EXPERT_GROUPS - N_EXPERTS), F32)], axis=1)
    br = jnp.concatenate([b_router_group, b_router_expert,
                          jnp.zeros((LANES - N_EXPERT_GROUPS - N_EXPERTS,), F32)]).reshape(1, LANES)
    gf = ffn_norm_g.reshape(1, d)
    wr_hi = wr.astype(BF16)
    wr2 = jnp.stack([wr_hi, (wr - wr_hi.astype(F32)).astype(BF16)])
    xa = _mix(x2, proj, o, w_head_out.astype(BF16), w_pool.astype(BF16), pool_scale.reshape(1, d),
              w_out.astype(BF16), gf, wr2, br, s)

    tgrp, nval, tok = _moe_plan(xa[:, D_MODEL + META_GROUP_LANE].astype(jnp.int32), MOE_TM)
    wd = w_expert_down.reshape(N_EXPERT_GROUPS, EXPERTS_PER_GROUP * D_EXPERT, d)
    x_out = _moe(xa, gf, tgrp, nval, tok, w_expert_gate.astype(BF16), w_expert_up.astype(BF16),
                 wd.astype(BF16), MOE_TM)
    return x_out.reshape(b, s, d)


def kernel(x, positions, attn_norm_g, w_in, q_norm_g, k_norm_cmp_g, k_norm_slc_g, k_norm_swa_g,
           cmp_pos_emb_k, cmp_w1_k, cmp_w2_k, cmp_pos_emb_v, cmp_w1_v, cmp_w2_v, w_head_out, w_pool,
           pool_scale, w_out, ffn_norm_g, w_router_group, b_router_group, w_router_expert,
           b_router_expert, w_expert_gate, w_expert_up, w_expert_down):
    params = (attn_norm_g, w_in, q_norm_g, k_norm_cmp_g, k_norm_slc_g, k_norm_swa_g, cmp_pos_emb_k,
              cmp_w1_k, cmp_w2_k, cmp_pos_emb_v, cmp_w1_v, cmp_w2_v, w_head_out, w_pool, pool_scale,
              w_out, ffn_norm_g, w_router_group, b_router_group, w_router_expert, b_router_expert,
              w_expert_gate, w_expert_up, w_expert_down)
    for layer in range(attn_norm_g.shape[0]):
        x = _layer(x, positions, *[p[layer] for p in params])
    return x
```

```python
import functools

import jax
import jax.numpy as jnp
import numpy as np
from jax import lax
from jax.experimental import pallas as pl
from jax.experimental.pallas import tpu as pltpu

F32 = jnp.float32
BF16 = jnp.bfloat16

D_MODEL = 2048
N_HEADS = 16
HEAD_DIM = 64
N_KV = 4
HEADS_PER_KV = N_HEADS // N_KV
ATTN_WIDTH = N_HEADS * HEAD_DIM
KV_WIDTH = N_KV * HEAD_DIM
N_BRANCH = 3
CMP_BLOCK = 32
CMP_STRIDE = 16
CMP_HIDDEN = 4 * HEAD_DIM
SEL_BLOCK = 64
N_SEL = 16
N_FORCED_LOCAL = 2
WINDOW = 512
FORCE_BONUS = 1e4
NEG_INF = -1e30
ROPE_THETA = 500000.0
ROPE_DIM = HEAD_DIM // 4
HEAD_OUT_WIDTH = D_MODEL // N_HEADS
POOL_WINDOWS = (2, 4, 8, 16)
POOL_GROUPS = 4
POOL_WIDTH = 1024
POOL_GROUP_WIDTH = POOL_WIDTH // POOL_GROUPS
POOL_OUT_WIDTH = D_MODEL // POOL_GROUPS
N_EXPERT_GROUPS = 4
EXPERTS_PER_GROUP = 4
N_EXPERTS = N_EXPERT_GROUPS * EXPERTS_PER_GROUP
D_EXPERT = 256
RMS_EPS = 1e-6

LANES = 128
Q_SCALE = HEAD_DIM ** -0.5 * float(np.log2(np.e))
BLK_LANE0 = HEAD_DIM
V_ONES = 16
PROJ_WIDTH = 8192
PROJ_TN = 1024
COL_Q, COL_POOL, COL_M0, COL_M1, COL_KVA, COL_KVB = 0, 1, 2, 4, 6, 7
GATE_OFF = 2 * KV_WIDTH

V7X_VMEM_BYTES = 64 * 1024 * 1024
VMEM_LIMIT = V7X_VMEM_BYTES - 8 * 1024 * 1024
PROJ_TM, PROJ_STEP_TN = 1024, 2048
PREP_TM = 512
ATTN_TQ = 512
MIX_TM = 512
MOE_TM = 512


def _cparams(sem):
    return pltpu.CompilerParams(dimension_semantics=sem, vmem_limit_bytes=VMEM_LIMIT)


def _nt_dot(a, b):
    return lax.dot_general(a, b, (((1,), (1,)), ((), ())), preferred_element_type=F32)


def _tn_dot(a, b):
    return lax.dot_general(a, b, (((0,), (0,)), ((), ())), preferred_element_type=F32)


def _split_dot(x, e):
    hi = x.astype(BF16)
    lo = (x - hi.astype(F32)).astype(BF16)
    return (jnp.dot(hi, e, preferred_element_type=F32)
            + jnp.dot(lo, e, preferred_element_type=F32))


def _proj_kernel(x_ref, g_ref, w_ref, o_ref, h_ref):
    @pl.when(pl.program_id(1) == 0)
    def _():
        x = x_ref[...]
        inv = lax.rsqrt(jnp.mean(x * x, axis=-1, keepdims=True) + RMS_EPS)
        h_ref[...] = (x * inv * g_ref[...]).astype(BF16)

    o_ref[...] = jnp.dot(h_ref[...], w_ref[...], preferred_element_type=F32)


def _proj(x2, g, w_packed, tm=PROJ_TM, tn=PROJ_STEP_TN):
    t = x2.shape[0]
    return pl.pallas_call(
        _proj_kernel,
        grid=(t // tm, PROJ_WIDTH // tn),
        in_specs=[
            pl.BlockSpec((tm, D_MODEL), lambda i, j: (i, 0)),
            pl.BlockSpec((1, D_MODEL), lambda i, j: (0, 0)),
            pl.BlockSpec((D_MODEL, tn), lambda i, j: (0, j)),
        ],
        out_specs=pl.BlockSpec((tm, tn), lambda i, j: (i, j)),
        out_shape=jax.ShapeDtypeStruct((t, PROJ_WIDTH), F32),
        scratch_shapes=[pltpu.VMEM((tm, D_MODEL), BF16)],
        compiler_params=_cparams(("parallel", "arbitrary")),
        name="proj",
    )(x2, g, w_packed)


def _rope(xn, cos, sin):
    w = xn.shape[-1]
    lane = lax.broadcasted_iota(jnp.int32, xn.shape, 1) & (HEAD_DIM - 1)
    up = pltpu.roll(xn, w - ROPE_DIM // 2, axis=1)
    dn = pltpu.roll(xn, ROPE_DIM // 2, axis=1)
    return jnp.where(lane < ROPE_DIM // 2, xn * cos - up * sin,
                     jnp.where(lane < ROPE_DIM, xn * cos + dn * sin, xn))


def _head_norm_rope(x, gain, head_of_lane, cos, sin):
    w = x.shape[-1]
    per_head = _split_dot(x * x, head_of_lane)
    hi = per_head.astype(BF16)
    lo = (per_head - hi.astype(F32)).astype(BF16)
    ssq = _nt_dot(hi, head_of_lane) + _nt_dot(lo, head_of_lane)
    xn = x * lax.rsqrt(ssq * (1.0 / HEAD_DIM) + RMS_EPS) * gain
    reps = w // LANES
    return _rope(xn, jnp.concatenate([cos] * reps, axis=1), jnp.concatenate([sin] * reps, axis=1))


def _widen_head(x, h, upper):
    col = x[:, (h // 2) * LANES:(h // 2 + 1) * LANES]
    if h % 2:
        col = pltpu.roll(col, HEAD_DIM, axis=1)
    lane = lax.broadcasted_iota(jnp.int32, col.shape, 1)
    return jnp.where(lane < HEAD_DIM, col, upper)


def _prep_kernel(q_ref, kva_ref, kvb_ref, pos_ref, f_ref, gq_ref, gs_ref, gw_ref, eq_ref, ek_ref,
                 qo_ref, kso_ref, vso_ref, kwo_ref, vwo_ref, kco_ref, vco_ref, go_ref):
    tm = q_ref.shape[0]
    ang = pos_ref[...].astype(F32) * f_ref[...]
    cos, sin = jnp.cos(ang), jnp.sin(ang)
    q = _head_norm_rope(q_ref[...], gq_ref[...], eq_ref[...], cos, sin) * Q_SCALE
    lane = lax.broadcasted_iota(jnp.int32, (tm, LANES), 1)
    t_in = pl.program_id(1) * tm + lax.broadcasted_iota(jnp.int32, (tm, LANES), 0)
    blk_onehot = (jnp.right_shift(t_in, SEL_BLOCK.bit_length() - 1) == lane - BLK_LANE0).astype(F32)
    for h in range(N_HEADS):
        qo_ref[0, h // HEADS_PER_KV, h % HEADS_PER_KV] = _widen_head(q, h, 0.0).astype(BF16)
    kva = kva_ref[...]
    kvb = kvb_ref[...]
    ks = _head_norm_rope(kva[:, 2 * KV_WIDTH:3 * KV_WIDTH], gs_ref[...], ek_ref[...], cos, sin)
    kw = _head_norm_rope(kvb[:, 0:KV_WIDTH], gw_ref[...], ek_ref[...], cos, sin)
    vs = kva[:, 3 * KV_WIDTH:4 * KV_WIDTH]
    vw = kvb[:, KV_WIDTH:2 * KV_WIDTH]
    kc = kva[:, 0:KV_WIDTH]
    vc = kva[:, KV_WIDTH:2 * KV_WIDTH]
    for g in range(N_KV):
        sl = slice(g * HEAD_DIM, (g + 1) * HEAD_DIM)
        kso_ref[0, g] = _widen_head(ks, g, blk_onehot).astype(BF16)
        kwo_ref[0, g] = _widen_head(kw, g, 0.0).astype(BF16)
        kco_ref[0, g] = kc[:, sl]
        vco_ref[0, g] = vc[:, sl]
    ones = jnp.ones((V_ONES, tm), BF16)
    for v, vo_ref in ((vs, vso_ref), (vw, vwo_ref)):
        vt = v.T.astype(BF16)
        for g in range(N_KV):
            vo_ref[0, g, 0:HEAD_DIM, :] = vt[g * HEAD_DIM:(g + 1) * HEAD_DIM]
            vo_ref[0, g, HEAD_DIM:HEAD_DIM + V_ONES, :] = ones
    go_ref[0] = jax.nn.sigmoid(kvb[:, GATE_OFF:GATE_OFF + LANES]).T


def _prep(proj, pos2, f_lane, gq, gs, gw, e_q, e_k, b, s, tm=PREP_TM):
    nt = s // tm
    row = lambda bb, i: bb * nt + i
    kv_shape = jax.ShapeDtypeStruct((b, N_KV, s, HEAD_DIM), F32)
    kv_spec = pl.BlockSpec((1, N_KV, tm, HEAD_DIM), lambda bb, i: (bb, 0, i, 0))
    kx_shape = jax.ShapeDtypeStruct((b, N_KV, s, LANES), BF16)
    kx_spec = pl.BlockSpec((1, N_KV, tm, LANES), lambda bb, i: (bb, 0, i, 0))
    vt_shape = jax.ShapeDtypeStruct((b, N_KV, HEAD_DIM + V_ONES, s), BF16)
    vt_spec = pl.BlockSpec((1, N_KV, HEAD_DIM + V_ONES, tm), lambda bb, i: (bb, 0, 0, i))
    const = lambda shape: pl.BlockSpec(shape, lambda bb, i: (0,) * len(shape))
    return pl.pallas_call(
        _prep_kernel,
        grid=(b, nt),
        in_specs=[
            pl.BlockSpec((tm, PROJ_TN), lambda bb, i: (row(bb, i), COL_Q)),
            pl.BlockSpec((tm, PROJ_TN), lambda bb, i: (row(bb, i), COL_KVA)),
            pl.BlockSpec((tm, PROJ_TN), lambda bb, i: (row(bb, i), COL_KVB)),
            pl.BlockSpec((tm, 1), lambda bb, i: (row(bb, i), 0)),
            const((1, LANES)), const((1, ATTN_WIDTH)), const((1, KV_WIDTH)), const((1, KV_WIDTH)),
            const((ATTN_WIDTH, LANES)), const((KV_WIDTH, LANES)),
        ],
        out_specs=[
            pl.BlockSpec((1, N_KV, HEADS_PER_KV, tm, LANES), lambda bb, i: (bb, 0, 0, i, 0)),
            kx_spec, vt_spec, kx_spec, vt_spec, kv_spec, kv_spec,
            pl.BlockSpec((1, LANES, tm), lambda bb, i: (bb, 0, i)),
        ],
        out_shape=[
            jax.ShapeDtypeStruct((b, N_KV, HEADS_PER_KV, s, LANES), BF16),
            kx_shape, vt_shape, kx_shape, vt_shape, kv_shape, kv_shape,
            jax.ShapeDtypeStruct((b, LANES, s), F32),
        ],
        compiler_params=_cparams(("parallel", "parallel")),
        name="prep",
    )(proj, proj, proj, pos2, f_lane, gq, gs, gw, e_q, e_k)


def _compress_one(x, pe_a, pe_b, w1_ref, w2_ref):
    half = CMP_STRIDE * HEAD_DIM
    nxt = pltpu.roll(x, x.shape[0] - 1, axis=0)
    a = (x + pe_a).astype(BF16)
    b = (nxt + pe_b).astype(BF16)
    hid = (jnp.dot(a, w1_ref[0:half, :], preferred_element_type=F32)
           + jnp.dot(b, w1_ref[half:2 * half, :], preferred_element_type=F32))
    hid = hid * jax.nn.sigmoid(hid)
    return jnp.dot(hid.astype(BF16), w2_ref[...], preferred_element_type=F32)


def _compress_kernel(xk_ref, xv_ref, pek_ref, pev_ref, w1k_ref, w2k_ref, w1v_ref, w2v_ref,
                     gk_ref, posc_ref, f_ref, kc_ref, vc_ref):
    ang = posc_ref[0].astype(F32) * f_ref[...]
    cos, sin = jnp.cos(ang), jnp.sin(ang)
    for g in range(N_KV):
        k = _compress_one(xk_ref[0, g], pek_ref[0:1, :], pek_ref[1:2, :], w1k_ref, w2k_ref)
        ssq = jnp.sum(k * k, axis=-1, keepdims=True)
        kn = k * lax.rsqrt(ssq * (1.0 / HEAD_DIM) + RMS_EPS) * gk_ref[...]
        kn = _rope(kn, cos, sin)
        kc_ref[0, g] = kn.astype(BF16)
        v = _compress_one(xv_ref[0, g], pev_ref[0:1, :], pev_ref[1:2, :], w1v_ref, w2v_ref)
        vc_ref[0, g] = v[:, 0:HEAD_DIM].astype(BF16)


def _compress(xk, xv, pek, pev, w1k, w2k, w1v, w2v, gk, posc, f_lane):
    b = xk.shape[0]
    nrow = xk.shape[2]
    half = CMP_STRIDE * HEAD_DIM
    const = lambda shape: pl.BlockSpec(shape, lambda bb: (0,) * len(shape))
    x_spec = pl.BlockSpec((1, N_KV, nrow, half), lambda bb: (bb, 0, 0, 0))
    o_spec = pl.BlockSpec((1, N_KV, nrow, HEAD_DIM), lambda bb: (bb, 0, 0, 0))
    o_shape = jax.ShapeDtypeStruct((b, N_KV, nrow, HEAD_DIM), BF16)
    k_spec = pl.BlockSpec((1, N_KV, nrow, LANES), lambda bb: (bb, 0, 0, 0))
    k_shape = jax.ShapeDtypeStruct((b, N_KV, nrow, LANES), BF16)
    return pl.pallas_call(
        _compress_kernel,
        grid=(b,),
        in_specs=[x_spec, x_spec, const((2, half)), const((2, half)),
                  const((2 * half, CMP_HIDDEN)), const((CMP_HIDDEN, LANES)),
                  const((2 * half, CMP_HIDDEN)), const((CMP_HIDDEN, LANES)),
                  const((1, LANES)), pl.BlockSpec((1, nrow, 1), lambda bb: (bb, 0, 0)), const((1, LANES))],
        out_specs=[k_spec, o_spec],
        out_shape=[k_shape, o_shape],
        compiler_params=_cparams(("parallel",)),
        name="compress",
    )(xk, xv, pek, pev, w1k, w2k, w1v, w2v, gk, posc, f_lane)


def _attn_kernel(q_ref, kc_ref, vc_ref, ks_ref, vs_ref, kw_ref, vw_ref, gate_ref, ovl_ref, place_ref,
                 o_ref, *, tq, seq):
    r = HEADS_PER_KV
    cols = r * tq
    n_cmp = kc_ref.shape[2]
    n_blk = seq // SEL_BLOCK
    i = pl.program_id(1)
    t0 = i * tq
    gates = gate_ref[0]

    kk = lax.broadcasted_iota(jnp.int32, (tq, cols), 0)
    tt = lax.broadcasted_iota(jnp.int32, (tq, cols), 1) & (tq - 1)
    causal_bias = jnp.where(kk <= tt, 0.0, NEG_INF)
    band_bias = jnp.where(kk > tt, 0.0, NEG_INF)

    q_sel, o_cmps = [], []

    for g in range(N_KV):
        q3 = q_ref[0, g]
        qg = q3.reshape(cols, LANES)

        s = _nt_dot(kc_ref[0, g], qg)
        n_idx = lax.broadcasted_iota(jnp.int32, (n_cmp, cols), 0)
        t_col = t0 + (lax.broadcasted_iota(jnp.int32, (n_cmp, cols), 1) & (tq - 1))
        mask = (n_idx * CMP_STRIDE + (CMP_BLOCK - 1)) <= t_col
        s = jnp.where(mask, s, NEG_INF)
        p = jnp.where(mask, jnp.exp2(s - jnp.max(s, axis=0, keepdims=True)), 0.0)
        l = jnp.sum(p, axis=0, keepdims=True)
        p = p * (1.0 / jnp.where(l > 0.0, l, 1.0))
        o_cmp = _tn_dot(vc_ref[0, g], p.astype(BF16))

        p_sum = p[:, 0:tq] + p[:, tq:2 * tq] + p[:, 2 * tq:3 * tq] + p[:, 3 * tq:4 * tq]
        hi = p_sum.astype(BF16)
        lo = (p_sum - hi.astype(F32)).astype(BF16)
        imp = (jnp.dot(ovl_ref[...], hi, preferred_element_type=F32)
               + jnp.dot(ovl_ref[...], lo, preferred_element_type=F32))
        j_idx = lax.broadcasted_iota(jnp.int32, (n_blk, tq), 0)
        t_idx = t0 + lax.broadcasted_iota(jnp.int32, (n_blk, tq), 1)
        dist = jnp.right_shift(t_idx, SEL_BLOCK.bit_length() - 1) - j_idx
        valid = dist >= 0
        forced = (j_idx == 0) | (valid & (dist < N_FORCED_LOCAL))
        score = jnp.where(valid, imp + FORCE_BONUS * forced.astype(F32), NEG_INF)
        cnt = jnp.zeros((n_blk, tq), F32)
        for jp in range(n_blk):
            row = score[jp:jp + 1, :]
            beats = (row > score) | ((row == score) & (j_idx > jp))
            cnt = cnt + beats.astype(F32)
        selected = (cnt < float(min(N_SEL, n_blk))) & valid
        blk_bias = jnp.where(selected, 0.0, NEG_INF).astype(BF16)
        q_bias = _tn_dot(blk_bias, place_ref[...]).astype(BF16)
        q_sel.append((q3 + q_bias[None]).reshape(cols, LANES))
        o_cmps.append(o_cmp)

    def sel_tile(g, k0, carry, bias):
        m_i, acc = carry
        s = _nt_dot(ks_ref[0, g, pl.ds(k0, tq), :], q_sel[g])
        if bias is not None:
            s = s + bias
        m_new = jnp.maximum(m_i, jnp.max(s, axis=0, keepdims=True))
        p = jnp.exp2((s - m_new).astype(BF16))
        acc = jnp.exp2(m_i - m_new) * acc + jnp.dot(vs_ref[0, g, :, pl.ds(k0, tq)], p,
                                                    preferred_element_type=F32)
        return m_new, acc

    def sel_body(kt, carries):
        k0 = pl.multiple_of(kt * tq, tq)
        return tuple(sel_tile(g, k0, carries[g], None) for g in range(N_KV))

    init = (jnp.full((1, cols), NEG_INF, F32), jnp.zeros((HEAD_DIM + V_ONES, cols), F32))
    carries = lax.fori_loop(0, i, sel_body, (init,) * N_KV)

    n_old = WINDOW // tq
    own = pl.multiple_of(t0, tq)
    oldest_bias = band_bias + jnp.where(t0 < WINDOW, NEG_INF, 0.0)
    for g in range(N_KV):
        starts = [t0 + (j - n_old) * tq for j in range(n_old + 1)]
        k0s = [pl.multiple_of(jnp.maximum(st, 0), tq) for st in starts]
        k_all = jnp.concatenate([ks_ref[0, g, pl.ds(own, tq), :]]
                                + [kw_ref[0, g, pl.ds(k0, tq), :] for k0 in k0s], axis=0)
        s_all = _nt_dot(k_all, q_sel[g])

        m_i, acc = carries[g]
        s = s_all[0:tq] + causal_bias
        m_new = jnp.maximum(m_i, jnp.max(s, axis=0, keepdims=True))
        acc_s = jnp.exp2(m_i - m_new) * acc + jnp.dot(vs_ref[0, g, :, pl.ds(own, tq)],
                                                      jnp.exp2((s - m_new).astype(BF16)),
                                                      preferred_element_type=F32)
        o_slc = acc_s[0:HEAD_DIM] * (1.0 / acc_s[HEAD_DIM:HEAD_DIM + 1])

        s_parts, v_parts = [], []
        for j in range(n_old + 1):
            s = s_all[(j + 1) * tq:(j + 2) * tq]
            if j == 0:
                s = s + oldest_bias
            elif j < n_old:
                s = s + jnp.where(starts[j] < 0, NEG_INF, 0.0)
            else:
                s = s + causal_bias
            s_parts.append(s)
            v_parts.append(vw_ref[0, g, :, pl.ds(k0s[j], tq)])
        m = functools.reduce(jnp.maximum, [jnp.max(s, axis=0, keepdims=True) for s in s_parts])
        acc_w = sum(jnp.dot(v, jnp.exp2((s - m).astype(BF16)), preferred_element_type=F32)
                    for s, v in zip(s_parts, v_parts))
        o_swa = acc_w[0:HEAD_DIM] * (1.0 / acc_w[HEAD_DIM:HEAD_DIM + 1])

        def gate_row(c):
            parts = [gates[(g * r + hh) * N_BRANCH + c:(g * r + hh) * N_BRANCH + c + 1, :] for hh in range(r)]
            return jnp.concatenate(parts, axis=1)

        o = gate_row(0) * o_cmps[g] + gate_row(1) * o_slc + gate_row(2) * o_swa
        for hh in range(r):
            o_ref[0, g * r + hh] = o[:, hh * tq:(hh + 1) * tq].astype(BF16)


def _attn(q, kc, vc, ks, vs, kw, vw, gates, ovl_t, place, tq=ATTN_TQ):
    b, _, _, s, _ = q.shape
    n_cmp = kc.shape[2]
    nq = s // tq
    assert WINDOW % tq == 0 and tq % SEL_BLOCK == 0
    full = lambda shape: pl.BlockSpec((1,) + shape, lambda bb, i: (bb,) + (0,) * len(shape))
    const = lambda shape: pl.BlockSpec(shape, lambda bb, i: (0,) * len(shape))
    return pl.pallas_call(
        functools.partial(_attn_kernel, tq=tq, seq=s),
        grid=(b, nq),
        in_specs=[
            pl.BlockSpec((1, N_KV, HEADS_PER_KV, tq, LANES), lambda bb, i: (bb, 0, 0, i, 0)),
            full((N_KV, n_cmp, LANES)), full((N_KV, n_cmp, HEAD_DIM)),
            full((N_KV, s, LANES)), full((N_KV, HEAD_DIM + V_ONES, s)),
            full((N_KV, s, LANES)), full((N_KV, HEAD_DIM + V_ONES, s)),
            pl.BlockSpec((1, LANES, tq), lambda bb, i: (bb, 0, i)),
            const(ovl_t.shape), const(place.shape),
        ],
        out_specs=pl.BlockSpec((1, N_HEADS, HEAD_DIM, tq), lambda bb, i: (bb, 0, 0, i)),
        out_shape=jax.ShapeDtypeStruct((b, N_HEADS, HEAD_DIM, s), BF16),
        compiler_params=_cparams(("parallel", "arbitrary")),
        name="attn",
    )(q, kc, vc, ks, vs, kw, vw, gates, ovl_t, place)


POOL_HALO = 16


def _sigmoid(x):
    return 0.5 * jnp.tanh(0.5 * x) + 0.5


def _mix_kernel(x_ref, m0_ref, m1_ref, u_ref, halo_ref, o_ref, wh_ref, wp_ref, ps_ref, wo_ref,
                gf_ref, wr_ref, br_ref, y_ref, *, tm, seq):
    it = pl.program_id(0) % (seq // tm)
    u = u_ref[...]
    halo = jnp.where(it == 0, 0.0, halo_ref[...])
    ext = jnp.concatenate([halo, u], axis=0)
    sums = []
    acc = ext
    for sh in (1, 2, 4, 8):
        acc = acc + pltpu.roll(acc, sh, axis=0)
        sums.append(acc)
    t_in = it * tm + lax.broadcasted_iota(jnp.int32, (tm, 1), 0)
    pool_parts = []
    for gi, win in enumerate(POOL_WINDOWS):
        sl = slice(gi * POOL_GROUP_WIDTH, (gi + 1) * POOL_GROUP_WIDTH)
        cnt = jnp.minimum(t_in + 1, win).astype(F32)
        pooled = sums[gi][POOL_HALO:, sl] / cnt - u[:, sl]
        pool_parts.append(jnp.dot(pooled.astype(BF16), wp_ref[gi], preferred_element_type=F32))
    pool_branch = jnp.concatenate(pool_parts, axis=1) * ps_ref[...]
    attn_branch = jnp.concatenate(
        [_tn_dot(o_ref[0, h], wh_ref[h]) for h in range(N_HEADS)], axis=1)
    y = _sigmoid(m0_ref[...]) * attn_branch + _sigmoid(m1_ref[...]) * pool_branch
    x1 = x_ref[...] + jnp.dot(y.astype(BF16), wo_ref[...], preferred_element_type=F32)
    y_ref[:, 0:D_MODEL] = x1
    inv = lax.rsqrt(jnp.mean(x1 * x1, axis=-1, keepdims=True) + RMS_EPS)
    ht = x1 * inv * gf_ref[...]
    h_hi = ht.astype(BF16)
    h_lo = (ht - h_hi.astype(F32)).astype(BF16)
    logits = (jnp.dot(h_hi, wr_ref[0], preferred_element_type=F32)
              + jnp.dot(h_lo, wr_ref[0], preferred_element_type=F32)
              + jnp.dot(h_hi, wr_ref[1], preferred_element_type=F32)) + br_ref[...]
    y_ref[:, D_MODEL:D_MODEL + LANES] = _route(logits)


def _mix(x2, proj, o, wh, wp, ps, wo, gf, wr, br, s, tm=MIX_TM):
    t = x2.shape[0]
    nt = s // tm
    hb = tm // POOL_HALO
    const = lambda shape: pl.BlockSpec(shape, lambda i: (0,) * len(shape))
    return pl.pallas_call(
        functools.partial(_mix_kernel, tm=tm, seq=s),
        grid=(t // tm,),
        in_specs=[
            pl.BlockSpec((tm, D_MODEL), lambda i: (i, 0)),
            pl.BlockSpec((tm, D_MODEL), lambda i: (i, COL_M0 // 2)),
            pl.BlockSpec((tm, D_MODEL), lambda i: (i, COL_M1 // 2)),
            pl.BlockSpec((tm, POOL_WIDTH), lambda i: (i, COL_POOL)),
            pl.BlockSpec((POOL_HALO, POOL_WIDTH), lambda i: (jnp.maximum(i * hb - 1, 0), COL_POOL)),
            pl.BlockSpec((1, N_HEADS, HEAD_DIM, tm), lambda i: (i // nt, 0, 0, i % nt)),
            const(wh.shape), const(wp.shape), const(ps.shape), const(wo.shape),
            const(gf.shape), const(wr.shape), const(br.shape),
        ],
        out_specs=pl.BlockSpec((tm, ROW_WIDTH), lambda i: (i, 0)),
        out_shape=jax.ShapeDtypeStruct((t, ROW_WIDTH), F32),
        compiler_params=_cparams(("parallel",)),
        name="mix",
    )(x2, proj, proj, proj, proj, o, wh, wp, ps, wo, gf, wr, br)


ROUTE_EXPERT_OFF = N_EXPERT_GROUPS
ROW_WIDTH = D_MODEL + LANES
META_GROUP_LANE = EXPERTS_PER_GROUP


def _route(logits):
    lane = lax.broadcasted_iota(jnp.int32, logits.shape, 1)
    is_g = lane < N_EXPERT_GROUPS
    gmax = jnp.max(jnp.where(is_g, logits, -jnp.inf), axis=-1, keepdims=True)
    ge = jnp.where(is_g, jnp.exp(logits - gmax), 0.0)
    gprob = ge / jnp.sum(ge, axis=-1, keepdims=True)
    p_group = jnp.max(jnp.where(is_g, gprob, -jnp.inf), axis=-1, keepdims=True)
    g_sel = jnp.min(jnp.where(is_g & (gprob == p_group), lane, LANES), axis=-1, keepdims=True)
    e_lane = lane - ROUTE_EXPERT_OFF
    in_grp = ((e_lane >= 0) & (e_lane < N_EXPERTS)
              & (jnp.right_shift(e_lane, EXPERTS_PER_GROUP.bit_length() - 1) == g_sel))
    el = jnp.where(in_grp, logits, -jnp.inf)
    v1 = jnp.max(el, axis=-1, keepdims=True)
    i1 = jnp.min(jnp.where(in_grp & (el == v1), lane, LANES), axis=-1, keepdims=True)
    el2 = jnp.where(lane == i1, -jnp.inf, el)
    v2 = jnp.max(el2, axis=-1, keepdims=True)
    i2 = jnp.min(jnp.where(in_grp & (lane != i1) & (el2 == v2), lane, LANES), axis=-1, keepdims=True)
    e2 = jnp.exp(v2 - v1)
    den = 1.0 + e2
    first = ROUTE_EXPERT_OFF + EXPERTS_PER_GROUP * g_sel
    return (jnp.where(lane == i1 - first, (1.0 / den) * p_group, 0.0)
            + jnp.where(lane == i2 - first, (e2 / den) * p_group, 0.0)
            + jnp.where(lane == META_GROUP_LANE, g_sel.astype(F32), 0.0))


def _moe_kernel(tgrp_ref, nval_ref, tok_ref, tok_next_ref, xa_ref, g_ref, wg_ref, wu_ref, wd_ref, out_ref,
                buf_ref, obuf_ref, sem_in, sem_out, *, tm, n_tiles):
    i = pl.program_id(0)
    slot = i % 2

    def row_in(tok_smem, r, s):
        return pltpu.make_async_copy(xa_ref.at[pl.ds(tok_smem[0, 0, r], 1)], buf_ref.at[s, pl.ds(r, 1)],
                                     sem_in.at[s])

    def row_out(r, s):
        return pltpu.make_async_copy(obuf_ref.at[s, pl.ds(r, 1)], out_ref.at[pl.ds(tok_ref[0, 0, r], 1)],
                                     sem_out.at[s])

    def start_gather(tok_smem, s):
        def body(r, carry):
            row_in(tok_smem, r, s).start()
            return carry
        lax.fori_loop(0, tm, body, 0, unroll=8)

    def wait_scatter(tile, s):
        n = nval_ref[tile]

        def chunk(c, carry):
            pltpu.make_async_copy(obuf_ref.at[s, pl.ds(0, WAIT_CHUNK)], out_ref.at[pl.ds(0, WAIT_CHUNK)],
                                  sem_out.at[s]).wait()
            return carry

        def row(r, carry):
            row_out(0, s).wait()
            return carry
        lax.fori_loop(0, jnp.right_shift(n, WAIT_CHUNK.bit_length() - 1), chunk, 0)
        lax.fori_loop(0, n & (WAIT_CHUNK - 1), row, 0)

    @pl.when(i == 0)
    def _():
        start_gather(tok_ref, 0)

    nxt = jnp.minimum(i + 1, n_tiles - 1)
    tile_in = pltpu.make_async_copy(xa_ref.at[pl.ds(0, tm)], buf_ref.at[slot], sem_in.at[slot])
    next_in = pltpu.make_async_copy(xa_ref.at[pl.ds(0, tm)], buf_ref.at[1 - slot], sem_in.at[1 - slot])
    n_seg = 2 * EXPERTS_PER_GROUP

    def prefetch_segment(seg):
        for r in range(seg * tm // n_seg, (seg + 1) * tm // n_seg):
            row_in(tok_next_ref, r, 1 - slot).start(priority=r % 2)

    @pl.when(i >= 2)
    def _():
        wait_scatter(i - 2, slot)

    @pl.when(nval_ref[i] > 0)
    def _():
        tile_in.wait()
        x = buf_ref[slot, :, 0:D_MODEL]
        meta = buf_ref[slot, :, D_MODEL:ROW_WIDTH]
        inv = lax.rsqrt(jnp.mean(x * x, axis=-1, keepdims=True) + RMS_EPS)
        h = (x * inv * g_ref[...]).astype(BF16)
        hids = []
        for e in range(EXPERTS_PER_GROUP):
            a = jnp.dot(h, wg_ref[e], preferred_element_type=F32)
            prefetch_segment(2 * e)
            up = jnp.dot(h, wu_ref[e], preferred_element_type=F32)
            prefetch_segment(2 * e + 1)
            hids.append(a * _sigmoid(a) * up * meta[:, e:e + 1])
        hid = jnp.concatenate(hids, axis=1).astype(BF16)
        obuf_ref[slot] = x + jnp.dot(hid, wd_ref[0], preferred_element_type=F32)

        n = nval_ref[i]
        n_main = n & ~(SCATTER_UNROLL - 1)

        def chunk(c, carry):
            for u in range(SCATTER_UNROLL):
                row_out(c * SCATTER_UNROLL + u, slot).start(priority=u % 2)
            return carry

        def row(r, carry):
            row_out(n_main + r, slot).start()
            return carry
        lax.fori_loop(0, jnp.right_shift(n, SCATTER_UNROLL.bit_length() - 1), chunk, 0)
        lax.fori_loop(0, n - n_main, row, 0)

        @pl.when((i + 1 >= n_tiles) | (nval_ref[nxt] == 0))
        def _():
            next_in.wait()

    @pl.when(i == n_tiles - 1)
    def _():
        @pl.when(i >= 1)
        def _():
            wait_scatter(i - 1, 1 - slot)
        wait_scatter(i, slot)


def _moe(xa, g, tgrp, nval, tok, wg, wu, wd, tm):
    t = xa.shape[0]
    n_tiles = tok.shape[0]
    hid_w = EXPERTS_PER_GROUP * D_EXPERT
    smem_tile = lambda off: pl.BlockSpec(
        (1, 1, tm), lambda i, tg, nv: (jnp.minimum(i + off, n_tiles - 1), 0, 0), memory_space=pltpu.SMEM)
    grid_spec = pltpu.PrefetchScalarGridSpec(
        num_scalar_prefetch=2,
        grid=(n_tiles,),
        in_specs=[
            smem_tile(0), smem_tile(1),
            pl.BlockSpec(memory_space=pl.ANY),
            pl.BlockSpec((1, D_MODEL), lambda i, tg, nv: (0, 0)),
            pl.BlockSpec((EXPERTS_PER_GROUP, D_MODEL, D_EXPERT), lambda i, tg, nv: (tg[i], 0, 0)),
            pl.BlockSpec((EXPERTS_PER_GROUP, D_MODEL, D_EXPERT), lambda i, tg, nv: (tg[i], 0, 0)),
            pl.BlockSpec((1, hid_w, D_MODEL), lambda i, tg, nv: (tg[i], 0, 0)),
        ],
        out_specs=pl.BlockSpec(memory_space=pl.ANY),
        scratch_shapes=[pltpu.VMEM((2, tm, ROW_WIDTH), F32), pltpu.VMEM((2, tm, D_MODEL), F32),
                        pltpu.SemaphoreType.DMA((2,)), pltpu.SemaphoreType.DMA((2,))],
    )
    return pl.pallas_call(
        functools.partial(_moe_kernel, tm=tm, n_tiles=n_tiles),
        grid_spec=grid_spec,
        out_shape=jax.ShapeDtypeStruct((t, D_MODEL), F32),
        compiler_params=_cparams(("arbitrary",)),
        name="moe",
    )(tgrp, nval, tok, tok, xa, g, wg, wu, wd)


WAIT_CHUNK = 64
SCATTER_UNROLL = 8


def _moe_plan(gid, tm):
    t = gid.shape[0]
    n_tiles = t // tm + N_EXPERT_GROUPS
    order = jnp.argsort(gid, stable=True).astype(jnp.int32)
    counts = jnp.sum((gid[:, None] == jnp.arange(N_EXPERT_GROUPS)[None, :]).astype(jnp.int32), axis=0)
    tiles_per = (counts + tm - 1) // tm
    tile_end = jnp.cumsum(tiles_per)
    row_start = jnp.cumsum(counts) - counts
    tile = jnp.arange(n_tiles, dtype=jnp.int32)
    tgrp = jnp.minimum(jnp.sum((tile[:, None] >= tile_end[None, :]).astype(jnp.int32), axis=1),
                       N_EXPERT_GROUPS - 1)
    local = tile - (tile_end - tiles_per)[tgrp]
    nval = jnp.clip(counts[tgrp] - local * tm, 0, tm)
    slot = jnp.arange(tm, dtype=jnp.int32)[None, :]
    src = row_start[tgrp][:, None] + local[:, None] * tm + jnp.where(slot < nval[:, None], slot, 0)
    tok = order[jnp.clip(src, 0, t - 1)]
    return tgrp.astype(jnp.int32), nval.astype(jnp.int32), tok.reshape(n_tiles, 1, tm)


def _rope_lane_freq():
    inv_freq = ROPE_THETA ** (-jnp.arange(0, ROPE_DIM, 2, dtype=F32) / ROPE_DIM)
    d = np.arange(LANES) % HEAD_DIM
    f = jnp.where(jnp.asarray(d < ROPE_DIM), inv_freq[np.asarray(d % (ROPE_DIM // 2))], 0.0)
    return f.reshape(1, LANES).astype(F32)


def _head_of_lane(width):
    return jnp.asarray((np.arange(width) // HEAD_DIM)[:, None] == np.arange(LANES)[None, :], dtype=BF16)


def _overlap_t(n_cmp_pad, n_blk):
    a0 = np.arange(n_cmp_pad)[:, None] * CMP_STRIDE
    b0 = np.arange(n_blk)[None, :] * SEL_BLOCK
    ov = np.clip(np.minimum(a0 + CMP_BLOCK, b0 + SEL_BLOCK) - np.maximum(a0, b0), 0, None) / CMP_BLOCK
    return jnp.asarray(ov.T, dtype=BF16)


def _bias_place(n_blk):
    return jnp.asarray(np.arange(n_blk)[:, None] + BLK_LANE0 == np.arange(LANES)[None, :], dtype=BF16)


def _layer(x, positions, attn_norm_g, w_in, q_norm_g, k_norm_cmp_g, k_norm_slc_g, k_norm_swa_g,
           cmp_pos_emb_k, cmp_w1_k, cmp_w2_k, cmp_pos_emb_v, cmp_w1_v, cmp_w2_v, w_head_out, w_pool,
           pool_scale, w_out, ffn_norm_g, w_router_group, b_router_group, w_router_expert,
           b_router_expert, w_expert_gate, w_expert_up, w_expert_down):
    b, s, d = x.shape
    t = b * s
    x2 = x.reshape(t, d)

    o_q, o_kv, o_g = ATTN_WIDTH, ATTN_WIDTH + 6 * KV_WIDTH, ATTN_WIDTH + 6 * KV_WIDTH + N_HEADS * N_BRANCH
    o_p = o_g + POOL_WIDTH
    used = 2 * ATTN_WIDTH + 2 * D_MODEL + 6 * KV_WIDTH + N_HEADS * N_BRANCH
    w_packed = jnp.concatenate(
        [w_in[:, :o_q], w_in[:, o_g:o_p], w_in[:, o_p:], w_in[:, o_q:o_kv], w_in[:, o_kv:o_g],
         jnp.zeros((d, PROJ_WIDTH - used), w_in.dtype)], axis=1).astype(BF16)
    proj = _proj(x2, attn_norm_g.reshape(1, d), w_packed)

    f_lane = _rope_lane_freq()
    tile_h = lambda g_, n: jnp.tile(g_, n).reshape(1, n * HEAD_DIM)
    q, ks, vs, kw, vw, kc_raw, vc_raw, gates = _prep(
        proj, positions.reshape(t, 1), f_lane, tile_h(q_norm_g, N_HEADS), tile_h(k_norm_slc_g, N_KV),
        tile_h(k_norm_swa_g, N_KV), _head_of_lane(ATTN_WIDTH), _head_of_lane(KV_WIDTH), b, s)

    n_row = s // CMP_STRIDE
    half = CMP_STRIDE * HEAD_DIM
    xk = kc_raw.reshape(b, N_KV, n_row, half)
    xv = vc_raw.reshape(b, N_KV, n_row, half)
    pad2 = lambda w2: jnp.pad(w2, ((0, 0), (0, LANES - HEAD_DIM))).astype(BF16)
    cmp_end = np.minimum(np.arange(n_row) * CMP_STRIDE + CMP_BLOCK - 1, s - 1)
    posc = positions[:, cmp_end].reshape(b, n_row, 1)
    gk = jnp.pad(k_norm_cmp_g, (0, LANES - HEAD_DIM)).reshape(1, LANES)
    kc, vc = _compress(xk, xv, cmp_pos_emb_k.reshape(2, half), cmp_pos_emb_v.reshape(2, half),
                       cmp_w1_k.astype(BF16), pad2(cmp_w2_k), cmp_w1_v.astype(BF16), pad2(cmp_w2_v),
                       gk, posc, f_lane)

    n_blk = s // SEL_BLOCK
    assert n_blk <= LANES - BLK_LANE0
    o = _attn(q, kc, vc, ks, vs, kw, vw, gates, _overlap_t(n_row, n_blk), _bias_place(n_blk))

    wr = jnp.concatenate([w_router_group, w_router_expert,
                          jnp.zeros((d, LANES - N_EXPERT_GROUPS - N_EXPERTS), F32)], axis=1)
    br = jnp.concatenate([b_router_group, b_router_expert,
                          jnp.zeros((LANES - N_EXPERT_GROUPS - N_EXPERTS,), F32)]).reshape(1, LANES)
    gf = ffn_norm_g.reshape(1, d)
    wr_hi = wr.astype(BF16)
    wr2 = jnp.stack([wr_hi, (wr - wr_hi.astype(F32)).astype(BF16)])
    xa = _mix(x2, proj, o, w_head_out.astype(BF16), w_pool.astype(BF16), pool_scale.reshape(1, d),
              w_out.astype(BF16), gf, wr2, br, s)

    tgrp, nval, tok = _moe_plan(xa[:, D_MODEL + META_GROUP_LANE].astype(jnp.int32), MOE_TM)
    wd = w_expert_down.reshape(N_EXPERT_GROUPS, EXPERTS_PER_GROUP * D_EXPERT, d)
    x_out = _moe(xa, gf, tgrp, nval, tok, w_expert_gate.astype(BF16), w_expert_up.astype(BF16),
                 wd.astype(BF16), MOE_TM)
    return x_out.reshape(b, s, d)


def kernel(x, positions, attn_norm_g, w_in, q_norm_g, k_norm_cmp_g, k_norm_slc_g, k_norm_swa_g,
           cmp_pos_emb_k, cmp_w1_k, cmp_w2_k, cmp_pos_emb_v, cmp_w1_v, cmp_w2_v, w_head_out, w_pool,
           pool_scale, w_out, ffn_norm_g, w_router_group, b_router_group, w_router_expert,
           b_router_expert, w_expert_gate, w_expert_up, w_expert_down):
    params = (attn_norm_g, w_in, q_norm_g, k_norm_cmp_g, k_norm_slc_g, k_norm_swa_g, cmp_pos_emb_k,
              cmp_w1_k, cmp_w2_k, cmp_pos_emb_v, cmp_w1_v, cmp_w2_v, w_head_out, w_pool, pool_scale,
              w_out, ffn_norm_g, w_router_group, b_router_group, w_router_expert, b_router_expert,
              w_expert_gate, w_expert_up, w_expert_down)
    for layer in range(attn_norm_g.shape[0]):
        x = _layer(x, positions, *[p[layer] for p in params])
    return x
```
